```python
import math
import jax, jax.numpy as jnp
from jax import lax
import numpy as np

D_MODEL = 1024
BATCH = 8
SEQ = 2048
DEPTH = 1
DEC_BATCH = 128
DEC_SEQ = 1
PAST_LEN = 16384
PAGE_SIZE = 128

A_WIDTH = D_MODEL
A_GROUPS = 8
A_GROUP_DIM = A_WIDTH // A_GROUPS
A_CHUNK = 128
B_DK = 128
B_DV = 128
B_HEADS = D_MODEL // B_DV
B_QK = B_HEADS * B_DK
B_V = B_HEADS * B_DV
C_QKV = 2 * B_QK + B_V
B_CONV = 4
B_CHUNK = 64
N_MEM = 256
M_HEADS = 4
M_HEAD_DIM = D_MODEL // M_HEADS
M_Q = M_HEADS * M_HEAD_DIM
N_GROUPS = 4
EXP_PER_GROUP = 8
N_EXPERTS = N_GROUPS * EXP_PER_GROUP
TOP_K = 2
D_EXPERT = 512
EPS = 1e-6
PROJ_SIZES = (A_WIDTH, A_WIDTH, C_QKV, B_V, B_HEADS, B_HEADS, M_Q, 3 * D_MODEL)
PROJ_WIDTH = 2 * A_WIDTH + C_QKV + B_V + 2 * B_HEADS + M_Q + 3 * D_MODEL

kernel_name = 'hybrid_gmlp_gdn_memxattn_hiermoe_step'

F32 = jnp.float32


def split_cols(t, sizes):
    out, start = [], 0
    for s in sizes:
        out.append(t[..., start:start + s])
        start += s
    return out


def rmsnorm(x, g):
    xf = x.astype(F32)
    y = xf * lax.rsqrt(jnp.mean(xf * xf, axis=-1, keepdims=True) + EPS) * g.astype(F32)
    return y.astype(x.dtype)


def l2norm(t):
    tf = t.astype(F32)
    return tf * lax.rsqrt(jnp.sum(tf * tf, axis=-1, keepdims=True) + EPS)


def chunk_spatial_gate(u, v, w_s, b_s):
    bsz, L, _ = v.shape
    pad = (-L) % A_CHUNK
    n = (L + pad) // A_CHUNK
    vc = jnp.pad(v, ((0, 0), (0, pad), (0, 0))).reshape(bsz, n, A_CHUNK, A_GROUPS, A_GROUP_DIM)
    causal = jnp.tril(jnp.ones((A_CHUNK, A_CHUNK), bool))
    ws = jnp.where(causal, w_s, 0.0).astype(v.dtype)
    sv = jnp.einsum('gij,bnjgc->bnigc', ws, vc) + b_s.T.astype(v.dtype)[None, None, :, :, None]
    sv = sv.reshape(bsz, n * A_CHUNK, A_WIDTH)[:, :L]
    return u * sv


def gated_delta_rule(q, k, v, g, beta, s0):
    bsz, L, H, _ = q.shape
    dv = v.shape[-1]
    C = math.gcd(B_CHUNK, L)
    n = L // C

    def chunks(t):
        return t.reshape(bsz, n, C, H, -1).transpose(1, 0, 3, 2, 4)

    def chunks_h(t):
        return t.reshape(bsz, n, C, H).transpose(1, 0, 3, 2)

    qc, kc, vc = chunks(q), chunks(k), chunks(v)
    gc, bc = chunks_h(g), chunks_h(beta)
    G = jnp.cumsum(gc, axis=-1)
    incl = jnp.tril(jnp.ones((C, C), bool))
    strict = jnp.tril(jnp.ones((C, C), bool), -1)
    decay = jnp.where(incl, jnp.exp(jnp.where(incl, G[..., :, None] - G[..., None, :], 0.0)), 0.0)
    kb = kc * bc[..., None]
    m = jnp.where(strict, jnp.einsum('nbhik,nbhjk->nbhij', kb, kc) * decay, 0.0)
    rhs = jnp.concatenate([vc * bc[..., None], kb * jnp.exp(G)[..., None]], axis=-1)
    sol = lax.linalg.triangular_solve(m + jnp.eye(C, dtype=m.dtype), rhs,
                                      left_side=True, lower=True, unit_diagonal=True)
    vt, wk = sol[..., :dv], sol[..., dv:]
    attn = jnp.einsum('nbhik,nbhjk->nbhij', qc, kc) * decay
    qg = qc * jnp.exp(G)[..., None]
    kd = kc * jnp.exp(G[..., -1:] - G)[..., None]
    gl = jnp.exp(G[..., -1])

    def step(S, xs):
        vt_i, wk_i, attn_i, qg_i, kd_i, gl_i = xs
        u = vt_i - jnp.einsum('bhck,bhkv->bhcv', wk_i, S)
        o = jnp.einsum('bhck,bhkv->bhcv', qg_i, S) + jnp.einsum('bhij,bhjv->bhiv', attn_i, u)
        S = S * gl_i[..., None, None] + jnp.einsum('bhck,bhcv->bhkv', kd_i, u)
        return S, o

    s_new, o = lax.scan(step, s0, (vt, wk, attn, qg, kd, gl))
    o = o.transpose(1, 0, 3, 2, 4).reshape(bsz, L, H, dv)
    return o, s_new


def memory_kv(mem, norm_mem, w_mem_kv):
    bsz = mem.shape[0]
    kv = rmsnorm(mem, norm_mem) @ w_mem_kv
    mk, mv = split_cols(kv, (M_Q, M_Q))
    return (mk.reshape(bsz, N_MEM, M_HEADS, M_HEAD_DIM), mv.reshape(bsz, N_MEM, M_HEADS, M_HEAD_DIM))


def memory_attend(q_m, mem_k, mem_v):
    bsz, L, _ = q_m.shape
    q = q_m.reshape(bsz, L, M_HEADS, M_HEAD_DIM)
    s = jnp.einsum('blhd,bmhd->bhlm', q, mem_k.astype(q.dtype)).astype(F32) * (M_HEAD_DIM ** -0.5)
    pr = jax.nn.softmax(s, axis=-1).astype(q.dtype)
    o = jnp.einsum('bhlm,bmhd->blhd', pr, mem_v.astype(q.dtype))
    return o.reshape(bsz, L, M_Q)


def hier_moe(h, p):
    bsz, L, D = h.shape
    t = h.reshape(-1, D)
    grp_logits = (t @ p['w_router_group'] + p['b_router_group']).astype(F32)
    grp_prob = jax.nn.softmax(grp_logits, axis=-1)
    g_idx = jnp.argmax(grp_logits, axis=-1)
    p_g = jnp.take_along_axis(grp_prob, g_idx[:, None], axis=-1)
    exp_logits = (t @ p['w_router_expert'] + p['b_router_expert']).astype(F32)
    exp_logits = exp_logits.reshape(-1, N_GROUPS, EXP_PER_GROUP)
    in_grp = jnp.take_along_axis(exp_logits, g_idx[:, None, None], axis=1)[:, 0]
    top_v, top_i = lax.top_k(in_grp, TOP_K)
    w_top = jax.nn.softmax(top_v, axis=-1) * p_g
    w_exp = jnp.sum(jax.nn.one_hot(top_i, EXP_PER_GROUP, dtype=F32) * w_top[..., None], axis=1)
    combine = (jax.nn.one_hot(g_idx, N_GROUPS, dtype=F32)[:, :, None] * w_exp[:, None, :]).astype(h.dtype)
    y = jnp.zeros_like(t)
    for gi in range(N_GROUPS):
        gate = jnp.einsum('td,edf->tef', t, p['w_exp_gate'][gi])
        up = jnp.einsum('td,edf->tef', t, p['w_exp_up'][gi])
        act = jax.nn.silu(gate) * up * combine[:, gi, :, None]
        y = y + jnp.einsum('tef,efd->td', act, p['w_exp_down'][gi])
    return y.reshape(bsz, L, D)


def hybrid_layer(x, mem_k, mem_v, s0, conv_buf, p):
    bsz, L, _ = x.shape
    h = rmsnorm(x, p['norm_mix'])
    proj = h @ p['w_in']
    a_u, a_v, qkv_pre, z, beta_l, alpha_l, q_m, gate_l = split_cols(proj, PROJ_SIZES)
    a_u = jax.nn.gelu(a_u)
    a_v = rmsnorm(jax.nn.gelu(a_v), p['norm_a_v'])
    o_a = chunk_spatial_gate(a_u, a_v, p['w_s'], p['b_s'])
    ext = jnp.concatenate([conv_buf.astype(qkv_pre.dtype), qkv_pre], axis=1)
    conv = sum(p['w_conv'][j] * ext[:, j:j + L] for j in range(B_CONV))
    new_buf = ext[:, L:]
    qkv = jax.nn.silu(conv)
    q, k, v = split_cols(qkv, (B_QK, B_QK, B_V))
    q = l2norm(q.reshape(bsz, L, B_HEADS, B_DK)) * (B_DK ** -0.5)
    k = l2norm(k.reshape(bsz, L, B_HEADS, B_DK))
    v = v.reshape(bsz, L, B_HEADS, B_DV).astype(F32)
    beta = jax.nn.sigmoid(beta_l.astype(F32))
    g = -jnp.exp(p['a_log'].astype(F32)) * jax.nn.softplus(alpha_l.astype(F32) + p['dt_bias'].astype(F32))
    o_b, s_new = gated_delta_rule(q, k, v, g, beta, s0.astype(F32))
    o_b = rmsnorm(o_b.astype(x.dtype), p['norm_gdn_out']) * jax.nn.silu(z.reshape(bsz, L, B_HEADS, B_DV))
    o_b = o_b.reshape(bsz, L, B_V)
    o_c = memory_attend(q_m, mem_k, mem_v)
    gates = jax.nn.sigmoid((gate_l + p['b_gate']).astype(F32)).astype(x.dtype)
    g_a, g_b, g_c = split_cols(gates, (D_MODEL, D_MODEL, D_MODEL))
    x = x + (g_a * o_a + g_b * o_b + g_c * o_c) @ p['w_o']
    x = x + hier_moe(rmsnorm(x, p['norm_ffn']), p)
    return x, s_new.astype(s0.dtype), new_buf, a_v


def setup_inputs(seed: int = 0) -> dict:
    key = jax.random.key(seed)
    ks = jax.random.split(key, 32)

    def nrm(k, shape, scale=1.0):
        return jax.random.normal(k, shape, F32) * scale

    def gain(k, shape):
        return 1.0 + 0.1 * jax.random.normal(k, shape, F32)

    dt = jnp.exp(jax.random.uniform(ks[10], (DEPTH, B_HEADS), F32, math.log(1e-3), math.log(1e-1)))
    return {
        'x_prompt': nrm(ks[0], (BATCH, SEQ, D_MODEL)),
        'x_sample': nrm(ks[1], (DEC_BATCH, DEC_SEQ, D_MODEL)),
        'mem_prompt': nrm(ks[2], (BATCH, N_MEM, D_MODEL)),
        'cache_mem_k': nrm(ks[3], (DEPTH, DEC_BATCH, N_MEM, M_HEADS, M_HEAD_DIM)),
        'cache_mem_v': nrm(ks[4], (DEPTH, DEC_BATCH, N_MEM, M_HEADS, M_HEAD_DIM)),
        'state_gdn': nrm(ks[5], (DEPTH, DEC_BATCH, B_HEADS, B_DK, B_DV), 0.3),
        'state_conv': nrm(ks[6], (DEPTH, DEC_BATCH, B_CONV - 1, C_QKV)),
        'norm_mix': gain(ks[7], (DEPTH, D_MODEL)),
        'w_in': nrm(ks[8], (DEPTH, D_MODEL, PROJ_WIDTH), D_MODEL ** -0.5),
        'b_gate': nrm(ks[9], (DEPTH, 3 * D_MODEL), 0.1),
        'w_s': nrm(ks[11], (DEPTH, A_GROUPS, A_CHUNK, A_CHUNK), A_CHUNK ** -0.5),
        'b_s': nrm(ks[12], (DEPTH, A_GROUPS, A_CHUNK), 0.1),
        'norm_a_v': gain(ks[13], (DEPTH, A_WIDTH)),
        'w_conv': nrm(ks[14], (DEPTH, B_CONV, C_QKV), B_CONV ** -0.5),
        'a_log': jnp.log(jax.random.uniform(ks[15], (DEPTH, B_HEADS), F32, 1.0, 16.0)),
        'dt_bias': dt + jnp.log(-jnp.expm1(-dt)),
        'norm_gdn_out': gain(ks[16], (DEPTH, B_DV)),
        'norm_mem': gain(ks[17], (DEPTH, D_MODEL)),
        'w_mem_kv': nrm(ks[18], (DEPTH, D_MODEL, 2 * M_Q), D_MODEL ** -0.5),
        'w_o': nrm(ks[19], (DEPTH, D_MODEL, D_MODEL), D_MODEL ** -0.5),
        'norm_ffn': gain(ks[20], (DEPTH, D_MODEL)),
        'w_router_group': nrm(ks[21], (DEPTH, D_MODEL, N_GROUPS), D_MODEL ** -0.5),
        'b_router_group': nrm(ks[22], (DEPTH, N_GROUPS), 0.01),
        'w_router_expert': nrm(ks[23], (DEPTH, D_MODEL, N_EXPERTS), D_MODEL ** -0.5),
        'b_router_expert': nrm(ks[24], (DEPTH, N_EXPERTS), 0.01),
        'w_exp_gate': nrm(ks[25], (DEPTH, N_GROUPS, EXP_PER_GROUP, D_MODEL, D_EXPERT), D_MODEL ** -0.5),
        'w_exp_up': nrm(ks[26], (DEPTH, N_GROUPS, EXP_PER_GROUP, D_MODEL, D_EXPERT), D_MODEL ** -0.5),
        'w_exp_down': nrm(ks[27], (DEPTH, N_GROUPS, EXP_PER_GROUP, D_EXPERT, D_MODEL), D_EXPERT ** -0.5),
        'norm_final': gain(ks[28], (D_MODEL,)),
    }


def reference(x_prompt, x_sample, mem_prompt, cache_mem_k, cache_mem_v, state_gdn, state_conv,
              norm_mix, w_in, b_gate, w_s, b_s, norm_a_v, w_conv, a_log, dt_bias, norm_gdn_out,
              norm_mem, w_mem_kv, w_o, norm_ffn, w_router_group, b_router_group, w_router_expert,
              b_router_expert, w_exp_gate, w_exp_up, w_exp_down, norm_final):
    bp = x_prompt.shape[0]
    y_p, y_s = x_prompt, x_sample
    mk_list, mv_list, sp_list, cp_list, ss_list, cs_list, vs_list = [], [], [], [], [], [], []
    for l in range(DEPTH):
        p = dict(norm_mix=norm_mix[l], w_in=w_in[l], b_gate=b_gate[l], w_s=w_s[l], b_s=b_s[l],
                 norm_a_v=norm_a_v[l], w_conv=w_conv[l], a_log=a_log[l], dt_bias=dt_bias[l],
                 norm_gdn_out=norm_gdn_out[l], w_o=w_o[l], norm_ffn=norm_ffn[l],
                 w_router_group=w_router_group[l], b_router_group=b_router_group[l],
                 w_router_expert=w_router_expert[l], b_router_expert=b_router_expert[l],
                 w_exp_gate=w_exp_gate[l], w_exp_up=w_exp_up[l], w_exp_down=w_exp_down[l])
        mk_p, mv_p = memory_kv(mem_prompt, norm_mem[l], w_mem_kv[l])
        s0_p = jnp.zeros((bp, B_HEADS, B_DK, B_DV), x_prompt.dtype)
        buf_p = jnp.zeros((bp, B_CONV - 1, C_QKV), x_prompt.dtype)
        y_p, s_p, c_p, _ = hybrid_layer(y_p, mk_p, mv_p, s0_p, buf_p, p)
        y_s, s_s, c_s, v_s = hybrid_layer(y_s, cache_mem_k[l], cache_mem_v[l], state_gdn[l], state_conv[l], p)
        mk_list.append(mk_p)
        mv_list.append(mv_p)
        sp_list.append(s_p)
        cp_list.append(c_p)
        ss_list.append(s_s)
        cs_list.append(c_s)
        vs_list.append(v_s)
    y_prompt = rmsnorm(y_p, norm_final)
    y_sample = rmsnorm(y_s, norm_final)
    return (y_prompt, y_sample, jnp.stack(mk_list), jnp.stack(mv_list), jnp.stack(sp_list),
            jnp.stack(cp_list), jnp.stack(ss_list), jnp.stack(cs_list), jnp.stack(vs_list))
```

```python
import functools
import math

import jax
import jax.numpy as jnp
from jax import lax
from jax.experimental import pallas as pl
from jax.experimental.pallas import tpu as pltpu

F32 = jnp.float32
BF16 = jnp.bfloat16

D_MODEL = 1024
A_GROUPS = 8
A_CHUNK = 128
B_HEADS = 8
B_DK = 128
C_QKV = 3 * D_MODEL
B_CONV = 4
N_MEM = 256
M_HEADS = 4
M_HEAD_DIM = 256
N_GROUPS = 4
EXP_PER_GROUP = 8
N_EXPERTS = 32
D_EXPERT = 512
EPS = 1e-6

LANES = 128
GDN_CHUNK = 128
MIX_TILE = 256
PAGE_ROWS = 16
TILE_PAGES = 64
TILE_SLOTS = PAGE_ROWS * TILE_PAGES
STEP_PAGES = 16
STEP_ROWS = PAGE_ROWS * STEP_PAGES
PROJ_COLS = 10 * D_MODEL
VMEM_LIMIT = 56 * 1024 * 1024


def _params(n_grid):
    return pltpu.CompilerParams(dimension_semantics=("arbitrary",) * n_grid,
                                vmem_limit_bytes=VMEM_LIMIT)


def _rms(x, g):
    return x * lax.rsqrt(jnp.mean(x * x, axis=-1, keepdims=True) + EPS) * g


def _dot(a, b):
    return jnp.dot(a.astype(BF16), b.astype(BF16), preferred_element_type=F32)


def _dot_nt(a, b):
    return lax.dot_general(a.astype(BF16), b.astype(BF16), (((1,), (1,)), ((), ())),
                           preferred_element_type=F32)


def _dot_tn(a, b):
    return lax.dot_general(a.astype(BF16), b.astype(BF16), (((0,), (0,)), ((), ())),
                           preferred_element_type=F32)


def _split3(x):
    hi = x.astype(BF16)
    r = x - hi.astype(F32)
    mid = r.astype(BF16)
    lo = (r - mid.astype(F32)).astype(BF16)
    return hi, mid, lo


def _iota(shape, axis):
    return lax.broadcasted_iota(jnp.int32, shape, axis)


def _memkv_kernel(mem_ref, g_ref, w_ref, k_ref, v_ref):
    h = _rms(mem_ref[...], g_ref[...]).astype(BF16)
    kv = jnp.dot(h, w_ref[...], preferred_element_type=F32)
    k_ref[...] = kv[:, :D_MODEL]
    v_ref[...] = kv[:, D_MODEL:]


def _memkv(mem2d, norm_mem, w_kv_bf16):
    rows = mem2d.shape[0]
    nb = rows // N_MEM
    return pl.pallas_call(
        _memkv_kernel,
        grid=(nb,),
        in_specs=[pl.BlockSpec((N_MEM, D_MODEL), lambda i: (i, 0)),
                  pl.BlockSpec((1, D_MODEL), lambda i: (0, 0)),
                  pl.BlockSpec((D_MODEL, 2 * D_MODEL), lambda i: (0, 0))],
        out_specs=[pl.BlockSpec((N_MEM, D_MODEL), lambda i: (i, 0)),
                   pl.BlockSpec((N_MEM, D_MODEL), lambda i: (i, 0))],
        out_shape=[jax.ShapeDtypeStruct((rows, D_MODEL), F32)] * 2,
        compiler_params=_params(1),
        name="memkv",
    )(mem2d, norm_mem, w_kv_bf16)


def _proj_kernel(x_ref, nm_ref, w_ref, wba_ref, nav_ref, bg_ref, p_ref, ba_ref, hn_ref):
    j = pl.program_id(1)

    @pl.when(j == 0)
    def _():
        hn = _rms(x_ref[...], nm_ref[...]).astype(BF16)
        hn_ref[...] = hn
        ba_ref[...] = jnp.dot(hn, wba_ref[...], preferred_element_type=F32)

    acc = jnp.dot(hn_ref[...], w_ref[...], preferred_element_type=F32)

    @pl.when(j == 0)
    def _():
        p_ref[...] = jax.nn.gelu(acc)

    @pl.when(j == 1)
    def _():
        p_ref[...] = _rms(jax.nn.gelu(acc), nav_ref[...])

    @pl.when((j >= 2) & (j <= 6))
    def _():
        p_ref[...] = acc

    @pl.when(j >= 7)
    def _():
        p_ref[...] = jax.nn.sigmoid(acc + bg_ref[...])


def _proj(x2d, norm_mix, w_bf16, wba_bf16, norm_a_v, b_gate, tm):
    rows = x2d.shape[0]
    ncol = PROJ_COLS // D_MODEL
    return pl.pallas_call(
        _proj_kernel,
        grid=(rows // tm, ncol),
        in_specs=[pl.BlockSpec((tm, D_MODEL), lambda i, j: (i, 0)),
                  pl.BlockSpec((1, D_MODEL), lambda i, j: (0, 0)),
                  pl.BlockSpec((D_MODEL, D_MODEL), lambda i, j: (0, j)),
                  pl.BlockSpec((D_MODEL, LANES), lambda i, j: (0, 0)),
                  pl.BlockSpec((1, D_MODEL), lambda i, j: (0, 0)),
                  pl.BlockSpec((1, D_MODEL), lambda i, j: (0, jnp.maximum(j - 7, 0)))],
        out_specs=[pl.BlockSpec((tm, D_MODEL), lambda i, j: (i, j)),
                   pl.BlockSpec((tm, LANES), lambda i, j: (i, 0))],
        out_shape=[jax.ShapeDtypeStruct((rows, PROJ_COLS), F32),
                   jax.ShapeDtypeStruct((rows, LANES), F32)],
        scratch_shapes=[pltpu.VMEM((tm, D_MODEL), BF16)],
        compiler_params=_params(2),
        name="proj",
    )(x2d, norm_mix, w_bf16, wba_bf16, norm_a_v, b_gate)


def _gate_terms(ba, alog_row, dt_row):
    beta = jax.nn.sigmoid(ba)
    g = -jnp.exp(alog_row) * jax.nn.softplus(ba + dt_row)
    return beta, g


def _gdn_kernel(q_ref, k_ref, v_ref, z_ref, ba_ref, wc_ref, alog_ref, dt_ref, ng_ref,
                ob_ref, s_out_ref, s_ref, ext_ref):
    c = pl.program_id(1)
    n_c = pl.num_programs(1)
    C = GDN_CHUNK

    @pl.when(c == 0)
    def _():
        s_ref[...] = jnp.zeros_like(s_ref)
        ext_ref[0:8, :] = jnp.zeros((8, C_QKV), F32)

    ext_ref[8:8 + C, 0:D_MODEL] = q_ref[...]
    ext_ref[8:8 + C, D_MODEL:2 * D_MODEL] = k_ref[...]
    ext_ref[8:8 + C, 2 * D_MODEL:3 * D_MODEL] = v_ref[...]

    beta_all, g_all = _gate_terms(ba_ref[...], alog_ref[...], dt_ref[...])
    row = _iota((C, C), 0)
    col = _iota((C, C), 1)
    incl = row >= col
    strict = row > col
    eye = (row == col).astype(F32)
    same_blk = [(row // s) == (col // s) for s in (8, 16, 32, 64, 128)]
    tri = incl.astype(BF16)
    g_hi, g_mid, g_lo = _split3(g_all)
    gcum = (jnp.dot(tri, g_hi, preferred_element_type=F32)
            + jnp.dot(tri, g_mid, preferred_element_type=F32)
            + jnp.dot(tri, g_lo, preferred_element_type=F32))
    gcum_t = gcum.T

    def conv_silu(lo):
        acc = wc_ref[3:4, lo:lo + B_DK] * ext_ref[8:8 + C, lo:lo + B_DK]
        for j in range(1, B_CONV):
            acc = acc + wc_ref[3 - j:4 - j, lo:lo + B_DK] * ext_ref[8 - j:8 - j + C, lo:lo + B_DK]
        return jax.nn.silu(acc)

    def l2n(t):
        return t * lax.rsqrt(jnp.sum(t * t, axis=-1, keepdims=True) + EPS)

    for h in range(B_HEADS):
        qh = l2n(conv_silu(h * B_DK)) * (B_DK ** -0.5)
        kh = l2n(conv_silu(D_MODEL + h * B_DK))
        vh = conv_silu(2 * D_MODEL + h * B_DK)
        bcol = beta_all[:, h:h + 1]
        gcol = gcum[:, 8 + h:9 + h]
        grow = gcum_t[8 + h:9 + h, :]
        glast = gcum[C - 1:C, 8 + h:9 + h]
        eg = jnp.exp(gcol)
        dec = jnp.where(incl, jnp.exp(jnp.where(incl, gcol - grow, 0.0)), 0.0)
        kb = kh * bcol
        a1 = _dot_nt(jnp.concatenate([kb, qh], axis=0), kh)
        m = jnp.where(strict, a1[:C] * dec, 0.0)
        attn = a1[C:] * dec
        m0 = jnp.where(same_blk[0], m, 0.0)
        x = eye - m0
        pw = _dot(m0, m0)
        x = x + _dot(x, pw)
        x = x + _dot(x, _dot(pw, pw))
        for lvl in range(1, len(same_blk)):
            off = jnp.where(same_blk[lvl] & jnp.logical_not(same_blk[lvl - 1]), m, 0.0)
            x = x - _dot(_dot(x, off), x)
        rhs = jnp.concatenate([vh * bcol, kb * eg], axis=1)
        sol = _dot(x, rhs)
        vt = sol[:, :B_DK]
        wk = sol[:, B_DK:]
        s_h = s_ref[h]
        b1 = _dot(jnp.concatenate([wk, qh * eg], axis=0), s_h)
        u = vt - b1[:C]
        o = b1[C:] + _dot(attn, u)
        kd = kh * jnp.exp(glast - gcol)
        s_new = s_h * jnp.exp(glast) + _dot_tn(kd, u)
        s_ref[h] = s_new
        zh = z_ref[:, h * B_DK:(h + 1) * B_DK]
        ob_ref[:, h * B_DK:(h + 1) * B_DK] = _rms(o, ng_ref[...]) * jax.nn.silu(zh)

    ext_ref[0:8, :] = ext_ref[C:C + 8, :]

    @pl.when(c == n_c - 1)
    def _():
        s_out_ref[...] = s_ref[...]


def _gdn(p, ba, w_conv, alog_row, dt_row, norm_gdn, bsz, seq):
    nc = seq // GDN_CHUNK
    blk = lambda colblk: pl.BlockSpec((GDN_CHUNK, D_MODEL), lambda b, c: (b * nc + c, colblk))
    small = lambda shape: pl.BlockSpec(shape, lambda b, c: (0, 0))
    return pl.pallas_call(
        _gdn_kernel,
        grid=(bsz, nc),
        in_specs=[blk(2), blk(3), blk(4), blk(5),
                  pl.BlockSpec((GDN_CHUNK, LANES), lambda b, c: (b * nc + c, 0)),
                  small((B_CONV, C_QKV)), small((1, LANES)), small((1, LANES)), small((1, B_DK))],
        out_specs=[pl.BlockSpec((GDN_CHUNK, D_MODEL), lambda b, c: (b * nc + c, 0)),
                   pl.BlockSpec((None, B_HEADS, B_DK, B_DK), lambda b, c: (b, 0, 0, 0))],
        out_shape=[jax.ShapeDtypeStruct((bsz * seq, D_MODEL), F32),
                   jax.ShapeDtypeStruct((bsz, B_HEADS, B_DK, B_DK), F32)],
        scratch_shapes=[pltpu.VMEM((B_HEADS, B_DK, B_DK), F32),
                        pltpu.VMEM((8 + GDN_CHUNK, C_QKV), F32)],
        compiler_params=_params(2),
        name="gdn",
    )(p, p, p, p, ba, w_conv, alog_row, dt_row, norm_gdn)


GS_REQ = 8


def _gdn_step_kernel(q_ref, k_ref, v_ref, z_ref, ba_ref, sc_ref, sg_ref, wc_ref, alog_ref, dt_ref,
                     ng_ref, ob_ref, snew_ref, cnew_ref):
    beta_all, g_all = _gate_terms(ba_ref[...], alog_ref[...], dt_ref[...])
    eg_all = jnp.exp(g_all)
    pre = jnp.concatenate([q_ref[...], k_ref[...], v_ref[...]], axis=1)
    conv = wc_ref[3:4, :] * pre
    for j in range(B_CONV - 1):
        conv = conv + wc_ref[j:j + 1, :] * sc_ref[:, j, :]
    cnew_ref[:, 0, :] = sc_ref[:, 1, :]
    cnew_ref[:, 1, :] = sc_ref[:, 2, :]
    cnew_ref[:, 2, :] = pre
    qkv = jax.nn.silu(conv)
    zero_rows = jnp.zeros((LANES - GS_REQ, B_DK), F32)

    def l2n(t):
        return t * lax.rsqrt(jnp.sum(t * t, axis=-1, keepdims=True) + EPS)

    for h in range(B_HEADS):
        qh = l2n(qkv[:, h * B_DK:(h + 1) * B_DK]) * (B_DK ** -0.5)
        kh = l2n(qkv[:, D_MODEL + h * B_DK:D_MODEL + (h + 1) * B_DK])
        vh = qkv[:, 2 * D_MODEL + h * B_DK:2 * D_MODEL + (h + 1) * B_DK]
        q_t = jnp.concatenate([qh, zero_rows], axis=0).T
        k_t = jnp.concatenate([kh, zero_rows], axis=0).T
        o_rows = []
        for r in range(GS_REQ):
            kcol = k_t[:, r:r + 1]
            qcol = q_t[:, r:r + 1]
            s_old = sg_ref[r, h]
            beta = beta_all[r:r + 1, h:h + 1]
            eg = eg_all[r:r + 1, 8 + h:9 + h]
            ks = jnp.sum(s_old * kcol, axis=0, keepdims=True)
            u = beta * (vh[r:r + 1, :] - eg * ks)
            s_new = s_old * eg + kcol * u
            snew_ref[r, h] = s_new
            o_rows.append(jnp.sum(s_new * qcol, axis=0, keepdims=True))
        o = jnp.concatenate(o_rows, axis=0)
        zh = z_ref[:, h * B_DK:(h + 1) * B_DK]
        ob_ref[:, h * B_DK:(h + 1) * B_DK] = _rms(o, ng_ref[...]) * jax.nn.silu(zh)


def _gdn_step(p, ba, state_conv, state_gdn, w_conv, alog_row, dt_row, norm_gdn):
    n = p.shape[0]
    blk = lambda colblk: pl.BlockSpec((GS_REQ, D_MODEL), lambda i: (i, colblk))
    small = lambda shape: pl.BlockSpec(shape, lambda i: (0, 0))
    return pl.pallas_call(
        _gdn_step_kernel,
        grid=(n // GS_REQ,),
        in_specs=[blk(2), blk(3), blk(4), blk(5),
                  pl.BlockSpec((GS_REQ, LANES), lambda i: (i, 0)),
                  pl.BlockSpec((GS_REQ, B_CONV - 1, C_QKV), lambda i: (i, 0, 0)),
                  pl.BlockSpec((GS_REQ, B_HEADS, B_DK, B_DK), lambda i: (i, 0, 0, 0)),
                  small((B_CONV, C_QKV)), small((1, LANES)), small((1, LANES)), small((1, B_DK))],
        out_specs=[pl.BlockSpec((GS_REQ, D_MODEL), lambda i: (i, 0)),
                   pl.BlockSpec((GS_REQ, B_HEADS, B_DK, B_DK), lambda i: (i, 0, 0, 0)),
                   pl.BlockSpec((GS_REQ, B_CONV - 1, C_QKV), lambda i: (i, 0, 0))],
        out_shape=[jax.ShapeDtypeStruct((n, D_MODEL), F32),
                   jax.ShapeDtypeStruct((n, B_HEADS, B_DK, B_DK), F32),
                   jax.ShapeDtypeStruct((n, B_CONV - 1, C_QKV), F32)],
        compiler_params=_params(1),
        name="gdn_step",
    )(p, p, p, p, ba, state_conv, state_gdn, w_conv, alog_row, dt_row, norm_gdn)


XA_REQ = 8


def _xattn1_kernel(q_ref, ck_ref, cv_ref, e_ref, et_ref, oc_ref):
    scale = M_HEAD_DIM ** -0.5

    def body(r, carry):
        q = q_ref[pl.ds(r, 1), :]
        prod = ck_ref[r] * q
        hi = prod.astype(BF16)
        lo = (prod - hi.astype(F32)).astype(BF16)
        s = (jnp.dot(hi, e_ref[...], preferred_element_type=F32)
             + jnp.dot(lo, e_ref[...], preferred_element_type=F32)) * scale
        mx = jnp.max(s, axis=0, keepdims=True)
        p = jnp.exp(s - mx)
        pr = p / jnp.sum(p, axis=0, keepdims=True)
        pe = jnp.dot(pr.astype(BF16), et_ref[...], preferred_element_type=F32)
        oc_ref[pl.ds(r, 1), :] = jnp.sum(pe * cv_ref[r], axis=0, keepdims=True)
        return carry

    lax.fori_loop(0, XA_REQ, body, 0)


def _xattn1(p, cache_k, cache_v, e_mat, et_mat):
    n = p.shape[0]
    return pl.pallas_call(
        _xattn1_kernel,
        grid=(n // XA_REQ,),
        in_specs=[pl.BlockSpec((XA_REQ, D_MODEL), lambda i: (i, 6)),
                  pl.BlockSpec((XA_REQ, N_MEM, D_MODEL), lambda i: (i, 0, 0)),
                  pl.BlockSpec((XA_REQ, N_MEM, D_MODEL), lambda i: (i, 0, 0)),
                  pl.BlockSpec((D_MODEL, LANES), lambda i: (0, 0)),
                  pl.BlockSpec((LANES, D_MODEL), lambda i: (0, 0))],
        out_specs=pl.BlockSpec((XA_REQ, D_MODEL), lambda i: (i, 0)),
        out_shape=jax.ShapeDtypeStruct((n, D_MODEL), F32),
        compiler_params=_params(1),
        name="xattn1",
    )(p, cache_k, cache_v, e_mat, et_mat)


def _route_and_dispatch(x1, nf_ref, wrt_ref, br_ref, xs_ref, route_ref, cnt_ref):
    tt = x1.shape[0]
    h2 = _rms(x1, nf_ref[...])
    h_hi, h_mid, h_lo = _split3(h2)
    w_hi, w_mid, w_lo = _split3(wrt_ref[...])

    def nt(a, b):
        return lax.dot_general(a, b, (((1,), (1,)), ((), ())), preferred_element_type=F32)

    logits = (nt(w_hi, h_hi) + (nt(w_hi, h_mid) + nt(w_mid, h_hi))
              + (nt(w_hi, h_lo) + nt(w_lo, h_hi) + nt(w_mid, h_mid))) + br_ref[:, 0:1]
    e_id = _iota((LANES, tt), 0).astype(F32)
    neg = jnp.float32(-jnp.inf)
    big = jnp.float32(1 << 20)
    is_grp = (e_id >= N_EXPERTS) & (e_id < N_EXPERTS + N_GROUPS)
    gl = jnp.where(is_grp, logits, neg)
    gmax = jnp.max(gl, axis=0, keepdims=True)
    g_lo = (jnp.min(jnp.where(gl == gmax, e_id, big), axis=0, keepdims=True) - N_EXPERTS) * EXP_PER_GROUP
    p_g = 1.0 / jnp.sum(jnp.exp(gl - gmax), axis=0, keepdims=True)
    in_grp = (e_id >= g_lo) & (e_id < g_lo + EXP_PER_GROUP)
    el = jnp.where(in_grp, logits, neg)
    v1 = jnp.max(el, axis=0, keepdims=True)
    i1 = jnp.min(jnp.where(el == v1, e_id, big), axis=0, keepdims=True)
    el2 = jnp.where(e_id == i1, neg, el)
    v2 = jnp.max(el2, axis=0, keepdims=True)
    i2 = jnp.min(jnp.where(el2 == v2, e_id, big), axis=0, keepdims=True)
    e2 = jnp.exp(v2 - v1)
    w1 = p_g / (1.0 + e2)
    w2 = p_g * e2 / (1.0 + e2)
    oh1 = (e_id == i1).astype(F32)
    oh2 = (e_id == i2).astype(F32)
    a_t = oh1 + oh2
    cnt = jnp.sum(a_t, axis=1, keepdims=True)
    pages = jnp.floor((cnt + (PAGE_ROWS - 1)) * (1.0 / PAGE_ROWS))
    lower = (_iota((LANES, LANES), 0) > _iota((LANES, LANES), 1)).astype(BF16)
    offp = jnp.dot(lower, jnp.broadcast_to(pages, (LANES, LANES)).astype(BF16),
                   preferred_element_type=F32)[:, 0:1]
    upper = (_iota((tt, tt), 0) < _iota((tt, tt), 1)).astype(BF16)
    rank = jnp.dot(a_t.astype(BF16), upper, preferred_element_type=F32)
    pos = offp * PAGE_ROWS + rank
    slot1 = jnp.sum(oh1 * pos, axis=0, keepdims=True)
    slot2 = jnp.sum(oh2 * pos, axis=0, keepdims=True)
    s_id = _iota((TILE_SLOTS, tt), 0).astype(F32)
    sel = jnp.where((s_id == slot1) | (s_id == slot2), 1.0, 0.0).astype(BF16)
    xs_ref[...] = jnp.dot(sel, h_hi, preferred_element_type=F32).astype(BF16)
    r_id = _iota((LANES, tt), 0)
    rows = (jnp.where(r_id == 0, slot1, 0.0) + jnp.where(r_id == 1, slot2, 0.0)
            + jnp.where(r_id == 2, w1, 0.0) + jnp.where(r_id == 3, w2, 0.0))
    route_ref[...] = rows.T
    cnt_ref[...] = jnp.broadcast_to(cnt, (LANES, LANES)).T[0:8, :]


def _merge(x, oa, ob, oc, ga, gb, gc, wo_ref):
    s = ga * oa + gb * ob + gc * oc
    return x + jnp.dot(s.astype(BF16), wo_ref[...], preferred_element_type=F32)


def _mix_kernel(x_ref, au_ref, av_ref, qm_ref, ga_ref, gb_ref, gc_ref, ob_ref, mk_ref, mv_ref,
                ws_ref, bst_ref, wo_ref, nf_ref, wrt_ref, br_ref,
                x1_ref, xs_ref, route_ref, cnt_ref, oa_ref, oc_ref):
    tt = x_ref.shape[0]
    causal = _iota((A_CHUNK, A_CHUNK), 0) >= _iota((A_CHUNK, A_CHUNK), 1)
    for g in range(A_GROUPS):
        wsg = jnp.where(causal, ws_ref[g], 0.0).astype(BF16)
        bcol = bst_ref[:, g:g + 1]
        lo = g * A_CHUNK
        for cc in range(tt // A_CHUNK):
            r0 = cc * A_CHUNK
            sv = jnp.dot(wsg, av_ref[r0:r0 + A_CHUNK, lo:lo + A_CHUNK].astype(BF16),
                         preferred_element_type=F32) + bcol
            oa_ref[r0:r0 + A_CHUNK, lo:lo + A_CHUNK] = au_ref[r0:r0 + A_CHUNK, lo:lo + A_CHUNK] * sv
    scale = M_HEAD_DIM ** -0.5
    for h in range(M_HEADS):
        lo = h * M_HEAD_DIM
        s = _dot_nt(qm_ref[:, lo:lo + M_HEAD_DIM], mk_ref[:, lo:lo + M_HEAD_DIM]) * scale
        p = jnp.exp(s - jnp.max(s, axis=-1, keepdims=True))
        pr = p / jnp.sum(p, axis=-1, keepdims=True)
        oc_ref[:, lo:lo + M_HEAD_DIM] = _dot(pr, mv_ref[:, lo:lo + M_HEAD_DIM])
    x1 = _merge(x_ref[...], oa_ref[...], ob_ref[...], oc_ref[...],
                ga_ref[...], gb_ref[...], gc_ref[...], wo_ref)
    x1_ref[...] = x1
    _route_and_dispatch(x1, nf_ref, wrt_ref, br_ref, xs_ref, route_ref, cnt_ref)


def _mix1_kernel(x_ref, au_ref, av_ref, ga_ref, gb_ref, gc_ref, ob_ref, oc_ref, ws0_ref, bs0_ref,
                 wo_ref, nf_ref, wrt_ref, br_ref, xs_in_ref,
                 x1_ref, xs_ref, route_ref, cnt_ref):
    del xs_in_ref
    oa = au_ref[...] * (ws0_ref[...] * av_ref[...] + bs0_ref[...])
    x1 = _merge(x_ref[...], oa, ob_ref[...], oc_ref[...], ga_ref[...], gb_ref[...], gc_ref[...], wo_ref)
    x1_ref[...] = x1
    _route_and_dispatch(x1, nf_ref, wrt_ref, br_ref, xs_ref, route_ref, cnt_ref)


def _route_out_specs(tt, tile_of):
    return [pl.BlockSpec((tt, D_MODEL), lambda i: (i, 0)),
            pl.BlockSpec((None, TILE_SLOTS, D_MODEL), lambda i: (tile_of(i), 0, 0)),
            pl.BlockSpec((tt, LANES), lambda i: (i, 0)),
            pl.BlockSpec((None, 8, LANES), lambda i: (i, 0, 0))]


def _mix(x2d, p, ob, mk, mv, w_s, bs_t, wo_bf16, norm_ffn, wr_t, br_col, seq, n_tiles_total):
    rows = x2d.shape[0]
    tt = MIX_TILE
    nt = rows // tt
    per_b = seq // tt
    pblk = lambda colblk: pl.BlockSpec((tt, D_MODEL), lambda i: (i, colblk))
    const = lambda shape: pl.BlockSpec(shape, lambda i: (0,) * len(shape))
    return pl.pallas_call(
        _mix_kernel,
        grid=(nt,),
        in_specs=[pblk(0), pblk(0), pblk(1), pblk(6), pblk(7), pblk(8), pblk(9), pblk(0),
                  pl.BlockSpec((N_MEM, D_MODEL), lambda i: (i // per_b, 0)),
                  pl.BlockSpec((N_MEM, D_MODEL), lambda i: (i // per_b, 0)),
                  const((A_GROUPS, A_CHUNK, A_CHUNK)), const((A_CHUNK, LANES)),
                  const((D_MODEL, D_MODEL)), const((1, D_MODEL)),
                  const((LANES, D_MODEL)), const((LANES, LANES))],
        out_specs=_route_out_specs(tt, lambda i: i),
        out_shape=[jax.ShapeDtypeStruct((rows, D_MODEL), F32),
                   jax.ShapeDtypeStruct((n_tiles_total, TILE_SLOTS, D_MODEL), BF16),
                   jax.ShapeDtypeStruct((rows, LANES), F32),
                   jax.ShapeDtypeStruct((nt, 8, LANES), F32)],
        scratch_shapes=[pltpu.VMEM((tt, D_MODEL), F32), pltpu.VMEM((tt, D_MODEL), F32)],
        compiler_params=_params(1),
        name="mix",
    )(x2d, p, p, p, p, p, p, ob, mk, mv, w_s, bs_t, wo_bf16, norm_ffn, wr_t, br_col)


def _mix1(x2d, p, ob, oc, ws0_row, bs0_row, wo_bf16, norm_ffn, wr_t, br_col, xs_all, tile_idx):
    rows = x2d.shape[0]
    pblk = lambda colblk: pl.BlockSpec((rows, D_MODEL), lambda i: (0, colblk))
    const = lambda shape: pl.BlockSpec(shape, lambda i: (0,) * len(shape))
    return pl.pallas_call(
        _mix1_kernel,
        grid=(1,),
        in_specs=[pblk(0), pblk(0), pblk(1), pblk(7), pblk(8), pblk(9), pblk(0), pblk(0),
                  const((1, D_MODEL)), const((1, D_MODEL)),
                  const((D_MODEL, D_MODEL)), const((1, D_MODEL)),
                  const((LANES, D_MODEL)), const((LANES, LANES)),
                  pl.BlockSpec(memory_space=pl.ANY)],
        out_specs=_route_out_specs(rows, lambda i: tile_idx),
        out_shape=[jax.ShapeDtypeStruct((rows, D_MODEL), F32),
                   jax.ShapeDtypeStruct(xs_all.shape, BF16),
                   jax.ShapeDtypeStruct((rows, LANES), F32),
                   jax.ShapeDtypeStruct((1, 8, LANES), F32)],
        input_output_aliases={14: 1},
        compiler_params=_params(1),
        name="mix1",
    )(x2d, p, p, p, p, p, ob, oc, ws0_row, bs0_row, wo_bf16, norm_ffn, wr_t, br_col, xs_all)


def _experts_kernel(tbl_ref, se_ref, nu_ref, *refs):
    del tbl_ref
    pages = refs[:STEP_PAGES]
    wg_ref, wu_ref, wd_ref, o_ref, wgb_ref, wub_ref, wdb_ref = refs[STEP_PAGES:]
    s = pl.program_id(0)
    prev = se_ref[jnp.maximum(s - 1, 0)]

    @pl.when((s == 0) | (se_ref[s] != prev))
    def _():
        wgb_ref[...] = wg_ref[...].astype(BF16)
        wub_ref[...] = wu_ref[...].astype(BF16)
        wdb_ref[...] = wd_ref[...].astype(BF16)

    @pl.when(s < nu_ref[0])
    def _():
        x = jnp.concatenate([pg[...] for pg in pages], axis=0)
        gate = jnp.dot(x, wgb_ref[...], preferred_element_type=F32)
        up = jnp.dot(x, wub_ref[...], preferred_element_type=F32)
        act = (jax.nn.silu(gate) * up).astype(BF16)
        o_ref[...] = jnp.dot(act, wdb_ref[...], preferred_element_type=F32).astype(BF16)

    @pl.when(s >= nu_ref[0])
    def _():
        o_ref[...] = jnp.zeros_like(o_ref)


def _experts(tbl, step_e, n_used, xs_pages, w_gate, w_up, w_down, n_steps):
    def page_spec(k):
        return pl.BlockSpec((None, PAGE_ROWS, D_MODEL),
                            lambda s, tbl, se, nu: (tbl[s * STEP_PAGES + k], 0, 0))

    wspec = lambda shape: pl.BlockSpec((None,) + shape, lambda s, tbl, se, nu: (se[s], 0, 0))
    grid_spec = pltpu.PrefetchScalarGridSpec(
        num_scalar_prefetch=3,
        grid=(n_steps,),
        in_specs=[page_spec(k) for k in range(STEP_PAGES)]
        + [wspec((D_MODEL, D_EXPERT)), wspec((D_MODEL, D_EXPERT)), wspec((D_EXPERT, D_MODEL))],
        out_specs=pl.BlockSpec((STEP_ROWS, D_MODEL), lambda s, tbl, se, nu: (s, 0)),
        scratch_shapes=[pltpu.VMEM((D_MODEL, D_EXPERT), BF16), pltpu.VMEM((D_MODEL, D_EXPERT), BF16),
                        pltpu.VMEM((D_EXPERT, D_MODEL), BF16)],
    )
    return pl.pallas_call(
        _experts_kernel,
        grid_spec=grid_spec,
        out_shape=jax.ShapeDtypeStruct((n_steps * STEP_ROWS, D_MODEL), BF16),
        compiler_params=_params(1),
        name="experts",
    )(tbl, step_e, n_used, *([xs_pages] * STEP_PAGES), w_gate, w_up, w_down)


def _combine_kernel(inv_ref, *refs):
    del inv_ref
    pages = refs[:TILE_PAGES]
    x1_ref, route_ref, nfin_ref, y_ref = refs[TILE_PAGES:]
    out_loc = jnp.concatenate([pg[...] for pg in pages], axis=0)
    tt = x1_ref.shape[0]
    route = route_ref[...]
    s_id = _iota((tt, TILE_SLOTS), 1).astype(F32)
    sel1 = jnp.where(s_id == route[:, 0:1], 1.0, 0.0).astype(BF16)
    sel2 = jnp.where(s_id == route[:, 1:2], 1.0, 0.0).astype(BF16)
    moe = (route[:, 2:3] * jnp.dot(sel1, out_loc, preferred_element_type=F32)
           + route[:, 3:4] * jnp.dot(sel2, out_loc, preferred_element_type=F32))
    y_ref[...] = _rms(x1_ref[...] + moe, nfin_ref[...])


def _combine(inv, out_pages, x1, route, norm_final, tt):
    rows = x1.shape[0]

    def page_spec(k):
        return pl.BlockSpec((None, PAGE_ROWS, D_MODEL), lambda i, inv: (inv[i * TILE_PAGES + k], 0, 0))

    grid_spec = pltpu.PrefetchScalarGridSpec(
        num_scalar_prefetch=1,
        grid=(rows // tt,),
        in_specs=[page_spec(k) for k in range(TILE_PAGES)]
        + [pl.BlockSpec((tt, D_MODEL), lambda i, inv: (i, 0)),
           pl.BlockSpec((tt, LANES), lambda i, inv: (i, 0)),
           pl.BlockSpec((1, D_MODEL), lambda i, inv: (0, 0))],
        out_specs=pl.BlockSpec((tt, D_MODEL), lambda i, inv: (i, 0)),
    )
    return pl.pallas_call(
        _combine_kernel,
        grid_spec=grid_spec,
        out_shape=jax.ShapeDtypeStruct((rows, D_MODEL), F32),
        compiler_params=_params(1),
        name="combine",
    )(inv, *([out_pages] * TILE_PAGES), x1, route, norm_final)


def _page_tables(cnt, n_steps):
    n_tiles = cnt.shape[0]
    pg = (cnt + (PAGE_ROWS - 1)) // PAGE_ROWS
    lend = jnp.cumsum(pg, axis=1)
    loff = lend - pg
    cum_t = jnp.cumsum(pg, axis=0)
    pref = cum_t - pg
    tot = cum_t[-1]
    totp = ((tot + (STEP_PAGES - 1)) // STEP_PAGES) * STEP_PAGES
    gend = jnp.cumsum(totp)
    gstart = gend - totp
    gp = jnp.arange(n_steps * STEP_PAGES, dtype=jnp.int32)
    e_gp = jnp.minimum(jnp.sum((gend[None, :] <= gp[:, None]).astype(jnp.int32), axis=1), N_EXPERTS - 1)
    r = gp - gstart[e_gp]
    valid = (r < tot[e_gp]) & (gp < gend[-1])
    cum_col = cum_t.T[e_gp]
    i_gp = jnp.minimum(jnp.sum((cum_col <= r[:, None]).astype(jnp.int32), axis=1), n_tiles - 1)
    src = i_gp * TILE_PAGES + loff[i_gp, e_gp] + (r - pref[i_gp, e_gp])
    tbl = jnp.where(valid, src, 0).astype(jnp.int32)
    step_e = e_gp[::STEP_PAGES].astype(jnp.int32)
    n_used = (gend[-1] // STEP_PAGES).astype(jnp.int32).reshape(1)
    lp = jnp.arange(TILE_PAGES, dtype=jnp.int32)
    e_l = jnp.minimum(jnp.sum((lend[:, None, :] <= lp[None, :, None]).astype(jnp.int32), axis=2),
                      N_EXPERTS - 1)
    valid_l = lp[None, :] < lend[:, -1:]
    take = lambda a: jnp.take_along_axis(a, e_l, axis=1)
    gpos = gstart[e_l] + take(pref) + (lp[None, :] - take(loff))
    zero_page = (n_steps - 1) * STEP_PAGES
    inv = jnp.where(valid_l, gpos, zero_page).astype(jnp.int32).reshape(-1)
    return tbl, step_e, n_used, inv


def kernel(x_prompt, x_sample, mem_prompt, cache_mem_k, cache_mem_v, state_gdn, state_conv,
           norm_mix, w_in, b_gate, w_s, b_s, norm_a_v, w_conv, a_log, dt_bias, norm_gdn_out,
           norm_mem, w_mem_kv, w_o, norm_ffn, w_router_group, b_router_group, w_router_expert,
           b_router_expert, w_exp_gate, w_exp_up, w_exp_down, norm_final):
    depth = norm_mix.shape[0]
    assert depth == 1
    bsz, seq, _ = x_prompt.shape
    n_s = x_sample.shape[0]
    assert x_sample.shape[1] == 1 and seq % MIX_TILE == 0 and n_s % GS_REQ == 0 and n_s <= MIX_TILE
    l = 0
    row = lambda v: v.reshape(1, -1)

    wi = w_in[l]
    o_z = 2 * D_MODEL + C_QKV
    o_beta = o_z + D_MODEL
    o_qm = o_beta + 2 * B_HEADS
    o_gate = o_qm + D_MODEL
    w_main = jnp.concatenate([wi[:, :o_beta], wi[:, o_qm:]], axis=1).astype(BF16)
    w_ba = jnp.pad(wi[:, o_beta:o_qm], ((0, 0), (0, LANES - 2 * B_HEADS))).astype(BF16)
    pad_heads = lambda v: jnp.pad(v.reshape(1, B_HEADS), ((0, 0), (B_HEADS, LANES - 2 * B_HEADS)))
    alog_row = pad_heads(a_log[l])
    dt_row = pad_heads(dt_bias[l])
    w_kv = w_mem_kv[l].astype(BF16)
    wo = w_o[l].astype(BF16)
    wr_t = jnp.pad(jnp.concatenate([w_router_expert[l], w_router_group[l]], axis=1).T,
                   ((0, LANES - N_EXPERTS - N_GROUPS), (0, 0)))
    br_col = jnp.pad(jnp.concatenate([b_router_expert[l], b_router_group[l]]).reshape(-1, 1),
                     ((0, LANES - N_EXPERTS - N_GROUPS), (0, LANES - 1)))
    bs_t = jnp.pad(b_s[l].T, ((0, 0), (0, LANES - A_GROUPS)))
    ws0_row = jnp.repeat(w_s[l][:, 0, 0], A_CHUNK).reshape(1, D_MODEL)
    bs0_row = jnp.repeat(b_s[l][:, 0], A_CHUNK).reshape(1, D_MODEL)
    head_of = jnp.arange(D_MODEL, dtype=jnp.int32) // M_HEAD_DIM
    e_mat = (head_of[:, None] == jnp.arange(LANES, dtype=jnp.int32)[None, :]).astype(BF16)
    et_mat = e_mat.T

    xp = x_prompt.reshape(bsz * seq, D_MODEL)
    xs_ = x_sample.reshape(n_s, D_MODEL)
    n_tiles_p = (bsz * seq) // MIX_TILE
    n_tiles = n_tiles_p + 1

    mk, mv = _memkv(mem_prompt.reshape(bsz * N_MEM, D_MODEL), row(norm_mem[l]), w_kv)
    p_p, ba_p = _proj(xp, row(norm_mix[l]), w_main, w_ba, row(norm_a_v[l]), row(b_gate[l]),
                      tm=min(1024, bsz * seq))
    ob_p, s_p = _gdn(p_p, ba_p, w_conv[l], alog_row, dt_row, row(norm_gdn_out[l]), bsz, seq)
    x1_p, xs_all, route_p, cnt_p = _mix(xp, p_p, ob_p, mk, mv, w_s[l], bs_t, wo, row(norm_ffn[l]),
                                        wr_t, br_col, seq, n_tiles)
    p_s, ba_s = _proj(xs_, row(norm_mix[l]), w_main, w_ba, row(norm_a_v[l]), row(b_gate[l]), tm=n_s)
    ob_s, s_s, c_s = _gdn_step(p_s, ba_s, state_conv[l], state_gdn[l], w_conv[l], alog_row, dt_row,
                               row(norm_gdn_out[l]))
    oc_s = _xattn1(p_s, cache_mem_k[l].reshape(n_s, N_MEM, D_MODEL),
                   cache_mem_v[l].reshape(n_s, N_MEM, D_MODEL), e_mat, et_mat)
    x1_s, xs_all, route_s, cnt_s = _mix1(xs_, p_s, ob_s, oc_s, ws0_row, bs0_row, wo, row(norm_ffn[l]),
                                         wr_t, br_col, xs_all, n_tiles_p)
    cnt = jnp.concatenate([cnt_p[:, 0, :N_EXPERTS], cnt_s[:, 0, :N_EXPERTS]], axis=0).astype(jnp.int32)
    max_pages = n_tiles_p * TILE_PAGES + (2 * n_s) // PAGE_ROWS + N_EXPERTS
    n_steps = (max_pages + N_EXPERTS * (STEP_PAGES - 1)) // STEP_PAGES + 1
    tbl, step_e, n_used, inv = _page_tables(cnt, n_steps)
    ne = N_GROUPS * EXP_PER_GROUP
    out_sorted = _experts(tbl, step_e, n_used, xs_all.reshape(n_tiles * TILE_PAGES, PAGE_ROWS, D_MODEL),
                          w_exp_gate[l].reshape(ne, D_MODEL, D_EXPERT),
                          w_exp_up[l].reshape(ne, D_MODEL, D_EXPERT),
                          w_exp_down[l].reshape(ne, D_EXPERT, D_MODEL), n_steps)
    out_pages = out_sorted.reshape(n_steps * STEP_PAGES, PAGE_ROWS, D_MODEL)
    y_p = _combine(inv[:n_tiles_p * TILE_PAGES], out_pages, x1_p, route_p, row(norm_final), MIX_TILE)
    y_s = _combine(inv[n_tiles_p * TILE_PAGES:], out_pages, x1_s, route_s, row(norm_final), n_s)

    qkv_p = p_p[:, 2 * D_MODEL:2 * D_MODEL + C_QKV].reshape(bsz, seq, C_QKV)
    return (y_p.reshape(bsz, seq, D_MODEL),
            y_s.reshape(n_s, 1, D_MODEL),
            mk.reshape(1, bsz, N_MEM, M_HEADS, M_HEAD_DIM),
            mv.reshape(1, bsz, N_MEM, M_HEADS, M_HEAD_DIM),
            s_p[None],
            qkv_p[:, seq - (B_CONV - 1):, :][None],
            s_s[None],
            c_s[None],
            p_s[:, D_MODEL:2 * D_MODEL].reshape(1, n_s, 1, D_MODEL))
```

```python
import functools
import math

import jax
import jax.numpy as jnp
from jax import lax
from jax.experimental import pallas as pl
from jax.experimental.pallas import tpu as pltpu

F32 = jnp.float32
BF16 = jnp.bfloat16

D_MODEL = 1024
A_GROUPS = 8
A_CHUNK = 128
B_HEADS = 8
B_DK = 128
C_QKV = 3 * D_MODEL
B_CONV = 4
N_MEM = 256
M_HEADS = 4
M_HEAD_DIM = 256
N_GROUPS = 4
EXP_PER_GROUP = 8
N_EXPERTS = 32
D_EXPERT = 512
EPS = 1e-6

LANES = 128
GDN_CHUNK = 128
MIX_TILE = 256
PAGE_ROWS = 16
TILE_PAGES = 64
TILE_SLOTS = PAGE_ROWS * TILE_PAGES
STEP_PAGES = 16
STEP_ROWS = PAGE_ROWS * STEP_PAGES
PROJ_COLS = 10 * D_MODEL
VMEM_LIMIT = 56 * 1024 * 1024


def _params(n_grid):
    return pltpu.CompilerParams(dimension_semantics=("arbitrary",) * n_grid,
                                vmem_limit_bytes=VMEM_LIMIT)


def _rms(x, g):
    return x * lax.rsqrt(jnp.mean(x * x, axis=-1, keepdims=True) + EPS) * g


def _dot(a, b):
    return jnp.dot(a.astype(BF16), b.astype(BF16), preferred_element_type=F32)


def _dot_nt(a, b):
    return lax.dot_general(a.astype(BF16), b.astype(BF16), (((1,), (1,)), ((), ())),
                           preferred_element_type=F32)


def _dot_tn(a, b):
    return lax.dot_general(a.astype(BF16), b.astype(BF16), (((0,), (0,)), ((), ())),
                           preferred_element_type=F32)


def _split3(x):
    hi = x.astype(BF16)
    r = x - hi.astype(F32)
    mid = r.astype(BF16)
    lo = (r - mid.astype(F32)).astype(BF16)
    return hi, mid, lo


def _iota(shape, axis):
    return lax.broadcasted_iota(jnp.int32, shape, axis)


def _memkv_kernel(mem_ref, g_ref, w_ref, k_ref, v_ref):
    h = _rms(mem_ref[...], g_ref[...]).astype(BF16)
    kv = jnp.dot(h, w_ref[...], preferred_element_type=F32)
    k_ref[...] = kv[:, :D_MODEL]
    v_ref[...] = kv[:, D_MODEL:]


def _memkv(mem2d, norm_mem, w_kv_bf16):
    rows = mem2d.shape[0]
    nb = rows // N_MEM
    return pl.pallas_call(
        _memkv_kernel,
        grid=(nb,),
        in_specs=[pl.BlockSpec((N_MEM, D_MODEL), lambda i: (i, 0)),
                  pl.BlockSpec((1, D_MODEL), lambda i: (0, 0)),
                  pl.BlockSpec((D_MODEL, 2 * D_MODEL), lambda i: (0, 0))],
        out_specs=[pl.BlockSpec((N_MEM, D_MODEL), lambda i: (i, 0)),
                   pl.BlockSpec((N_MEM, D_MODEL), lambda i: (i, 0))],
        out_shape=[jax.ShapeDtypeStruct((rows, D_MODEL), F32)] * 2,
        compiler_params=_params(1),
        name="memkv",
    )(mem2d, norm_mem, w_kv_bf16)


def _proj_kernel(x_ref, nm_ref, w_ref, wba_ref, nav_ref, bg_ref, p_ref, ba_ref, hn_ref):
    j = pl.program_id(1)

    @pl.when(j == 0)
    def _():
        hn = _rms(x_ref[...], nm_ref[...]).astype(BF16)
        hn_ref[...] = hn
        ba_ref[...] = jnp.dot(hn, wba_ref[...], preferred_element_type=F32)

    acc = jnp.dot(hn_ref[...], w_ref[...], preferred_element_type=F32)

    @pl.when(j == 0)
    def _():
        p_ref[...] = jax.nn.gelu(acc)

    @pl.when(j == 1)
    def _():
        p_ref[...] = _rms(jax.nn.gelu(acc), nav_ref[...])

    @pl.when((j >= 2) & (j <= 6))
    def _():
        p_ref[...] = acc

    @pl.when(j >= 7)
    def _():
        p_ref[...] = jax.nn.sigmoid(acc + bg_ref[...])


def _proj(x2d, norm_mix, w_bf16, wba_bf16, norm_a_v, b_gate, tm):
    rows = x2d.shape[0]
    ncol = PROJ_COLS // D_MODEL
    return pl.pallas_call(
        _proj_kernel,
        grid=(rows // tm, ncol),
        in_specs=[pl.BlockSpec((tm, D_MODEL), lambda i, j: (i, 0)),
                  pl.BlockSpec((1, D_MODEL), lambda i, j: (0, 0)),
                  pl.BlockSpec((D_MODEL, D_MODEL), lambda i, j: (0, j)),
                  pl.BlockSpec((D_MODEL, LANES), lambda i, j: (0, 0)),
                  pl.BlockSpec((1, D_MODEL), lambda i, j: (0, 0)),
                  pl.BlockSpec((1, D_MODEL), lambda i, j: (0, jnp.maximum(j - 7, 0)))],
        out_specs=[pl.BlockSpec((tm, D_MODEL), lambda i, j: (i, j)),
                   pl.BlockSpec((tm, LANES), lambda i, j: (i, 0))],
        out_shape=[jax.ShapeDtypeStruct((rows, PROJ_COLS), F32),
                   jax.ShapeDtypeStruct((rows, LANES), F32)],
        scratch_shapes=[pltpu.VMEM((tm, D_MODEL), BF16)],
        compiler_params=_params(2),
        name="proj",
    )(x2d, norm_mix, w_bf16, wba_bf16, norm_a_v, b_gate)


def _gate_terms(ba, alog_row, dt_row):
    beta = jax.nn.sigmoid(ba)
    g = -jnp.exp(alog_row) * jax.nn.softplus(ba + dt_row)
    return beta, g


def _gdn_kernel(q_ref, k_ref, v_ref, z_ref, ba_ref, wc_ref, alog_ref, dt_ref, ng_ref,
                ob_ref, s_out_ref, s_ref, ext_ref):
    c = pl.program_id(1)
    n_c = pl.num_programs(1)
    C = GDN_CHUNK

    @pl.when(c == 0)
    def _():
        s_ref[...] = jnp.zeros_like(s_ref)
        ext_ref[0:8, :] = jnp.zeros((8, C_QKV), F32)

    ext_ref[8:8 + C, 0:D_MODEL] = q_ref[...]
    ext_ref[8:8 + C, D_MODEL:2 * D_MODEL] = k_ref[...]
    ext_ref[8:8 + C, 2 * D_MODEL:3 * D_MODEL] = v_ref[...]

    beta_all, g_all = _gate_terms(ba_ref[...], alog_ref[...], dt_ref[...])
    row = _iota((C, C), 0)
    col = _iota((C, C), 1)
    incl = row >= col
    strict = row > col
    eye = (row == col).astype(F32)
    same_blk = [(row // s) == (col // s) for s in (8, 16, 32, 64, 128)]
    tri = incl.astype(BF16)
    g_hi, g_mid, g_lo = _split3(g_all)
    gcum = (jnp.dot(tri, g_hi, preferred_element_type=F32)
            + jnp.dot(tri, g_mid, preferred_element_type=F32)
            + jnp.dot(tri, g_lo, preferred_element_type=F32))
    gcum_t = gcum.T

    def conv_silu(lo):
        acc = wc_ref[3:4, lo:lo + B_DK] * ext_ref[8:8 + C, lo:lo + B_DK]
        for j in range(1, B_CONV):
            acc = acc + wc_ref[3 - j:4 - j, lo:lo + B_DK] * ext_ref[8 - j:8 - j + C, lo:lo + B_DK]
        return jax.nn.silu(acc)

    def l2n(t):
        return t * lax.rsqrt(jnp.sum(t * t, axis=-1, keepdims=True) + EPS)

    heads = range(B_HEADS)
    q = [l2n(conv_silu(h * B_DK)) * (B_DK ** -0.5) for h in heads]
    k = [l2n(conv_silu(D_MODEL + h * B_DK)) for h in heads]
    v = [conv_silu(2 * D_MODEL + h * B_DK) for h in heads]
    bcol = [beta_all[:, h:h + 1] for h in heads]
    gcol = [gcum[:, 8 + h:9 + h] for h in heads]
    glast = [gcum[C - 1:C, 8 + h:9 + h] for h in heads]
    eg = [jnp.exp(gcol[h]) for h in heads]
    dec = [jnp.where(incl, jnp.exp(jnp.where(incl, gcol[h] - gcum_t[8 + h:9 + h, :], 0.0)), 0.0)
           for h in heads]
    kb = [k[h] * bcol[h] for h in heads]
    a1 = [_dot_nt(jnp.concatenate([kb[h], q[h]], axis=0), k[h]) for h in heads]
    m = [jnp.where(strict, a1[h][:C] * dec[h], 0.0) for h in heads]
    attn = [a1[h][C:] * dec[h] for h in heads]
    m0 = [jnp.where(same_blk[0], m[h], 0.0) for h in heads]
    x = [eye - m0[h] for h in heads]
    pw = [_dot(m0[h], m0[h]) for h in heads]
    x = [x[h] + _dot(x[h], pw[h]) for h in heads]
    pw = [_dot(pw[h], pw[h]) for h in heads]
    x = [x[h] + _dot(x[h], pw[h]) for h in heads]
    for lvl in range(1, len(same_blk)):
        in_lvl = same_blk[lvl] & jnp.logical_not(same_blk[lvl - 1])
        xb = [x[h].astype(BF16) for h in heads]
        t = [_dot(xb[h], jnp.where(in_lvl, m[h], 0.0)) for h in heads]
        x = [x[h] - _dot(t[h], xb[h]) for h in heads]
    rhs = [jnp.concatenate([v[h] * bcol[h], kb[h] * eg[h]], axis=1) for h in heads]
    sol = [_dot(x[h], rhs[h]) for h in heads]
    s_old = [s_ref[h] for h in heads]
    b1 = [_dot(jnp.concatenate([sol[h][:, B_DK:], q[h] * eg[h]], axis=0), s_old[h]) for h in heads]
    u = [sol[h][:, :B_DK] - b1[h][:C] for h in heads]
    o = [b1[h][C:] + _dot(attn[h], u[h]) for h in heads]
    kd = [k[h] * jnp.exp(glast[h] - gcol[h]) for h in heads]
    s_new = [s_old[h] * jnp.exp(glast[h]) + _dot_tn(kd[h], u[h]) for h in heads]
    for h in heads:
        s_ref[h] = s_new[h]
        zh = z_ref[:, h * B_DK:(h + 1) * B_DK]
        ob_ref[:, h * B_DK:(h + 1) * B_DK] = _rms(o[h], ng_ref[...]) * jax.nn.silu(zh)

    ext_ref[0:8, :] = ext_ref[C:C + 8, :]

    @pl.when(c == n_c - 1)
    def _():
        s_out_ref[...] = s_ref[...]


def _gdn(p, ba, w_conv, alog_row, dt_row, norm_gdn, bsz, seq):
    nc = seq // GDN_CHUNK
    blk = lambda colblk: pl.BlockSpec((GDN_CHUNK, D_MODEL), lambda b, c: (b * nc + c, colblk))
    small = lambda shape: pl.BlockSpec(shape, lambda b, c: (0, 0))
    return pl.pallas_call(
        _gdn_kernel,
        grid=(bsz, nc),
        in_specs=[blk(2), blk(3), blk(4), blk(5),
                  pl.BlockSpec((GDN_CHUNK, LANES), lambda b, c: (b * nc + c, 0)),
                  small((B_CONV, C_QKV)), small((1, LANES)), small((1, LANES)), small((1, B_DK))],
        out_specs=[pl.BlockSpec((GDN_CHUNK, D_MODEL), lambda b, c: (b * nc + c, 0)),
                   pl.BlockSpec((None, B_HEADS, B_DK, B_DK), lambda b, c: (b, 0, 0, 0))],
        out_shape=[jax.ShapeDtypeStruct((bsz * seq, D_MODEL), F32),
                   jax.ShapeDtypeStruct((bsz, B_HEADS, B_DK, B_DK), F32)],
        scratch_shapes=[pltpu.VMEM((B_HEADS, B_DK, B_DK), F32),
                        pltpu.VMEM((8 + GDN_CHUNK, C_QKV), F32)],
        compiler_params=_params(2),
        name="gdn",
    )(p, p, p, p, ba, w_conv, alog_row, dt_row, norm_gdn)


GS_REQ = 8


def _gdn_step_kernel(q_ref, k_ref, v_ref, z_ref, ba_ref, sc_ref, sg_ref, wc_ref, alog_ref, dt_ref,
                     ng_ref, ob_ref, snew_ref, cnew_ref):
    beta_all, g_all = _gate_terms(ba_ref[...], alog_ref[...], dt_ref[...])
    eg_all = jnp.exp(g_all)
    pre = jnp.concatenate([q_ref[...], k_ref[...], v_ref[...]], axis=1)
    conv = wc_ref[3:4, :] * pre
    for j in range(B_CONV - 1):
        conv = conv + wc_ref[j:j + 1, :] * sc_ref[:, j, :]
    cnew_ref[:, 0, :] = sc_ref[:, 1, :]
    cnew_ref[:, 1, :] = sc_ref[:, 2, :]
    cnew_ref[:, 2, :] = pre
    qkv = jax.nn.silu(conv)
    zero_rows = jnp.zeros((LANES - GS_REQ, B_DK), F32)

    def l2n(t):
        return t * lax.rsqrt(jnp.sum(t * t, axis=-1, keepdims=True) + EPS)

    for h in range(B_HEADS):
        qh = l2n(qkv[:, h * B_DK:(h + 1) * B_DK]) * (B_DK ** -0.5)
        kh = l2n(qkv[:, D_MODEL + h * B_DK:D_MODEL + (h + 1) * B_DK])
        vh = qkv[:, 2 * D_MODEL + h * B_DK:2 * D_MODEL + (h + 1) * B_DK]
        q_t = jnp.concatenate([qh, zero_rows], axis=0).T
        k_t = jnp.concatenate([kh, zero_rows], axis=0).T
        o_rows = []
        for r in range(GS_REQ):
            kcol = k_t[:, r:r + 1]
            qcol = q_t[:, r:r + 1]
            s_old = sg_ref[r, h]
            beta = beta_all[r:r + 1, h:h + 1]
            eg = eg_all[r:r + 1, 8 + h:9 + h]
            ks = jnp.sum(s_old * kcol, axis=0, keepdims=True)
            u = beta * (vh[r:r + 1, :] - eg * ks)
            s_new = s_old * eg + kcol * u
            snew_ref[r, h] = s_new
            o_rows.append(jnp.sum(s_new * qcol, axis=0, keepdims=True))
        o = jnp.concatenate(o_rows, axis=0)
        zh = z_ref[:, h * B_DK:(h + 1) * B_DK]
        ob_ref[:, h * B_DK:(h + 1) * B_DK] = _rms(o, ng_ref[...]) * jax.nn.silu(zh)


def _gdn_step(p, ba, state_conv, state_gdn, w_conv, alog_row, dt_row, norm_gdn):
    n = p.shape[0]
    blk = lambda colblk: pl.BlockSpec((GS_REQ, D_MODEL), lambda i: (i, colblk))
    small = lambda shape: pl.BlockSpec(shape, lambda i: (0, 0))
    return pl.pallas_call(
        _gdn_step_kernel,
        grid=(n // GS_REQ,),
        in_specs=[blk(2), blk(3), blk(4), blk(5),
                  pl.BlockSpec((GS_REQ, LANES), lambda i: (i, 0)),
                  pl.BlockSpec((GS_REQ, B_CONV - 1, C_QKV), lambda i: (i, 0, 0)),
                  pl.BlockSpec((GS_REQ, B_HEADS, B_DK, B_DK), lambda i: (i, 0, 0, 0)),
                  small((B_CONV, C_QKV)), small((1, LANES)), small((1, LANES)), small((1, B_DK))],
        out_specs=[pl.BlockSpec((GS_REQ, D_MODEL), lambda i: (i, 0)),
                   pl.BlockSpec((GS_REQ, B_HEADS, B_DK, B_DK), lambda i: (i, 0, 0, 0)),
                   pl.BlockSpec((GS_REQ, B_CONV - 1, C_QKV), lambda i: (i, 0, 0))],
        out_shape=[jax.ShapeDtypeStruct((n, D_MODEL), F32),
                   jax.ShapeDtypeStruct((n, B_HEADS, B_DK, B_DK), F32),
                   jax.ShapeDtypeStruct((n, B_CONV - 1, C_QKV), F32)],
        compiler_params=_params(1),
        name="gdn_step",
    )(p, p, p, p, ba, state_conv, state_gdn, w_conv, alog_row, dt_row, norm_gdn)


XA_REQ = 4


def _xattn1_kernel(q_ref, ck_ref, cv_ref, oc_ref):
    scale = M_HEAD_DIM ** -0.5
    base = pl.program_id(0) * XA_REQ

    def body(r, carry):
        for h in range(M_HEADS):
            lo = h * M_HEAD_DIM
            q = q_ref[pl.ds(base + r, 1), lo:lo + M_HEAD_DIM]
            s = jnp.sum(ck_ref[r, :, h, :] * q, axis=-1, keepdims=True) * scale
            p = jnp.exp(s - jnp.max(s, axis=0, keepdims=True))
            pr = p / jnp.sum(p, axis=0, keepdims=True)
            oc_ref[pl.ds(base + r, 1), lo:lo + M_HEAD_DIM] = jnp.sum(pr * cv_ref[r, :, h, :], axis=0,
                                                                       keepdims=True)
        return carry

    lax.fori_loop(0, XA_REQ, body, 0)


def _xattn1(p, cache_k, cache_v):
    n = p.shape[0]
    cache_spec = pl.BlockSpec((XA_REQ, N_MEM, M_HEADS, M_HEAD_DIM), lambda i: (i, 0, 0, 0))
    return pl.pallas_call(
        _xattn1_kernel,
        grid=(n // XA_REQ,),
        in_specs=[pl.BlockSpec((n, D_MODEL), lambda i: (0, 6)), cache_spec, cache_spec],
        out_specs=pl.BlockSpec((n, D_MODEL), lambda i: (0, 0)),
        out_shape=jax.ShapeDtypeStruct((n, D_MODEL), F32),
        compiler_params=_params(1),
        name="xattn1",
    )(p, cache_k, cache_v)


def _route_and_dispatch(x1, nf_ref, wrt_ref, br_ref, xs_ref, route_ref, cnt_ref):
    tt = x1.shape[0]
    h2 = _rms(x1, nf_ref[...])
    h_hi, h_mid, h_lo = _split3(h2)
    w_hi, w_mid, w_lo = _split3(wrt_ref[...])

    def nt(a, b):
        return lax.dot_general(a, b, (((1,), (1,)), ((), ())), preferred_element_type=F32)

    logits = (nt(w_hi, h_hi) + (nt(w_hi, h_mid) + nt(w_mid, h_hi))
              + (nt(w_hi, h_lo) + nt(w_lo, h_hi) + nt(w_mid, h_mid))) + br_ref[:, 0:1]
    e_id = _iota((LANES, tt), 0).astype(F32)
    neg = jnp.float32(-jnp.inf)
    big = jnp.float32(1 << 20)
    is_grp = (e_id >= N_EXPERTS) & (e_id < N_EXPERTS + N_GROUPS)
    gl = jnp.where(is_grp, logits, neg)
    gmax = jnp.max(gl, axis=0, keepdims=True)
    g_lo = (jnp.min(jnp.where(gl == gmax, e_id, big), axis=0, keepdims=True) - N_EXPERTS) * EXP_PER_GROUP
    p_g = 1.0 / jnp.sum(jnp.exp(gl - gmax), axis=0, keepdims=True)
    in_grp = (e_id >= g_lo) & (e_id < g_lo + EXP_PER_GROUP)
    el = jnp.where(in_grp, logits, neg)
    v1 = jnp.max(el, axis=0, keepdims=True)
    i1 = jnp.min(jnp.where(el == v1, e_id, big), axis=0, keepdims=True)
    el2 = jnp.where(e_id == i1, neg, el)
    v2 = jnp.max(el2, axis=0, keepdims=True)
    i2 = jnp.min(jnp.where(el2 == v2, e_id, big), axis=0, keepdims=True)
    e2 = jnp.exp(v2 - v1)
    w1 = p_g / (1.0 + e2)
    w2 = p_g * e2 / (1.0 + e2)
    oh1 = (e_id == i1).astype(F32)
    oh2 = (e_id == i2).astype(F32)
    a_t = oh1 + oh2
    cnt = jnp.sum(a_t, axis=1, keepdims=True)
    pages = jnp.floor((cnt + (PAGE_ROWS - 1)) * (1.0 / PAGE_ROWS))
    lower = (_iota((LANES, LANES), 0) > _iota((LANES, LANES), 1)).astype(BF16)
    offp = jnp.dot(lower, jnp.broadcast_to(pages, (LANES, LANES)).astype(BF16),
                   preferred_element_type=F32)[:, 0:1]
    upper = (_iota((tt, tt), 0) < _iota((tt, tt), 1)).astype(BF16)
    rank = jnp.dot(a_t.astype(BF16), upper, preferred_element_type=F32)
    pos = offp * PAGE_ROWS + rank
    slot1 = jnp.sum(oh1 * pos, axis=0, keepdims=True)
    slot2 = jnp.sum(oh2 * pos, axis=0, keepdims=True)
    s_id = _iota((TILE_SLOTS, tt), 0).astype(F32)
    sel = jnp.where((s_id == slot1) | (s_id == slot2), 1.0, 0.0).astype(BF16)
    xs_ref[...] = jnp.dot(sel, h_hi, preferred_element_type=F32).astype(BF16)
    r_id = _iota((LANES, tt), 0)
    rows = (jnp.where(r_id == 0, slot1, 0.0) + jnp.where(r_id == 1, slot2, 0.0)
            + jnp.where(r_id == 2, w1, 0.0) + jnp.where(r_id == 3, w2, 0.0))
    route_ref[...] = rows.T
    cnt_ref[...] = jnp.broadcast_to(cnt, (LANES, LANES)).T[0:8, :]


def _merge(x, oa, ob, oc, ga, gb, gc, wo_ref):
    s = ga * oa + gb * ob + gc * oc
    return x + jnp.dot(s.astype(BF16), wo_ref[...], preferred_element_type=F32)


def _mix_kernel(x_ref, au_ref, av_ref, qm_ref, ga_ref, gb_ref, gc_ref, ob_ref, mk_ref, mv_ref,
                ws_ref, bst_ref, wo_ref, nf_ref, wrt_ref, br_ref,
                x1_ref, xs_ref, route_ref, cnt_ref, oa_ref, oc_ref):
    tt = x_ref.shape[0]
    causal = _iota((A_CHUNK, A_CHUNK), 0) >= _iota((A_CHUNK, A_CHUNK), 1)
    for g in range(A_GROUPS):
        wsg = jnp.where(causal, ws_ref[g], 0.0).astype(BF16)
        bcol = bst_ref[:, g:g + 1]
        lo = g * A_CHUNK
        for cc in range(tt // A_CHUNK):
            r0 = cc * A_CHUNK
            sv = jnp.dot(wsg, av_ref[r0:r0 + A_CHUNK, lo:lo + A_CHUNK].astype(BF16),
                         preferred_element_type=F32) + bcol
            oa_ref[r0:r0 + A_CHUNK, lo:lo + A_CHUNK] = au_ref[r0:r0 + A_CHUNK, lo:lo + A_CHUNK] * sv
    scale = M_HEAD_DIM ** -0.5
    for h in range(M_HEADS):
        lo = h * M_HEAD_DIM
        s = _dot_nt(qm_ref[:, lo:lo + M_HEAD_DIM], mk_ref[:, lo:lo + M_HEAD_DIM]) * scale
        p = jnp.exp(s - jnp.max(s, axis=-1, keepdims=True))
        pr = p / jnp.sum(p, axis=-1, keepdims=True)
        oc_ref[:, lo:lo + M_HEAD_DIM] = _dot(pr, mv_ref[:, lo:lo + M_HEAD_DIM])
    x1 = _merge(x_ref[...], oa_ref[...], ob_ref[...], oc_ref[...],
                ga_ref[...], gb_ref[...], gc_ref[...], wo_ref)
    x1_ref[...] = x1
    _route_and_dispatch(x1, nf_ref, wrt_ref, br_ref, xs_ref, route_ref, cnt_ref)


def _mix1_kernel(x_ref, au_ref, av_ref, ga_ref, gb_ref, gc_ref, ob_ref, oc_ref, ws0_ref, bs0_ref,
                 wo_ref, nf_ref, wrt_ref, br_ref, xs_in_ref,
                 x1_ref, xs_ref, route_ref, cnt_ref):
    del xs_in_ref
    oa = au_ref[...] * (ws0_ref[...] * av_ref[...] + bs0_ref[...])
    x1 = _merge(x_ref[...], oa, ob_ref[...], oc_ref[...], ga_ref[...], gb_ref[...], gc_ref[...], wo_ref)
    x1_ref[...] = x1
    _route_and_dispatch(x1, nf_ref, wrt_ref, br_ref, xs_ref, route_ref, cnt_ref)


def _route_out_specs(tt, tile_of):
    return [pl.BlockSpec((tt, D_MODEL), lambda i: (i, 0)),
            pl.BlockSpec((None, TILE_SLOTS, D_MODEL), lambda i: (tile_of(i), 0, 0)),
            pl.BlockSpec((tt, LANES), lambda i: (i, 0)),
            pl.BlockSpec((None, 8, LANES), lambda i: (i, 0, 0))]


def _mix(x2d, p, ob, mk, mv, w_s, bs_t, wo_bf16, norm_ffn, wr_t, br_col, seq, n_tiles_total):
    rows = x2d.shape[0]
    tt = MIX_TILE
    nt = rows // tt
    per_b = seq // tt
    pblk = lambda colblk: pl.BlockSpec((tt, D_MODEL), lambda i: (i, colblk))
    const = lambda shape: pl.BlockSpec(shape, lambda i: (0,) * len(shape))
    return pl.pallas_call(
        _mix_kernel,
        grid=(nt,),
        in_specs=[pblk(0), pblk(0), pblk(1), pblk(6), pblk(7), pblk(8), pblk(9), pblk(0),
                  pl.BlockSpec((N_MEM, D_MODEL), lambda i: (i // per_b, 0)),
                  pl.BlockSpec((N_MEM, D_MODEL), lambda i: (i // per_b, 0)),
                  const((A_GROUPS, A_CHUNK, A_CHUNK)), const((A_CHUNK, LANES)),
                  const((D_MODEL, D_MODEL)), const((1, D_MODEL)),
                  const((LANES, D_MODEL)), const((LANES, LANES))],
        out_specs=_route_out_specs(tt, lambda i: i),
        out_shape=[jax.ShapeDtypeStruct((rows, D_MODEL), F32),
                   jax.ShapeDtypeStruct((n_tiles_total, TILE_SLOTS, D_MODEL), BF16),
                   jax.ShapeDtypeStruct((rows, LANES), F32),
                   jax.ShapeDtypeStruct((nt, 8, LANES), F32)],
        scratch_shapes=[pltpu.VMEM((tt, D_MODEL), F32), pltpu.VMEM((tt, D_MODEL), F32)],
        compiler_params=_params(1),
        name="mix",
    )(x2d, p, p, p, p, p, p, ob, mk, mv, w_s, bs_t, wo_bf16, norm_ffn, wr_t, br_col)


def _mix1(x2d, p, ob, oc, ws0_row, bs0_row, wo_bf16, norm_ffn, wr_t, br_col, xs_all, tile_idx):
    rows = x2d.shape[0]
    pblk = lambda colblk: pl.BlockSpec((rows, D_MODEL), lambda i: (0, colblk))
    const = lambda shape: pl.BlockSpec(shape, lambda i: (0,) * len(shape))
    return pl.pallas_call(
        _mix1_kernel,
        grid=(1,),
        in_specs=[pblk(0), pblk(0), pblk(1), pblk(7), pblk(8), pblk(9), pblk(0), pblk(0),
                  const((1, D_MODEL)), const((1, D_MODEL)),
                  const((D_MODEL, D_MODEL)), const((1, D_MODEL)),
                  const((LANES, D_MODEL)), const((LANES, LANES)),
                  pl.BlockSpec(memory_space=pl.ANY)],
        out_specs=_route_out_specs(rows, lambda i: tile_idx),
        out_shape=[jax.ShapeDtypeStruct((rows, D_MODEL), F32),
                   jax.ShapeDtypeStruct(xs_all.shape, BF16),
                   jax.ShapeDtypeStruct((rows, LANES), F32),
                   jax.ShapeDtypeStruct((1, 8, LANES), F32)],
        input_output_aliases={14: 1},
        compiler_params=_params(1),
        name="mix1",
    )(x2d, p, p, p, p, p, ob, oc, ws0_row, bs0_row, wo_bf16, norm_ffn, wr_t, br_col, xs_all)


def _experts_kernel(tbl_ref, se_ref, nu_ref, *refs):
    del tbl_ref
    pages = refs[:STEP_PAGES]
    wg_ref, wu_ref, wd_ref, o_ref, wgb_ref, wub_ref, wdb_ref = refs[STEP_PAGES:]
    s = pl.program_id(0)
    prev = se_ref[jnp.maximum(s - 1, 0)]

    @pl.when((s == 0) | (se_ref[s] != prev))
    def _():
        wgb_ref[...] = wg_ref[...].astype(BF16)
        wub_ref[...] = wu_ref[...].astype(BF16)
        wdb_ref[...] = wd_ref[...].astype(BF16)

    @pl.when(s < nu_ref[0])
    def _():
        x = jnp.concatenate([pg[...] for pg in pages], axis=0)
        gate = jnp.dot(x, wgb_ref[...], preferred_element_type=F32)
        up = jnp.dot(x, wub_ref[...], preferred_element_type=F32)
        act = (jax.nn.silu(gate) * up).astype(BF16)
        o_ref[...] = jnp.dot(act, wdb_ref[...], preferred_element_type=F32).astype(BF16)

    @pl.when(s >= nu_ref[0])
    def _():
        o_ref[...] = jnp.zeros_like(o_ref)


def _experts(tbl, step_e, n_used, xs_pages, w_gate, w_up, w_down, n_steps):
    def page_spec(k):
        return pl.BlockSpec((None, PAGE_ROWS, D_MODEL),
                            lambda s, tbl, se, nu: (tbl[s * STEP_PAGES + k], 0, 0))

    wspec = lambda shape: pl.BlockSpec((None,) + shape, lambda s, tbl, se, nu: (se[s], 0, 0))
    grid_spec = pltpu.PrefetchScalarGridSpec(
        num_scalar_prefetch=3,
        grid=(n_steps,),
        in_specs=[page_spec(k) for k in range(STEP_PAGES)]
        + [wspec((D_MODEL, D_EXPERT)), wspec((D_MODEL, D_EXPERT)), wspec((D_EXPERT, D_MODEL))],
        out_specs=pl.BlockSpec((STEP_ROWS, D_MODEL), lambda s, tbl, se, nu: (s, 0)),
        scratch_shapes=[pltpu.VMEM((D_MODEL, D_EXPERT), BF16), pltpu.VMEM((D_MODEL, D_EXPERT), BF16),
                        pltpu.VMEM((D_EXPERT, D_MODEL), BF16)],
    )
    return pl.pallas_call(
        _experts_kernel,
        grid_spec=grid_spec,
        out_shape=jax.ShapeDtypeStruct((n_steps * STEP_ROWS, D_MODEL), BF16),
        compiler_params=_params(1),
        name="experts",
    )(tbl, step_e, n_used, *([xs_pages] * STEP_PAGES), w_gate, w_up, w_down)


def _combine_kernel(inv_ref, *refs):
    del inv_ref
    pages = refs[:TILE_PAGES]
    x1_ref, route_ref, nfin_ref, y_ref = refs[TILE_PAGES:]
    out_loc = jnp.concatenate([pg[...] for pg in pages], axis=0)
    tt = x1_ref.shape[0]
    route = route_ref[...]
    s_id = _iota((tt, TILE_SLOTS), 1).astype(F32)
    sel1 = jnp.where(s_id == route[:, 0:1], 1.0, 0.0).astype(BF16)
    sel2 = jnp.where(s_id == route[:, 1:2], 1.0, 0.0).astype(BF16)
    moe = (route[:, 2:3] * jnp.dot(sel1, out_loc, preferred_element_type=F32)
           + route[:, 3:4] * jnp.dot(sel2, out_loc, preferred_element_type=F32))
    y_ref[...] = _rms(x1_ref[...] + moe, nfin_ref[...])


def _combine(inv, out_pages, x1, route, norm_final, tt):
    rows = x1.shape[0]

    def page_spec(k):
        return pl.BlockSpec((None, PAGE_ROWS, D_MODEL), lambda i, inv: (inv[i * TILE_PAGES + k], 0, 0))

    grid_spec = pltpu.PrefetchScalarGridSpec(
        num_scalar_prefetch=1,
        grid=(rows // tt,),
        in_specs=[page_spec(k) for k in range(TILE_PAGES)]
        + [pl.BlockSpec((tt, D_MODEL), lambda i, inv: (i, 0)),
           pl.BlockSpec((tt, LANES), lambda i, inv: (i, 0)),
           pl.BlockSpec((1, D_MODEL), lambda i, inv: (0, 0))],
        out_specs=pl.BlockSpec((tt, D_MODEL), lambda i, inv: (i, 0)),
    )
    return pl.pallas_call(
        _combine_kernel,
        grid_spec=grid_spec,
        out_shape=jax.ShapeDtypeStruct((rows, D_MODEL), F32),
        compiler_params=_params(1),
        name="combine",
    )(inv, *([out_pages] * TILE_PAGES), x1, route, norm_final)


def _page_tables(cnt, n_steps):
    n_tiles = cnt.shape[0]
    pg = (cnt + (PAGE_ROWS - 1)) // PAGE_ROWS
    lend = jnp.cumsum(pg, axis=1)
    loff = lend - pg
    cum_t = jnp.cumsum(pg, axis=0)
    pref = cum_t - pg
    tot = cum_t[-1]
    totp = ((tot + (STEP_PAGES - 1)) // STEP_PAGES) * STEP_PAGES
    gend = jnp.cumsum(totp)
    gstart = gend - totp
    gp = jnp.arange(n_steps * STEP_PAGES, dtype=jnp.int32)
    e_gp = jnp.minimum(jnp.sum((gend[None, :] <= gp[:, None]).astype(jnp.int32), axis=1), N_EXPERTS - 1)
    r = gp - gstart[e_gp]
    valid = (r < tot[e_gp]) & (gp < gend[-1])
    cum_col = cum_t.T[e_gp]
    i_gp = jnp.minimum(jnp.sum((cum_col <= r[:, None]).astype(jnp.int32), axis=1), n_tiles - 1)
    src = i_gp * TILE_PAGES + loff[i_gp, e_gp] + (r - pref[i_gp, e_gp])
    tbl = jnp.where(valid, src, 0).astype(jnp.int32)
    step_e = e_gp[::STEP_PAGES].astype(jnp.int32)
    n_used = (gend[-1] // STEP_PAGES).astype(jnp.int32).reshape(1)
    lp = jnp.arange(TILE_PAGES, dtype=jnp.int32)
    e_l = jnp.minimum(jnp.sum((lend[:, None, :] <= lp[None, :, None]).astype(jnp.int32), axis=2),
                      N_EXPERTS - 1)
    valid_l = lp[None, :] < lend[:, -1:]
    take = lambda a: jnp.take_along_axis(a, e_l, axis=1)
    gpos = gstart[e_l] + take(pref) + (lp[None, :] - take(loff))
    zero_page = (n_steps - 1) * STEP_PAGES
    inv = jnp.where(valid_l, gpos, zero_page).astype(jnp.int32).reshape(-1)
    return tbl, step_e, n_used, inv


def kernel(x_prompt, x_sample, mem_prompt, cache_mem_k, cache_mem_v, state_gdn, state_conv,
           norm_mix, w_in, b_gate, w_s, b_s, norm_a_v, w_conv, a_log, dt_bias, norm_gdn_out,
           norm_mem, w_mem_kv, w_o, norm_ffn, w_router_group, b_router_group, w_router_expert,
           b_router_expert, w_exp_gate, w_exp_up, w_exp_down, norm_final):
    depth = norm_mix.shape[0]
    assert depth == 1
    bsz, seq, _ = x_prompt.shape
    n_s = x_sample.shape[0]
    assert x_sample.shape[1] == 1 and seq % MIX_TILE == 0 and n_s % GS_REQ == 0 and n_s <= MIX_TILE
    l = 0
    row = lambda v: v.reshape(1, -1)

    wi = w_in[l]
    o_z = 2 * D_MODEL + C_QKV
    o_beta = o_z + D_MODEL
    o_qm = o_beta + 2 * B_HEADS
    o_gate = o_qm + D_MODEL
    w_main = jnp.concatenate([wi[:, :o_beta], wi[:, o_qm:]], axis=1).astype(BF16)
    w_ba = jnp.pad(wi[:, o_beta:o_qm], ((0, 0), (0, LANES - 2 * B_HEADS))).astype(BF16)
    pad_heads = lambda v: jnp.pad(v.reshape(1, B_HEADS), ((0, 0), (B_HEADS, LANES - 2 * B_HEADS)))
    alog_row = pad_heads(a_log[l])
    dt_row = pad_heads(dt_bias[l])
    w_kv = w_mem_kv[l].astype(BF16)
    wo = w_o[l].astype(BF16)
    wr_t = jnp.pad(jnp.concatenate([w_router_expert[l], w_router_group[l]], axis=1).T,
                   ((0, LANES - N_EXPERTS - N_GROUPS), (0, 0)))
    br_col = jnp.pad(jnp.concatenate([b_router_expert[l], b_router_group[l]]).reshape(-1, 1),
                     ((0, LANES - N_EXPERTS - N_GROUPS), (0, LANES - 1)))
    bs_t = jnp.pad(b_s[l].T, ((0, 0), (0, LANES - A_GROUPS)))
    ws0_row = jnp.repeat(w_s[l][:, 0, 0], A_CHUNK).reshape(1, D_MODEL)
    bs0_row = jnp.repeat(b_s[l][:, 0], A_CHUNK).reshape(1, D_MODEL)

    xp = x_prompt.reshape(bsz * seq, D_MODEL)
    xs_ = x_sample.reshape(n_s, D_MODEL)
    n_tiles_p = (bsz * seq) // MIX_TILE
    n_tiles = n_tiles_p + 1

    mk, mv = _memkv(mem_prompt.reshape(bsz * N_MEM, D_MODEL), row(norm_mem[l]), w_kv)
    p_p, ba_p = _proj(xp, row(norm_mix[l]), w_main, w_ba, row(norm_a_v[l]), row(b_gate[l]),
                      tm=min(1024, bsz * seq))
    ob_p, s_p = _gdn(p_p, ba_p, w_conv[l], alog_row, dt_row, row(norm_gdn_out[l]), bsz, seq)
    x1_p, xs_all, route_p, cnt_p = _mix(xp, p_p, ob_p, mk, mv, w_s[l], bs_t, wo, row(norm_ffn[l]),
                                        wr_t, br_col, seq, n_tiles)
    p_s, ba_s = _proj(xs_, row(norm_mix[l]), w_main, w_ba, row(norm_a_v[l]), row(b_gate[l]), tm=n_s)
    ob_s, s_s, c_s = _gdn_step(p_s, ba_s, state_conv[l], state_gdn[l], w_conv[l], alog_row, dt_row,
                               row(norm_gdn_out[l]))
    oc_s = _xattn1(p_s, cache_mem_k[l], cache_mem_v[l])
    x1_s, xs_all, route_s, cnt_s = _mix1(xs_, p_s, ob_s, oc_s, ws0_row, bs0_row, wo, row(norm_ffn[l]),
                                         wr_t, br_col, xs_all, n_tiles_p)
    cnt = jnp.concatenate([cnt_p[:, 0, :N_EXPERTS], cnt_s[:, 0, :N_EXPERTS]], axis=0).astype(jnp.int32)
    max_pages = n_tiles_p * TILE_PAGES + (2 * n_s) // PAGE_ROWS + N_EXPERTS
    n_steps = (max_pages + N_EXPERTS * (STEP_PAGES - 1)) // STEP_PAGES + 1
    tbl, step_e, n_used, inv = _page_tables(cnt, n_steps)
    ne = N_GROUPS * EXP_PER_GROUP
    out_sorted = _experts(tbl, step_e, n_used, xs_all.reshape(n_tiles * TILE_PAGES, PAGE_ROWS, D_MODEL),
                          w_exp_gate[l].reshape(ne, D_MODEL, D_EXPERT),
                          w_exp_up[l].reshape(ne, D_MODEL, D_EXPERT),
                          w_exp_down[l].reshape(ne, D_EXPERT, D_MODEL), n_steps)
    out_pages = out_sorted.reshape(n_steps * STEP_PAGES, PAGE_ROWS, D_MODEL)
    y_p = _combine(inv[:n_tiles_p * TILE_PAGES], out_pages, x1_p, route_p, row(norm_final), MIX_TILE)
    y_s = _combine(inv[n_tiles_p * TILE_PAGES:], out_pages, x1_s, route_s, row(norm_final), n_s)

    conv_tail = p_p.reshape(bsz, seq, PROJ_COLS)[:, seq - (B_CONV - 1):, 2 * D_MODEL:2 * D_MODEL + C_QKV]
    return (y_p.reshape(bsz, seq, D_MODEL),
            y_s.reshape(n_s, 1, D_MODEL),
            mk.reshape(1, bsz, N_MEM, M_HEADS, M_HEAD_DIM),
            mv.reshape(1, bsz, N_MEM, M_HEADS, M_HEAD_DIM),
            s_p[None],
            conv_tail[None],
            s_s[None],
            c_s[None],
            p_s[:, D_MODEL:2 * D_MODEL].reshape(1, n_s, 1, D_MODEL))
```

```python
import functools
import math

import jax
import jax.numpy as jnp
from jax import lax
from jax.experimental import pallas as pl
from jax.experimental.pallas import tpu as pltpu

F32 = jnp.float32
BF16 = jnp.bfloat16

D_MODEL = 1024
A_GROUPS = 8
A_CHUNK = 128
B_HEADS = 8
B_DK = 128
C_QKV = 3 * D_MODEL
B_CONV = 4
N_MEM = 256
M_HEADS = 4
M_HEAD_DIM = 256
N_GROUPS = 4
EXP_PER_GROUP = 8
N_EXPERTS = 32
D_EXPERT = 512
EPS = 1e-6

LANES = 128
GDN_CHUNK = 128
MIX_TILE = 256
PAGE_ROWS = 16
TILE_PAGES = 64
TILE_SLOTS = PAGE_ROWS * TILE_PAGES
STEP_PAGES = 16
STEP_ROWS = PAGE_ROWS * STEP_PAGES
PROJ_COLS = 10 * D_MODEL
PROJ_CHUNK = 256
PROJ_ROWS = 2048
VMEM_LIMIT = 56 * 1024 * 1024


def _params(n_grid):
    return pltpu.CompilerParams(dimension_semantics=("arbitrary",) * n_grid,
                                vmem_limit_bytes=VMEM_LIMIT)


def _rms(x, g):
    return x * lax.rsqrt(jnp.mean(x * x, axis=-1, keepdims=True) + EPS) * g


def _dot(a, b):
    return jnp.dot(a.astype(BF16), b.astype(BF16), preferred_element_type=F32)


def _dot_nt(a, b):
    return lax.dot_general(a.astype(BF16), b.astype(BF16), (((1,), (1,)), ((), ())),
                           preferred_element_type=F32)


def _dot_tn(a, b):
    return lax.dot_general(a.astype(BF16), b.astype(BF16), (((0,), (0,)), ((), ())),
                           preferred_element_type=F32)


def _split3(x):
    hi = x.astype(BF16)
    r = x - hi.astype(F32)
    mid = r.astype(BF16)
    lo = (r - mid.astype(F32)).astype(BF16)
    return hi, mid, lo


def _iota(shape, axis):
    return lax.broadcasted_iota(jnp.int32, shape, axis)


def _memkv_kernel(mem_ref, g_ref, w_ref, k_ref, v_ref):
    h = _rms(mem_ref[...], g_ref[...]).astype(BF16)
    kv = jnp.dot(h, w_ref[...], preferred_element_type=F32)
    k_ref[...] = kv[:, :D_MODEL]
    v_ref[...] = kv[:, D_MODEL:]


def _memkv(mem2d, norm_mem, w_kv_bf16):
    rows = mem2d.shape[0]
    nb = rows // N_MEM
    return pl.pallas_call(
        _memkv_kernel,
        grid=(nb,),
        in_specs=[pl.BlockSpec((N_MEM, D_MODEL), lambda i: (i, 0)),
                  pl.BlockSpec((1, D_MODEL), lambda i: (0, 0)),
                  pl.BlockSpec((D_MODEL, 2 * D_MODEL), lambda i: (0, 0))],
        out_specs=[pl.BlockSpec((N_MEM, D_MODEL), lambda i: (i, 0)),
                   pl.BlockSpec((N_MEM, D_MODEL), lambda i: (i, 0))],
        out_shape=[jax.ShapeDtypeStruct((rows, D_MODEL), F32)] * 2,
        compiler_params=_params(1),
        name="memkv",
    )(mem2d, norm_mem, w_kv_bf16)


def _proj_kernel(x_ref, nm_ref, w_ref, wba_ref, nav_ref, bg_ref, p_ref, ba_ref, hn_ref):
    j = pl.program_id(1)

    @pl.when(j == 0)
    def _():
        hn = _rms(x_ref[...], nm_ref[...]).astype(BF16)
        hn_ref[...] = hn
        ba_ref[...] = jnp.dot(hn, wba_ref[...], preferred_element_type=F32)

    def chunks(epilogue):
        for c0 in range(0, D_MODEL, PROJ_CHUNK):
            acc = jnp.dot(hn_ref[...], w_ref[:, c0:c0 + PROJ_CHUNK], preferred_element_type=F32)
            p_ref[:, c0:c0 + PROJ_CHUNK] = epilogue(acc, c0)

    @pl.when(j == 0)
    def _():
        chunks(lambda acc, c0: jax.nn.gelu(acc))

    @pl.when(j == 1)
    def _():
        ssq = []

        def gelu_ssq(acc, c0):
            a = jax.nn.gelu(acc)
            ssq.append(jnp.sum(a * a, axis=-1, keepdims=True))
            return a

        chunks(gelu_ssq)
        inv = lax.rsqrt(sum(ssq) * (1.0 / D_MODEL) + EPS)
        p_ref[...] = p_ref[...] * inv * nav_ref[...]

    @pl.when((j >= 2) & (j <= 6))
    def _():
        chunks(lambda acc, c0: acc)

    @pl.when(j >= 7)
    def _():
        chunks(lambda acc, c0: jax.nn.sigmoid(acc + bg_ref[:, c0:c0 + PROJ_CHUNK]))


def _proj(x2d, norm_mix, w_bf16, wba_bf16, norm_a_v, b_gate, tm):
    rows = x2d.shape[0]
    ncol = PROJ_COLS // D_MODEL
    return pl.pallas_call(
        _proj_kernel,
        grid=(rows // tm, ncol),
        in_specs=[pl.BlockSpec((tm, D_MODEL), lambda i, j: (i, 0)),
                  pl.BlockSpec((1, D_MODEL), lambda i, j: (0, 0)),
                  pl.BlockSpec((D_MODEL, D_MODEL), lambda i, j: (0, j)),
                  pl.BlockSpec((D_MODEL, LANES), lambda i, j: (0, 0)),
                  pl.BlockSpec((1, D_MODEL), lambda i, j: (0, 0)),
                  pl.BlockSpec((1, D_MODEL), lambda i, j: (0, jnp.maximum(j - 7, 0)))],
        out_specs=[pl.BlockSpec((tm, D_MODEL), lambda i, j: (i, j)),
                   pl.BlockSpec((tm, LANES), lambda i, j: (i, 0))],
        out_shape=[jax.ShapeDtypeStruct((rows, PROJ_COLS), F32),
                   jax.ShapeDtypeStruct((rows, LANES), F32)],
        scratch_shapes=[pltpu.VMEM((tm, D_MODEL), BF16)],
        compiler_params=_params(2),
        name="proj",
    )(x2d, norm_mix, w_bf16, wba_bf16, norm_a_v, b_gate)


def _gate_terms(ba, alog_row, dt_row):
    beta = jax.nn.sigmoid(ba)
    g = -jnp.exp(alog_row) * jax.nn.softplus(ba + dt_row)
    return beta, g


def _round_robin(tasks):
    done = object()
    while tasks:
        tasks = [t for t in tasks if next(t, done) is not done]


def _gdn_kernel(n_c, q_ref, k_ref, v_ref, ba_ref, z_ref, wc_ref, alog_ref, dt_ref, ng_ref,
                ob_ref, s_out_ref, s_ref, ext_ref, prep_ref, gate_ref):
    s = pl.program_id(0)
    C = GDN_CHUNK
    par = s % 2
    heads = range(B_HEADS)

    @pl.when(s == 0)
    def _():
        s_ref[...] = jnp.zeros_like(s_ref)
        ext_ref[0:8, :] = jnp.zeros((8, C_QKV), F32)
        prep_ref[...] = jnp.zeros_like(prep_ref)
        gate_ref[...] = jnp.zeros_like(gate_ref)

    row = _iota((C, C), 0)
    col = _iota((C, C), 1)
    incl = row >= col
    pace = []

    def mark(results):
        pace.append(results[B_HEADS - 1][0:1, 0:B_DK] * 0.0)

    def prepare():
        keep = (s % n_c != 0).astype(F32)
        ext_ref[0:8, :] = ext_ref[0:8, :] * keep
        ext_ref[8:8 + C, 0:D_MODEL] = q_ref[...]
        ext_ref[8:8 + C, D_MODEL:2 * D_MODEL] = k_ref[...]
        ext_ref[8:8 + C, 2 * D_MODEL:3 * D_MODEL] = v_ref[...]
        beta_all, g_all = _gate_terms(ba_ref[...], alog_ref[...], dt_ref[...])
        g_hi, g_mid, g_lo = _split3(g_all)
        tri = incl.astype(BF16)
        gcum = (jnp.dot(tri, g_hi, preferred_element_type=F32)
                + jnp.dot(tri, g_mid, preferred_element_type=F32)
                + jnp.dot(tri, g_lo, preferred_element_type=F32))
        gate_ref[par, 0] = beta_all
        gate_ref[par, 1] = gcum
        gate_ref[par, 2] = gcum.T
        yield

        def conv_silu(lo):
            pace_row = pace[-1] if pace else 0.0
            acc = (wc_ref[3:4, lo:lo + B_DK] + pace_row) * ext_ref[8:8 + C, lo:lo + B_DK]
            for j in range(1, B_CONV):
                acc = acc + ((wc_ref[3 - j:4 - j, lo:lo + B_DK] + pace_row)
                             * ext_ref[8 - j:8 - j + C, lo:lo + B_DK])
            return jax.nn.silu(acc)

        def l2n(t):
            return t * lax.rsqrt(jnp.sum(t * t, axis=-1, keepdims=True) + EPS)

        for h in heads:
            lo = h * B_DK
            prep_ref[par, :, lo:lo + B_DK] = l2n(conv_silu(lo)) * (B_DK ** -0.5)
            yield
            prep_ref[par, :, D_MODEL + lo:D_MODEL + lo + B_DK] = l2n(conv_silu(D_MODEL + lo))
            yield
            prep_ref[par, :, 2 * D_MODEL + lo:2 * D_MODEL + lo + B_DK] = conv_silu(2 * D_MODEL + lo)
            yield
        ext_ref[0:8, :] = ext_ref[C:C + 8, :]

    def chain():
        old = 1 - par
        strict = row > col
        eye = (row == col).astype(F32)
        same_blk = [(row // b) == (col // b) for b in (8, 16, 32, 64, 128)]
        beta_all = gate_ref[old, 0]
        gcum = gate_ref[old, 1]
        gcum_t = gate_ref[old, 2]
        q = [prep_ref[old, :, h * B_DK:(h + 1) * B_DK] for h in heads]
        k = [prep_ref[old, :, D_MODEL + h * B_DK:D_MODEL + (h + 1) * B_DK] for h in heads]
        v = [prep_ref[old, :, 2 * D_MODEL + h * B_DK:2 * D_MODEL + (h + 1) * B_DK] for h in heads]
        bcol = [beta_all[:, h:h + 1] for h in heads]
        gcol = [gcum[:, 8 + h:9 + h] for h in heads]
        glast = [gcum[C - 1:C, 8 + h:9 + h] for h in heads]
        eg = [jnp.exp(gcol[h]) for h in heads]
        dec = [jnp.where(incl, jnp.exp(jnp.where(incl, gcol[h] - gcum_t[8 + h:9 + h, :], 0.0)), 0.0)
               for h in heads]
        kb = [k[h] * bcol[h] for h in heads]
        yield
        a1 = [_dot_nt(jnp.concatenate([kb[h], q[h]], axis=0), k[h]) for h in heads]
        mark(a1)
        yield
        m = [jnp.where(strict, a1[h][:C] * dec[h], 0.0) for h in heads]
        attn = [a1[h][C:] * dec[h] for h in heads]
        m0 = [jnp.where(same_blk[0], m[h], 0.0) for h in heads]
        x = [eye - m0[h] for h in heads]
        pw = [_dot(m0[h], m0[h]) for h in heads]
        mark(pw)
        yield
        x = [x[h] + _dot(x[h], pw[h]) for h in heads]
        mark(x)
        yield
        pw = [_dot(pw[h], pw[h]) for h in heads]
        mark(pw)
        yield
        x = [x[h] + _dot(x[h], pw[h]) for h in heads]
        mark(x)
        yield
        for lvl in range(1, len(same_blk)):
            in_lvl = same_blk[lvl] & jnp.logical_not(same_blk[lvl - 1])
            xb = [x[h].astype(BF16) for h in heads]
            t = [_dot(xb[h], jnp.where(in_lvl, m[h], 0.0)) for h in heads]
            mark(t)
            yield
            x = [x[h] - _dot(t[h], xb[h]) for h in heads]
            mark(x)
            yield
        rhs = [jnp.concatenate([v[h] * bcol[h], kb[h] * eg[h]], axis=1) for h in heads]
        sol = [_dot(x[h], rhs[h]) for h in heads]
        mark(sol)
        yield
        keep = ((s + n_c - 1) % n_c != 0).astype(F32)
        s_old = [s_ref[h] * keep for h in heads]
        b1 = [_dot(jnp.concatenate([sol[h][:, B_DK:], q[h] * eg[h]], axis=0), s_old[h]) for h in heads]
        mark(b1)
        yield
        u = [sol[h][:, :B_DK] - b1[h][:C] for h in heads]
        o = [b1[h][C:] + _dot(attn[h], u[h]) for h in heads]
        kd = [k[h] * jnp.exp(glast[h] - gcol[h]) for h in heads]
        mark(o)
        yield
        s_new = [s_old[h] * jnp.exp(glast[h]) + _dot_tn(kd[h], u[h]) for h in heads]
        mark(s_new)
        yield
        for h in heads:
            s_ref[h] = s_new[h]
            s_out_ref[h] = s_new[h]
            zh = z_ref[:, h * B_DK:(h + 1) * B_DK]
            ob_ref[:, h * B_DK:(h + 1) * B_DK] = _rms(o[h], ng_ref[...]) * jax.nn.silu(zh)
            yield

    done = object()
    chain_task, prep_task = chain(), prepare()
    for _ in range(3):
        next(chain_task)
    alive = True
    while alive:
        alive = False
        for task in (prep_task, prep_task, chain_task):
            alive = (next(task, done) is not done) or alive


def _gdn(p, ba, w_conv, alog_row, dt_row, norm_gdn, bsz, seq):
    nc = seq // GDN_CHUNK
    total = bsz * nc
    cur = lambda s: jnp.minimum(s, total - 1)
    prev = lambda s: jnp.maximum(s - 1, 0)
    blk = lambda colblk: pl.BlockSpec((GDN_CHUNK, D_MODEL), lambda s: (cur(s), colblk))
    small = lambda shape: pl.BlockSpec(shape, lambda s: (0, 0))
    return pl.pallas_call(
        functools.partial(_gdn_kernel, nc),
        grid=(total + 1,),
        in_specs=[blk(2), blk(3), blk(4),
                  pl.BlockSpec((GDN_CHUNK, LANES), lambda s: (cur(s), 0)),
                  pl.BlockSpec((GDN_CHUNK, D_MODEL), lambda s: (prev(s), 5)),
                  small((B_CONV, C_QKV)), small((1, LANES)), small((1, LANES)), small((1, B_DK))],
        out_specs=[pl.BlockSpec((GDN_CHUNK, D_MODEL), lambda s: (prev(s), 0)),
                   pl.BlockSpec((None, B_HEADS, B_DK, B_DK), lambda s: (prev(s) // nc, 0, 0, 0))],
        out_shape=[jax.ShapeDtypeStruct((bsz * seq, D_MODEL), F32),
                   jax.ShapeDtypeStruct((bsz, B_HEADS, B_DK, B_DK), F32)],
        scratch_shapes=[pltpu.VMEM((B_HEADS, B_DK, B_DK), F32),
                        pltpu.VMEM((8 + GDN_CHUNK, C_QKV), F32),
                        pltpu.VMEM((2, GDN_CHUNK, C_QKV), F32),
                        pltpu.VMEM((2, 3, GDN_CHUNK, LANES), F32)],
        compiler_params=_params(1),
        name="gdn",
    )(p, p, p, ba, p, w_conv, alog_row, dt_row, norm_gdn)


GS_REQ = 8


def _gdn_step_kernel(q_ref, k_ref, v_ref, z_ref, ba_ref, sc_ref, sg_ref, wc_ref, alog_ref, dt_ref,
                     ng_ref, ob_ref, snew_ref, cnew_ref):
    beta_all, g_all = _gate_terms(ba_ref[...], alog_ref[...], dt_ref[...])
    eg_all = jnp.exp(g_all)
    pre = jnp.concatenate([q_ref[...], k_ref[...], v_ref[...]], axis=1)
    conv = wc_ref[3:4, :] * pre
    for j in range(B_CONV - 1):
        conv = conv + wc_ref[j:j + 1, :] * sc_ref[:, j, :]
    cnew_ref[:, 0, :] = sc_ref[:, 1, :]
    cnew_ref[:, 1, :] = sc_ref[:, 2, :]
    cnew_ref[:, 2, :] = pre
    qkv = jax.nn.silu(conv)
    zero_rows = jnp.zeros((LANES - GS_REQ, B_DK), F32)

    def l2n(t):
        return t * lax.rsqrt(jnp.sum(t * t, axis=-1, keepdims=True) + EPS)

    for h in range(B_HEADS):
        qh = l2n(qkv[:, h * B_DK:(h + 1) * B_DK]) * (B_DK ** -0.5)
        kh = l2n(qkv[:, D_MODEL + h * B_DK:D_MODEL + (h + 1) * B_DK])
        vh = qkv[:, 2 * D_MODEL + h * B_DK:2 * D_MODEL + (h + 1) * B_DK]
        q_t = jnp.concatenate([qh, zero_rows], axis=0).T
        k_t = jnp.concatenate([kh, zero_rows], axis=0).T
        o_rows = []
        for r in range(GS_REQ):
            kcol = k_t[:, r:r + 1]
            qcol = q_t[:, r:r + 1]
            s_old = sg_ref[r, h]
            beta = beta_all[r:r + 1, h:h + 1]
            eg = eg_all[r:r + 1, 8 + h:9 + h]
            ks = jnp.sum(s_old * kcol, axis=0, keepdims=True)
            u = beta * (vh[r:r + 1, :] - eg * ks)
            s_new = s_old * eg + kcol * u
            snew_ref[r, h] = s_new
            o_rows.append(jnp.sum(s_new * qcol, axis=0, keepdims=True))
        o = jnp.concatenate(o_rows, axis=0)
        zh = z_ref[:, h * B_DK:(h + 1) * B_DK]
        ob_ref[:, h * B_DK:(h + 1) * B_DK] = _rms(o, ng_ref[...]) * jax.nn.silu(zh)


def _gdn_step(p, ba, state_conv, state_gdn, w_conv, alog_row, dt_row, norm_gdn):
    n = p.shape[0]
    blk = lambda colblk: pl.BlockSpec((GS_REQ, D_MODEL), lambda i: (i, colblk))
    small = lambda shape: pl.BlockSpec(shape, lambda i: (0, 0))
    return pl.pallas_call(
        _gdn_step_kernel,
        grid=(n // GS_REQ,),
        in_specs=[blk(2), blk(3), blk(4), blk(5),
                  pl.BlockSpec((GS_REQ, LANES), lambda i: (i, 0)),
                  pl.BlockSpec((GS_REQ, B_CONV - 1, C_QKV), lambda i: (i, 0, 0)),
                  pl.BlockSpec((GS_REQ, B_HEADS, B_DK, B_DK), lambda i: (i, 0, 0, 0)),
                  small((B_CONV, C_QKV)), small((1, LANES)), small((1, LANES)), small((1, B_DK))],
        out_specs=[pl.BlockSpec((GS_REQ, D_MODEL), lambda i: (i, 0)),
                   pl.BlockSpec((GS_REQ, B_HEADS, B_DK, B_DK), lambda i: (i, 0, 0, 0)),
                   pl.BlockSpec((GS_REQ, B_CONV - 1, C_QKV), lambda i: (i, 0, 0))],
        out_shape=[jax.ShapeDtypeStruct((n, D_MODEL), F32),
                   jax.ShapeDtypeStruct((n, B_HEADS, B_DK, B_DK), F32),
                   jax.ShapeDtypeStruct((n, B_CONV - 1, C_QKV), F32)],
        compiler_params=_params(1),
        name="gdn_step",
    )(p, p, p, p, ba, state_conv, state_gdn, w_conv, alog_row, dt_row, norm_gdn)


XA_REQ = 4


def _xattn1_kernel(q_ref, ck_ref, cv_ref, oc_ref):
    scale = M_HEAD_DIM ** -0.5
    base = pl.program_id(0) * XA_REQ

    def body(r, carry):
        for h in range(M_HEADS):
            lo = h * M_HEAD_DIM
            q = q_ref[pl.ds(base + r, 1), lo:lo + M_HEAD_DIM]
            s = jnp.sum(ck_ref[r, :, h, :] * q, axis=-1, keepdims=True) * scale
            p = jnp.exp(s - jnp.max(s, axis=0, keepdims=True))
            pr = p / jnp.sum(p, axis=0, keepdims=True)
            oc_ref[pl.ds(base + r, 1), lo:lo + M_HEAD_DIM] = jnp.sum(pr * cv_ref[r, :, h, :], axis=0,
                                                                       keepdims=True)
        return carry

    lax.fori_loop(0, XA_REQ, body, 0)


def _xattn1(p, cache_k, cache_v):
    n = p.shape[0]
    cache_spec = pl.BlockSpec((XA_REQ, N_MEM, M_HEADS, M_HEAD_DIM), lambda i: (i, 0, 0, 0))
    return pl.pallas_call(
        _xattn1_kernel,
        grid=(n // XA_REQ,),
        in_specs=[pl.BlockSpec((n, D_MODEL), lambda i: (0, 6)), cache_spec, cache_spec],
        out_specs=pl.BlockSpec((n, D_MODEL), lambda i: (0, 0)),
        out_shape=jax.ShapeDtypeStruct((n, D_MODEL), F32),
        compiler_params=_params(1),
        name="xattn1",
    )(p, cache_k, cache_v)


def _route_and_dispatch(x1, nf_ref, wrt_ref, br_ref, xs_ref, route_ref, cnt_ref):
    tt = x1.shape[0]
    h2 = _rms(x1, nf_ref[...])
    h_hi, h_mid, h_lo = _split3(h2)
    w_hi, w_mid, w_lo = _split3(wrt_ref[...])
    yield

    def nt(a, b):
        return lax.dot_general(a, b, (((1,), (1,)), ((), ())), preferred_element_type=F32)

    logits = (nt(w_hi, h_hi) + (nt(w_hi, h_mid) + nt(w_mid, h_hi))
              + (nt(w_hi, h_lo) + nt(w_lo, h_hi) + nt(w_mid, h_mid))) + br_ref[:, 0:1]
    yield
    e_id = _iota((LANES, tt), 0).astype(F32)
    neg = jnp.float32(-jnp.inf)
    big = jnp.float32(1 << 20)
    is_grp = (e_id >= N_EXPERTS) & (e_id < N_EXPERTS + N_GROUPS)
    gl = jnp.where(is_grp, logits, neg)
    gmax = jnp.max(gl, axis=0, keepdims=True)
    g_lo = (jnp.min(jnp.where(gl == gmax, e_id, big), axis=0, keepdims=True) - N_EXPERTS) * EXP_PER_GROUP
    p_g = 1.0 / jnp.sum(jnp.exp(gl - gmax), axis=0, keepdims=True)
    yield
    in_grp = (e_id >= g_lo) & (e_id < g_lo + EXP_PER_GROUP)
    el = jnp.where(in_grp, logits, neg)
    v1 = jnp.max(el, axis=0, keepdims=True)
    i1 = jnp.min(jnp.where(el == v1, e_id, big), axis=0, keepdims=True)
    yield
    el2 = jnp.where(e_id == i1, neg, el)
    v2 = jnp.max(el2, axis=0, keepdims=True)
    i2 = jnp.min(jnp.where(el2 == v2, e_id, big), axis=0, keepdims=True)
    yield
    e2 = jnp.exp(v2 - v1)
    w1 = p_g / (1.0 + e2)
    w2 = p_g * e2 / (1.0 + e2)
    oh1 = (e_id == i1).astype(F32)
    oh2 = (e_id == i2).astype(F32)
    a_t = oh1 + oh2
    cnt = jnp.sum(a_t, axis=1, keepdims=True)
    pages = jnp.floor((cnt + (PAGE_ROWS - 1)) * (1.0 / PAGE_ROWS))
    yield
    lower = (_iota((LANES, LANES), 0) > _iota((LANES, LANES), 1)).astype(BF16)
    offp = jnp.dot(lower, jnp.broadcast_to(pages, (LANES, LANES)).astype(BF16),
                   preferred_element_type=F32)[:, 0:1]
    upper = (_iota((tt, tt), 0) < _iota((tt, tt), 1)).astype(BF16)
    rank = jnp.dot(a_t.astype(BF16), upper, preferred_element_type=F32)
    yield
    pos = offp * PAGE_ROWS + rank
    slot1 = jnp.sum(oh1 * pos, axis=0, keepdims=True)
    slot2 = jnp.sum(oh2 * pos, axis=0, keepdims=True)
    yield
    s_id = _iota((TILE_SLOTS, tt), 0).astype(F32)
    sel = jnp.where((s_id == slot1) | (s_id == slot2), 1.0, 0.0).astype(BF16)
    for r0 in range(0, TILE_SLOTS, TILE_SLOTS // 4):
        r1 = r0 + TILE_SLOTS // 4
        xs_ref[r0:r1, :] = jnp.dot(sel[r0:r1], h_hi, preferred_element_type=F32).astype(BF16)
        yield
    r_id = _iota((LANES, tt), 0)
    rows = (jnp.where(r_id == 0, slot1, 0.0) + jnp.where(r_id == 1, slot2, 0.0)
            + jnp.where(r_id == 2, w1, 0.0) + jnp.where(r_id == 3, w2, 0.0))
    route_ref[...] = rows.T
    cnt_ref[...] = jnp.broadcast_to(cnt, (LANES, LANES)).T[0:8, :]


def _merge(x, oa, ob, oc, ga, gb, gc, wo_ref):
    s = ga * oa + gb * ob + gc * oc
    return x + jnp.dot(s.astype(BF16), wo_ref[...], preferred_element_type=F32)


def _mix_kernel(x_ref, au_ref, av_ref, qm_ref, ga_ref, gb_ref, gc_ref, ob_ref, mk_ref, mv_ref,
                ws_ref, bst_ref, wo_ref, nf_ref, wrt_ref, br_ref,
                x1_ref, xs_ref, route_ref, cnt_ref, oa_ref, oc_ref, x1_prev_ref):
    i = pl.program_id(0)
    tt = x_ref.shape[0]

    @pl.when(i == 0)
    def _():
        x1_prev_ref[...] = jnp.zeros_like(x1_prev_ref)

    def mix_tile():
        causal = _iota((A_CHUNK, A_CHUNK), 0) >= _iota((A_CHUNK, A_CHUNK), 1)
        for g in range(A_GROUPS):
            wsg = jnp.where(causal, ws_ref[g], 0.0).astype(BF16)
            bcol = bst_ref[:, g:g + 1]
            lo = g * A_CHUNK
            for cc in range(tt // A_CHUNK):
                r0 = cc * A_CHUNK
                sv = jnp.dot(wsg, av_ref[r0:r0 + A_CHUNK, lo:lo + A_CHUNK].astype(BF16),
                             preferred_element_type=F32) + bcol
                oa_ref[r0:r0 + A_CHUNK, lo:lo + A_CHUNK] = au_ref[r0:r0 + A_CHUNK, lo:lo + A_CHUNK] * sv
            if g % 2 == 1:
                yield
        scale = M_HEAD_DIM ** -0.5
        for h in range(M_HEADS):
            lo = h * M_HEAD_DIM
            s = _dot_nt(qm_ref[:, lo:lo + M_HEAD_DIM], mk_ref[:, lo:lo + M_HEAD_DIM]) * scale
            yield
            p = jnp.exp(s - jnp.max(s, axis=-1, keepdims=True))
            pr = p / jnp.sum(p, axis=-1, keepdims=True)
            oc_ref[:, lo:lo + M_HEAD_DIM] = _dot(pr, mv_ref[:, lo:lo + M_HEAD_DIM])
            yield
        x1 = _merge(x_ref[...], oa_ref[...], ob_ref[...], oc_ref[...],
                    ga_ref[...], gb_ref[...], gc_ref[...], wo_ref)
        x1_ref[...] = x1
        x1_prev_ref[i % 2] = x1

    _round_robin([_route_and_dispatch(x1_prev_ref[(i + 1) % 2], nf_ref, wrt_ref, br_ref, xs_ref,
                                      route_ref, cnt_ref),
                  mix_tile()])


def _mix1_kernel(x_ref, au_ref, av_ref, ga_ref, gb_ref, gc_ref, ob_ref, oc_ref, ws0_ref, bs0_ref,
                 wo_ref, nf_ref, wrt_ref, br_ref, xs_in_ref,
                 x1_ref, xs_ref, route_ref, cnt_ref):
    del xs_in_ref
    oa = au_ref[...] * (ws0_ref[...] * av_ref[...] + bs0_ref[...])
    x1 = _merge(x_ref[...], oa, ob_ref[...], oc_ref[...], ga_ref[...], gb_ref[...], gc_ref[...], wo_ref)
    x1_ref[...] = x1
    for _ in _route_and_dispatch(x1, nf_ref, wrt_ref, br_ref, xs_ref, route_ref, cnt_ref):
        pass


def _route_out_specs(tt, x1_of, slot_of, route_of):
    return [pl.BlockSpec((tt, D_MODEL), lambda i: (x1_of(i), 0)),
            pl.BlockSpec((None, TILE_SLOTS, D_MODEL), lambda i: (slot_of(i), 0, 0)),
            pl.BlockSpec((tt, LANES), lambda i: (route_of(i), 0)),
            pl.BlockSpec((None, 8, LANES), lambda i: (route_of(i), 0, 0))]


def _mix(x2d, p, ob, mk, mv, w_s, bs_t, wo_bf16, norm_ffn, wr_t, br_col, seq):
    rows = x2d.shape[0]
    tt = MIX_TILE
    nt = rows // tt
    per_b = seq // tt
    cur = lambda i: jnp.minimum(i, nt - 1)
    prev = lambda i: jnp.maximum(i - 1, 0)
    pblk = lambda colblk: pl.BlockSpec((tt, D_MODEL), lambda i: (cur(i), colblk))
    const = lambda shape: pl.BlockSpec(shape, lambda i: (0,) * len(shape))
    return pl.pallas_call(
        _mix_kernel,
        grid=(nt + 1,),
        in_specs=[pblk(0), pblk(0), pblk(1), pblk(6), pblk(7), pblk(8), pblk(9), pblk(0),
                  pl.BlockSpec((N_MEM, D_MODEL), lambda i: (cur(i) // per_b, 0)),
                  pl.BlockSpec((N_MEM, D_MODEL), lambda i: (cur(i) // per_b, 0)),
                  const((A_GROUPS, A_CHUNK, A_CHUNK)), const((A_CHUNK, LANES)),
                  const((D_MODEL, D_MODEL)), const((1, D_MODEL)),
                  const((LANES, D_MODEL)), const((LANES, LANES))],
        out_specs=_route_out_specs(tt, cur, lambda i: i, prev),
        out_shape=[jax.ShapeDtypeStruct((rows, D_MODEL), F32),
                   jax.ShapeDtypeStruct((nt + 1, TILE_SLOTS, D_MODEL), BF16),
                   jax.ShapeDtypeStruct((rows, LANES), F32),
                   jax.ShapeDtypeStruct((nt, 8, LANES), F32)],
        scratch_shapes=[pltpu.VMEM((tt, D_MODEL), F32), pltpu.VMEM((tt, D_MODEL), F32),
                        pltpu.VMEM((2, tt, D_MODEL), F32)],
        compiler_params=_params(1),
        name="mix",
    )(x2d, p, p, p, p, p, p, ob, mk, mv, w_s, bs_t, wo_bf16, norm_ffn, wr_t, br_col)


def _mix1(x2d, p, ob, oc, ws0_row, bs0_row, wo_bf16, norm_ffn, wr_t, br_col, xs_all):
    rows = x2d.shape[0]
    pblk = lambda colblk: pl.BlockSpec((rows, D_MODEL), lambda i: (0, colblk))
    const = lambda shape: pl.BlockSpec(shape, lambda i: (0,) * len(shape))
    zero = lambda i: 0
    return pl.pallas_call(
        _mix1_kernel,
        grid=(1,),
        in_specs=[pblk(0), pblk(0), pblk(1), pblk(7), pblk(8), pblk(9), pblk(0), pblk(0),
                  const((1, D_MODEL)), const((1, D_MODEL)),
                  const((D_MODEL, D_MODEL)), const((1, D_MODEL)),
                  const((LANES, D_MODEL)), const((LANES, LANES)),
                  pl.BlockSpec(memory_space=pl.ANY)],
        out_specs=_route_out_specs(rows, zero, zero, zero),
        out_shape=[jax.ShapeDtypeStruct((rows, D_MODEL), F32),
                   jax.ShapeDtypeStruct(xs_all.shape, BF16),
                   jax.ShapeDtypeStruct((rows, LANES), F32),
                   jax.ShapeDtypeStruct((1, 8, LANES), F32)],
        input_output_aliases={14: 1},
        compiler_params=_params(1),
        name="mix1",
    )(x2d, p, p, p, p, p, ob, oc, ws0_row, bs0_row, wo_bf16, norm_ffn, wr_t, br_col, xs_all)


def _experts_kernel(tbl_ref, se_ref, nu_ref, *refs):
    del tbl_ref
    pages = refs[:STEP_PAGES]
    wg_ref, wu_ref, wd_ref, o_ref, wgb_ref, wub_ref, wdb_ref = refs[STEP_PAGES:]
    s = pl.program_id(0)
    prev = se_ref[jnp.maximum(s - 1, 0)]

    @pl.when((s == 0) | (se_ref[s] != prev))
    def _():
        wgb_ref[...] = wg_ref[...].astype(BF16)
        wub_ref[...] = wu_ref[...].astype(BF16)
        wdb_ref[...] = wd_ref[...].astype(BF16)

    @pl.when(s < nu_ref[0])
    def _():
        x = jnp.concatenate([pg[...] for pg in pages], axis=0)
        gate = jnp.dot(x, wgb_ref[...], preferred_element_type=F32)
        up = jnp.dot(x, wub_ref[...], preferred_element_type=F32)
        act = (jax.nn.silu(gate) * up).astype(BF16)
        o_ref[...] = jnp.dot(act, wdb_ref[...], preferred_element_type=F32).astype(BF16)

    @pl.when(s >= nu_ref[0])
    def _():
        o_ref[...] = jnp.zeros_like(o_ref)


def _experts(tbl, step_e, n_used, xs_pages, w_gate, w_up, w_down, n_steps):
    def page_spec(k):
        return pl.BlockSpec((None, PAGE_ROWS, D_MODEL),
                            lambda s, tbl, se, nu: (tbl[s * STEP_PAGES + k], 0, 0))

    wspec = lambda shape: pl.BlockSpec((None,) + shape, lambda s, tbl, se, nu: (se[s], 0, 0))
    grid_spec = pltpu.PrefetchScalarGridSpec(
        num_scalar_prefetch=3,
        grid=(n_steps,),
        in_specs=[page_spec(k) for k in range(STEP_PAGES)]
        + [wspec((D_MODEL, D_EXPERT)), wspec((D_MODEL, D_EXPERT)), wspec((D_EXPERT, D_MODEL))],
        out_specs=pl.BlockSpec((STEP_ROWS, D_MODEL), lambda s, tbl, se, nu: (s, 0)),
        scratch_shapes=[pltpu.VMEM((D_MODEL, D_EXPERT), BF16), pltpu.VMEM((D_MODEL, D_EXPERT), BF16),
                        pltpu.VMEM((D_EXPERT, D_MODEL), BF16)],
    )
    return pl.pallas_call(
        _experts_kernel,
        grid_spec=grid_spec,
        out_shape=jax.ShapeDtypeStruct((n_steps * STEP_ROWS, D_MODEL), BF16),
        compiler_params=_params(1),
        name="experts",
    )(tbl, step_e, n_used, *([xs_pages] * STEP_PAGES), w_gate, w_up, w_down)


def _combine_kernel(inv_ref, *refs):
    del inv_ref
    pages = refs[:TILE_PAGES]
    x1_ref, route_ref, nfin_ref, y_ref = refs[TILE_PAGES:]
    out_loc = jnp.concatenate([pg[...] for pg in pages], axis=0)
    tt = x1_ref.shape[0]
    route = route_ref[...]
    s_id = _iota((tt, TILE_SLOTS), 1).astype(F32)
    sel1 = jnp.where(s_id == route[:, 0:1], 1.0, 0.0).astype(BF16)
    sel2 = jnp.where(s_id == route[:, 1:2], 1.0, 0.0).astype(BF16)
    moe = (route[:, 2:3] * jnp.dot(sel1, out_loc, preferred_element_type=F32)
           + route[:, 3:4] * jnp.dot(sel2, out_loc, preferred_element_type=F32))
    y_ref[...] = _rms(x1_ref[...] + moe, nfin_ref[...])


def _combine(inv, out_pages, x1, route, norm_final, tt):
    rows = x1.shape[0]

    def page_spec(k):
        return pl.BlockSpec((None, PAGE_ROWS, D_MODEL), lambda i, inv: (inv[i * TILE_PAGES + k], 0, 0))

    grid_spec = pltpu.PrefetchScalarGridSpec(
        num_scalar_prefetch=1,
        grid=(rows // tt,),
        in_specs=[page_spec(k) for k in range(TILE_PAGES)]
        + [pl.BlockSpec((tt, D_MODEL), lambda i, inv: (i, 0)),
           pl.BlockSpec((tt, LANES), lambda i, inv: (i, 0)),
           pl.BlockSpec((1, D_MODEL), lambda i, inv: (0, 0))],
        out_specs=pl.BlockSpec((tt, D_MODEL), lambda i, inv: (i, 0)),
    )
    return pl.pallas_call(
        _combine_kernel,
        grid_spec=grid_spec,
        out_shape=jax.ShapeDtypeStruct((rows, D_MODEL), F32),
        compiler_params=_params(1),
        name="combine",
    )(inv, *([out_pages] * TILE_PAGES), x1, route, norm_final)


def _page_tables(cnt, n_steps):
    n_tiles = cnt.shape[0]
    pg = (cnt + (PAGE_ROWS - 1)) // PAGE_ROWS
    lend = jnp.cumsum(pg, axis=1)
    loff = lend - pg
    cum_t = jnp.cumsum(pg, axis=0)
    pref = cum_t - pg
    tot = cum_t[-1]
    totp = ((tot + (STEP_PAGES - 1)) // STEP_PAGES) * STEP_PAGES
    gend = jnp.cumsum(totp)
    gstart = gend - totp
    sel = lambda onehot, vals: jnp.dot(onehot, vals.astype(F32), precision=lax.Precision.HIGHEST)
    gp = jnp.arange(n_steps * STEP_PAGES, dtype=jnp.int32)
    in_e = (gp[:, None] >= gstart[None, :]) & (gp[:, None] < gend[None, :])
    oh_e = in_e.astype(F32)
    r = gp.astype(F32) - sel(oh_e, gstart)
    valid = jnp.any(in_e, axis=1) & (r < sel(oh_e, tot))
    cum_col = sel(oh_e, cum_t.T)
    pref_col = sel(oh_e, pref.T)
    loff_col = sel(oh_e, loff.T)
    in_i = (r[:, None] >= pref_col) & (r[:, None] < cum_col)
    tile_base = (jnp.arange(n_tiles, dtype=jnp.int32) * TILE_PAGES).astype(F32)
    src = jnp.sum(jnp.where(in_i, tile_base[None, :] + loff_col + (r[:, None] - pref_col), 0.0), axis=1)
    tbl = jnp.where(valid, src, 0.0).astype(jnp.int32)
    in_step = in_e[::STEP_PAGES]
    e_ids = jnp.arange(N_EXPERTS, dtype=jnp.int32)
    step_e = jnp.where(jnp.any(in_step, axis=1), jnp.sum(jnp.where(in_step, e_ids[None, :], 0), axis=1),
                       N_EXPERTS - 1).astype(jnp.int32)
    n_used = (gend[-1] // STEP_PAGES).astype(jnp.int32).reshape(1)
    lp = jnp.arange(TILE_PAGES, dtype=jnp.int32)[None, :, None]
    in_l = (lp >= loff[:, None, :]) & (lp < lend[:, None, :])
    gpos = jnp.sum(jnp.where(in_l, gstart[None, None, :] + pref[:, None, :] + lp - loff[:, None, :], 0), axis=2)
    zero_page = (n_steps - 1) * STEP_PAGES
    inv = jnp.where(jnp.any(in_l, axis=2), gpos, zero_page).astype(jnp.int32).reshape(-1)
    return tbl, step_e, n_used, inv


def kernel(x_prompt, x_sample, mem_prompt, cache_mem_k, cache_mem_v, state_gdn, state_conv,
           norm_mix, w_in, b_gate, w_s, b_s, norm_a_v, w_conv, a_log, dt_bias, norm_gdn_out,
           norm_mem, w_mem_kv, w_o, norm_ffn, w_router_group, b_router_group, w_router_expert,
           b_router_expert, w_exp_gate, w_exp_up, w_exp_down, norm_final):
    depth = norm_mix.shape[0]
    assert depth == 1
    bsz, seq, _ = x_prompt.shape
    n_s = x_sample.shape[0]
    assert x_sample.shape[1] == 1 and seq % MIX_TILE == 0 and n_s % GS_REQ == 0 and n_s <= MIX_TILE
    l = 0
    row = lambda v: v.reshape(1, -1)

    wi = w_in[l]
    o_z = 2 * D_MODEL + C_QKV
    o_beta = o_z + D_MODEL
    o_qm = o_beta + 2 * B_HEADS
    o_gate = o_qm + D_MODEL
    w_main = jnp.concatenate([wi[:, :o_beta], wi[:, o_qm:]], axis=1).astype(BF16)
    w_ba = jnp.pad(wi[:, o_beta:o_qm], ((0, 0), (0, LANES - 2 * B_HEADS))).astype(BF16)
    pad_heads = lambda v: jnp.pad(v.reshape(1, B_HEADS), ((0, 0), (B_HEADS, LANES - 2 * B_HEADS)))
    alog_row = pad_heads(a_log[l])
    dt_row = pad_heads(dt_bias[l])
    w_kv = w_mem_kv[l].astype(BF16)
    wo = w_o[l].astype(BF16)
    wr_t = jnp.pad(jnp.concatenate([w_router_expert[l], w_router_group[l]], axis=1).T,
                   ((0, LANES - N_EXPERTS - N_GROUPS), (0, 0)))
    br_col = jnp.pad(jnp.concatenate([b_router_expert[l], b_router_group[l]]).reshape(-1, 1),
                     ((0, LANES - N_EXPERTS - N_GROUPS), (0, LANES - 1)))
    bs_t = jnp.pad(b_s[l].T, ((0, 0), (0, LANES - A_GROUPS)))
    ws0_row = jnp.repeat(w_s[l][:, 0, 0], A_CHUNK).reshape(1, D_MODEL)
    bs0_row = jnp.repeat(b_s[l][:, 0], A_CHUNK).reshape(1, D_MODEL)

    xp = x_prompt.reshape(bsz * seq, D_MODEL)
    xs_ = x_sample.reshape(n_s, D_MODEL)
    n_tiles_p = (bsz * seq) // MIX_TILE
    n_tiles = n_tiles_p + 1

    mk, mv = _memkv(mem_prompt.reshape(bsz * N_MEM, D_MODEL), row(norm_mem[l]), w_kv)
    p_p, ba_p = _proj(xp, row(norm_mix[l]), w_main, w_ba, row(norm_a_v[l]), row(b_gate[l]),
                      tm=min(PROJ_ROWS, bsz * seq))
    ob_p, s_p = _gdn(p_p, ba_p, w_conv[l], alog_row, dt_row, row(norm_gdn_out[l]), bsz, seq)
    x1_p, xs_all, route_p, cnt_p = _mix(xp, p_p, ob_p, mk, mv, w_s[l], bs_t, wo, row(norm_ffn[l]),
                                        wr_t, br_col, seq)
    p_s, ba_s = _proj(xs_, row(norm_mix[l]), w_main, w_ba, row(norm_a_v[l]), row(b_gate[l]), tm=n_s)
    ob_s, s_s, c_s = _gdn_step(p_s, ba_s, state_conv[l], state_gdn[l], w_conv[l], alog_row, dt_row,
                               row(norm_gdn_out[l]))
    oc_s = _xattn1(p_s, cache_mem_k[l], cache_mem_v[l])
    x1_s, xs_all, route_s, cnt_s = _mix1(xs_, p_s, ob_s, oc_s, ws0_row, bs0_row, wo, row(norm_ffn[l]),
                                         wr_t, br_col, xs_all)
    cnt = jnp.concatenate([cnt_s[:, 0, :N_EXPERTS], cnt_p[:, 0, :N_EXPERTS]], axis=0).astype(jnp.int32)
    max_pages = n_tiles_p * TILE_PAGES + (2 * n_s) // PAGE_ROWS + N_EXPERTS
    n_steps = (max_pages + N_EXPERTS * (STEP_PAGES - 1)) // STEP_PAGES + 1
    tbl, step_e, n_used, inv = _page_tables(cnt, n_steps)
    ne = N_GROUPS * EXP_PER_GROUP
    out_sorted = _experts(tbl, step_e, n_used, xs_all.reshape(n_tiles * TILE_PAGES, PAGE_ROWS, D_MODEL),
                          w_exp_gate[l].reshape(ne, D_MODEL, D_EXPERT),
                          w_exp_up[l].reshape(ne, D_MODEL, D_EXPERT),
                          w_exp_down[l].reshape(ne, D_EXPERT, D_MODEL), n_steps)
    out_pages = out_sorted.reshape(n_steps * STEP_PAGES, PAGE_ROWS, D_MODEL)
    y_s = _combine(inv[:TILE_PAGES], out_pages, x1_s, route_s, row(norm_final), n_s)
    y_p = _combine(inv[TILE_PAGES:], out_pages, x1_p, route_p, row(norm_final), MIX_TILE)

    conv_tail = p_p.reshape(bsz, seq, PROJ_COLS)[:, seq - (B_CONV - 1):, 2 * D_MODEL:2 * D_MODEL + C_QKV]
    return (y_p.reshape(bsz, seq, D_MODEL),
            y_s.reshape(n_s, 1, D_MODEL),
            mk.reshape(1, bsz, N_MEM, M_HEADS, M_HEAD_DIM),
            mv.reshape(1, bsz, N_MEM, M_HEADS, M_HEAD_DIM),
            s_p[None],
            conv_tail[None],
            s_s[None],
            c_s[None],
            p_s[:, D_MODEL:2 * D_MODEL].reshape(1, n_s, 1, D_MODEL))
```

```python
import functools
import math

import jax
import jax.numpy as jnp
from jax import lax
from jax.experimental import pallas as pl
from jax.experimental.pallas import tpu as pltpu

F32 = jnp.float32
BF16 = jnp.bfloat16

D_MODEL = 1024
A_GROUPS = 8
A_CHUNK = 128
B_HEADS = 8
B_DK = 128
C_QKV = 3 * D_MODEL
B_CONV = 4
N_MEM = 256
M_HEADS = 4
M_HEAD_DIM = 256
N_GROUPS = 4
EXP_PER_GROUP = 8
N_EXPERTS = 32
D_EXPERT = 512
EPS = 1e-6

LANES = 128
GDN_CHUNK = 128
GDN_SEQS = 2
MIX_TILE = 256
PAGE_ROWS = 16
TILE_PAGES = 64
TILE_SLOTS = PAGE_ROWS * TILE_PAGES
STEP_PAGES = 16
STEP_ROWS = PAGE_ROWS * STEP_PAGES
PROJ_COLS = 10 * D_MODEL
PROJ_CHUNK = 256
PROJ_ROWS = 2048
PROJ_PIECE_ROWS = 1024
VMEM_LIMIT = 56 * 1024 * 1024


def _params(n_grid):
    return pltpu.CompilerParams(dimension_semantics=("arbitrary",) * n_grid,
                                vmem_limit_bytes=VMEM_LIMIT)


def _rms(x, g):
    return x * lax.rsqrt(jnp.mean(x * x, axis=-1, keepdims=True) + EPS) * g


def _dot(a, b):
    return jnp.dot(a.astype(BF16), b.astype(BF16), preferred_element_type=F32)


def _dot_nt(a, b):
    return lax.dot_general(a.astype(BF16), b.astype(BF16), (((1,), (1,)), ((), ())),
                           preferred_element_type=F32)


def _dot_tn(a, b):
    return lax.dot_general(a.astype(BF16), b.astype(BF16), (((0,), (0,)), ((), ())),
                           preferred_element_type=F32)


def _split3(x):
    hi = x.astype(BF16)
    r = x - hi.astype(F32)
    mid = r.astype(BF16)
    lo = (r - mid.astype(F32)).astype(BF16)
    return hi, mid, lo


def _iota(shape, axis):
    return lax.broadcasted_iota(jnp.int32, shape, axis)


def _memkv_kernel(mem_ref, g_ref, w_ref, k_ref, v_ref):
    h = _rms(mem_ref[...], g_ref[...]).astype(BF16)
    kv = jnp.dot(h, w_ref[...], preferred_element_type=F32)
    k_ref[...] = kv[:, :D_MODEL]
    v_ref[...] = kv[:, D_MODEL:]


def _memkv(mem2d, norm_mem, w_kv_bf16):
    rows = mem2d.shape[0]
    nb = rows // N_MEM
    return pl.pallas_call(
        _memkv_kernel,
        grid=(nb,),
        in_specs=[pl.BlockSpec((N_MEM, D_MODEL), lambda i: (i, 0)),
                  pl.BlockSpec((1, D_MODEL), lambda i: (0, 0)),
                  pl.BlockSpec((D_MODEL, 2 * D_MODEL), lambda i: (0, 0))],
        out_specs=[pl.BlockSpec((N_MEM, D_MODEL), lambda i: (i, 0)),
                   pl.BlockSpec((N_MEM, D_MODEL), lambda i: (i, 0))],
        out_shape=[jax.ShapeDtypeStruct((rows, D_MODEL), F32)] * 2,
        compiler_params=_params(1),
        name="memkv",
    )(mem2d, norm_mem, w_kv_bf16)


def _proj_kernel(x_ref, nm_ref, w_ref, wba_ref, nav_ref, bg_ref, p_ref, ba_ref, hn_ref):
    j = pl.program_id(1)

    @pl.when(j == 0)
    def _():
        hn = _rms(x_ref[...], nm_ref[...]).astype(BF16)
        hn_ref[...] = hn
        ba_ref[...] = jnp.dot(hn, wba_ref[...], preferred_element_type=F32)

    tm = x_ref.shape[0]
    rows = [(r0, min(tm, r0 + PROJ_PIECE_ROWS)) for r0 in range(0, tm, PROJ_PIECE_ROWS)]
    pieces = [(r, c0) for r in rows for c0 in range(0, D_MODEL, PROJ_CHUNK)]

    def chunks(epilogue):
        def matmul(piece):
            (r0, r1), c0 = piece
            return jnp.dot(hn_ref[r0:r1, :], w_ref[:, c0:c0 + PROJ_CHUNK], preferred_element_type=F32)

        def finish(piece, acc):
            (r0, r1), c0 = piece
            p_ref[r0:r1, c0:c0 + PROJ_CHUNK] = epilogue(acc, piece)

        acc = matmul(pieces[0])
        for n in range(1, len(pieces)):
            nxt = matmul(pieces[n])
            finish(pieces[n - 1], acc)
            acc = nxt
        finish(pieces[-1], acc)

    @pl.when(j == 0)
    def _():
        chunks(lambda acc, piece: jax.nn.gelu(acc))

    @pl.when(j == 1)
    def _():
        ssq = {r: [] for r in rows}

        def gelu_ssq(acc, piece):
            a = jax.nn.gelu(acc)
            ssq[piece[0]].append(jnp.sum(a * a, axis=-1, keepdims=True))
            return a

        chunks(gelu_ssq)
        for r0, r1 in rows:
            inv = lax.rsqrt(sum(ssq[(r0, r1)]) * (1.0 / D_MODEL) + EPS)
            p_ref[r0:r1, :] = p_ref[r0:r1, :] * inv * nav_ref[...]

    @pl.when((j >= 2) & (j <= 6))
    def _():
        chunks(lambda acc, piece: acc)

    @pl.when(j >= 7)
    def _():
        chunks(lambda acc, piece: jax.nn.sigmoid(acc + bg_ref[:, piece[1]:piece[1] + PROJ_CHUNK]))


def _proj(x2d, norm_mix, w_bf16, wba_bf16, norm_a_v, b_gate, tm):
    rows = x2d.shape[0]
    ncol = PROJ_COLS // D_MODEL
    return pl.pallas_call(
        _proj_kernel,
        grid=(rows // tm, ncol),
        in_specs=[pl.BlockSpec((tm, D_MODEL), lambda i, j: (i, 0)),
                  pl.BlockSpec((1, D_MODEL), lambda i, j: (0, 0)),
                  pl.BlockSpec((D_MODEL, D_MODEL), lambda i, j: (0, j)),
                  pl.BlockSpec((D_MODEL, LANES), lambda i, j: (0, 0)),
                  pl.BlockSpec((1, D_MODEL), lambda i, j: (0, 0)),
                  pl.BlockSpec((1, D_MODEL), lambda i, j: (0, jnp.maximum(j - 7, 0)))],
        out_specs=[pl.BlockSpec((tm, D_MODEL), lambda i, j: (i, j)),
                   pl.BlockSpec((tm, LANES), lambda i, j: (i, 0))],
        out_shape=[jax.ShapeDtypeStruct((rows, PROJ_COLS), F32),
                   jax.ShapeDtypeStruct((rows, LANES), F32)],
        scratch_shapes=[pltpu.VMEM((tm, D_MODEL), BF16)],
        compiler_params=_params(2),
        name="proj",
    )(x2d, norm_mix, w_bf16, wba_bf16, norm_a_v, b_gate)


def _gate_terms(ba, alog_row, dt_row):
    beta = jax.nn.sigmoid(ba)
    g = -jnp.exp(alog_row) * jax.nn.softplus(ba + dt_row)
    return beta, g


class _Lazy:
    def __init__(self, make):
        self._make = make

    def __getitem__(self, index):
        return self._make(index)


def _round_robin(tasks):
    done = object()
    while tasks:
        tasks = [t for t in tasks if next(t, done) is not done]


def _gdn_kernel(n_c, *refs):
    per = 5
    q_refs, k_refs, v_refs, ba_refs, z_refs = (
        [refs[per * g + j] for g in range(GDN_SEQS)] for j in range(per))
    wc_ref, alog_ref, dt_ref, ng_ref = refs[per * GDN_SEQS:per * GDN_SEQS + 4]
    ob_refs = refs[per * GDN_SEQS + 4:per * GDN_SEQS + 4 + GDN_SEQS]
    s_out_refs = refs[per * GDN_SEQS + 4 + GDN_SEQS:per * GDN_SEQS + 4 + 2 * GDN_SEQS]
    s_ref, ext_ref, prep_ref, gate_ref = refs[per * GDN_SEQS + 4 + 2 * GDN_SEQS:]
    s = pl.program_id(0)
    C = GDN_CHUNK
    par = s % 2
    heads = range(B_HEADS)

    @pl.when(s == 0)
    def _():
        s_ref[...] = jnp.zeros_like(s_ref)
        ext_ref[:, 0:8, :] = jnp.zeros((GDN_SEQS, 8, C_QKV), F32)
        prep_ref[...] = jnp.zeros_like(prep_ref)
        gate_ref[...] = jnp.zeros_like(gate_ref)

    row = _iota((C, C), 0)
    col = _iota((C, C), 1)
    incl = row >= col
    pace = []

    def mark(results):
        pace.append(results[-1][0:1, 0:B_DK] * 0.0)

    def prepare(g):
        q_ref, k_ref, v_ref, ba_ref = q_refs[g], k_refs[g], v_refs[g], ba_refs[g]
        keep = (s % n_c != 0).astype(F32)
        ext_ref[g, 0:8, :] = ext_ref[g, 0:8, :] * keep
        ext_ref[g, 8:8 + C, 0:D_MODEL] = q_ref[...]
        ext_ref[g, 8:8 + C, D_MODEL:2 * D_MODEL] = k_ref[...]
        ext_ref[g, 8:8 + C, 2 * D_MODEL:3 * D_MODEL] = v_ref[...]
        beta_all, g_all = _gate_terms(ba_ref[...], alog_ref[...], dt_ref[...])
        g_hi, g_mid, g_lo = _split3(g_all)
        tri = incl.astype(BF16)
        gcum = (jnp.dot(tri, g_hi, preferred_element_type=F32)
                + jnp.dot(tri, g_mid, preferred_element_type=F32)
                + jnp.dot(tri, g_lo, preferred_element_type=F32))
        gate_ref[par, g, 0] = beta_all
        gate_ref[par, g, 1] = gcum
        gate_ref[par, g, 2] = gcum.T
        yield

        def conv_silu(lo):
            pace_row = pace[-1] if pace else 0.0
            acc = (wc_ref[3:4, lo:lo + B_DK] + pace_row) * ext_ref[g, 8:8 + C, lo:lo + B_DK]
            for j in range(1, B_CONV):
                acc = acc + ((wc_ref[3 - j:4 - j, lo:lo + B_DK] + pace_row)
                             * ext_ref[g, 8 - j:8 - j + C, lo:lo + B_DK])
            return jax.nn.silu(acc)

        def l2n(t):
            return t * lax.rsqrt(jnp.sum(t * t, axis=-1, keepdims=True) + EPS)

        for h in heads:
            lo = h * B_DK
            prep_ref[par, g, :, lo:lo + B_DK] = l2n(conv_silu(lo)) * (B_DK ** -0.5)
            yield
            prep_ref[par, g, :, D_MODEL + lo:D_MODEL + lo + B_DK] = l2n(conv_silu(D_MODEL + lo))
            yield
            prep_ref[par, g, :, 2 * D_MODEL + lo:2 * D_MODEL + lo + B_DK] = conv_silu(2 * D_MODEL + lo)
            yield
        ext_ref[g, 0:8, :] = ext_ref[g, C:C + 8, :]

    def chain():
        heads = range(GDN_SEQS * B_HEADS)
        half = [u // B_HEADS for u in heads]
        head = [u % B_HEADS for u in heads]
        old = 1 - par
        strict = row > col
        eye = (row == col).astype(F32)
        same_blk = [(row // b) == (col // b) for b in (8, 16, 32, 64, 128)]
        beta_all = [gate_ref[old, g, 0] for g in range(GDN_SEQS)]
        gcum = [gate_ref[old, g, 1] for g in range(GDN_SEQS)]
        gcum_t = [gate_ref[old, g, 2] for g in range(GDN_SEQS)]
        lanes = lambda part, u: slice(part * D_MODEL + head[u] * B_DK, part * D_MODEL + (head[u] + 1) * B_DK)
        q = _Lazy(lambda u: prep_ref[old, half[u], :, lanes(0, u)])
        k = _Lazy(lambda u: prep_ref[old, half[u], :, lanes(1, u)])
        v = _Lazy(lambda u: prep_ref[old, half[u], :, lanes(2, u)])
        bcol = [beta_all[half[u]][:, head[u]:head[u] + 1] for u in heads]
        gcol = [gcum[half[u]][:, 8 + head[u]:9 + head[u]] for u in heads]
        glast = [gcum[half[u]][C - 1:C, 8 + head[u]:9 + head[u]] for u in heads]
        eg = [jnp.exp(gcol[h]) for h in heads]
        kb = _Lazy(lambda u: k[u] * bcol[u])
        yield
        a1 = [_dot_nt(jnp.concatenate([kb[h], q[h]], axis=0), k[h]) for h in heads]
        mark(a1)
        yield
        m, attn = [], []
        for u in heads:
            grow = gcum_t[half[u]][8 + head[u]:9 + head[u], :]
            dec = jnp.where(incl, jnp.exp(jnp.where(incl, gcol[u] - grow, 0.0)), 0.0)
            m.append(jnp.where(strict, a1[u][:C] * dec, 0.0))
            attn.append(a1[u][C:] * dec)
        m0 = [jnp.where(same_blk[0], m[h], 0.0) for h in heads]
        x = [eye - m0[h] for h in heads]
        pw = [_dot(m0[h], m0[h]) for h in heads]
        mark(pw)
        yield
        x = [x[h] + _dot(x[h], pw[h]) for h in heads]
        mark(x)
        yield
        pw = [_dot(pw[h], pw[h]) for h in heads]
        mark(pw)
        yield
        x = [x[h] + _dot(x[h], pw[h]) for h in heads]
        mark(x)
        yield
        for lvl in range(1, len(same_blk)):
            in_lvl = same_blk[lvl] & jnp.logical_not(same_blk[lvl - 1])
            xb = [x[h].astype(BF16) for h in heads]
            t = [_dot(xb[h], jnp.where(in_lvl, m[h], 0.0)) for h in heads]
            mark(t)
            yield
            x = [x[h] - _dot(t[h], xb[h]) for h in heads]
            mark(x)
            yield
        rhs = [jnp.concatenate([v[h] * bcol[h], kb[h] * eg[h]], axis=1) for h in heads]
        sol = [_dot(x[h], rhs[h]) for h in heads]
        mark(sol)
        yield
        keep = ((s + n_c - 1) % n_c != 0).astype(F32)
        s_old = [s_ref[half[u], head[u]] * keep for u in heads]
        b1 = [_dot(jnp.concatenate([sol[h][:, B_DK:], q[h] * eg[h]], axis=0), s_old[h]) for h in heads]
        mark(b1)
        yield
        u = [sol[h][:, :B_DK] - b1[h][:C] for h in heads]
        o = [b1[h][C:] + _dot(attn[h], u[h]) for h in heads]
        kd = [k[h] * jnp.exp(glast[h] - gcol[h]) for h in heads]
        mark(o)
        yield
        s_new = [s_old[h] * jnp.exp(glast[h]) + _dot_tn(kd[h], u[h]) for h in heads]
        mark(s_new)
        yield
        for u in heads:
            g, h = half[u], head[u]
            s_ref[g, h] = s_new[u]
            s_out_refs[g][h] = s_new[u]
            zh = z_refs[g][:, h * B_DK:(h + 1) * B_DK]
            ob_refs[g][:, h * B_DK:(h + 1) * B_DK] = _rms(o[u], ng_ref[...]) * jax.nn.silu(zh)
            yield

    done = object()
    chain_task = chain()
    prep_tasks = [prepare(g) for g in range(GDN_SEQS)]
    for _ in range(3):
        next(chain_task)
    alive = True
    while alive:
        alive = False
        for task in prep_tasks + [chain_task] + prep_tasks + prep_tasks + [chain_task]:
            alive = (next(task, done) is not done) or alive


def _gdn(p, ba, w_conv, alog_row, dt_row, norm_gdn, bsz, seq):
    nc = seq // GDN_CHUNK
    total = (bsz // GDN_SEQS) * nc
    cur = lambda s: jnp.minimum(s, total - 1)
    prev = lambda s: jnp.maximum(s - 1, 0)
    small = lambda shape: pl.BlockSpec(shape, lambda s: (0, 0))

    def half_specs(g):
        blk = lambda colblk: pl.BlockSpec((GDN_CHUNK, D_MODEL), lambda s: (g * total + cur(s), colblk))
        return [blk(2), blk(3), blk(4),
                pl.BlockSpec((GDN_CHUNK, LANES), lambda s: (g * total + cur(s), 0)),
                pl.BlockSpec((GDN_CHUNK, D_MODEL), lambda s: (g * total + prev(s), 5))]

    halves = range(GDN_SEQS)
    outs = pl.pallas_call(
        functools.partial(_gdn_kernel, nc),
        grid=(total + 1,),
        in_specs=[spec for g in halves for spec in half_specs(g)]
        + [small((B_CONV, C_QKV)), small((1, LANES)), small((1, LANES)), small((1, B_DK))],
        out_specs=[pl.BlockSpec((GDN_CHUNK, D_MODEL), lambda s: (prev(s), 0)) for g in halves]
        + [pl.BlockSpec((None, B_HEADS, B_DK, B_DK), lambda s: (prev(s) // nc, 0, 0, 0)) for g in halves],
        out_shape=[jax.ShapeDtypeStruct((total * GDN_CHUNK, D_MODEL), F32) for g in halves]
        + [jax.ShapeDtypeStruct((bsz // GDN_SEQS, B_HEADS, B_DK, B_DK), F32) for g in halves],
        scratch_shapes=[pltpu.VMEM((GDN_SEQS, B_HEADS, B_DK, B_DK), F32),
                        pltpu.VMEM((GDN_SEQS, 8 + GDN_CHUNK, C_QKV), F32),
                        pltpu.VMEM((2, GDN_SEQS, GDN_CHUNK, C_QKV), F32),
                        pltpu.VMEM((2, GDN_SEQS, 3, GDN_CHUNK, LANES), F32)],
        compiler_params=_params(1),
        name="gdn",
    )(*([p, p, p, ba, p] * GDN_SEQS), w_conv, alog_row, dt_row, norm_gdn)
    return outs[:GDN_SEQS], outs[GDN_SEQS:]


GS_REQ = 8


def _gdn_step_kernel(q_ref, k_ref, v_ref, z_ref, ba_ref, sc_ref, sg_ref, wc_ref, alog_ref, dt_ref,
                     ng_ref, ob_ref, snew_ref, cnew_ref):
    beta_all, g_all = _gate_terms(ba_ref[...], alog_ref[...], dt_ref[...])
    eg_all = jnp.exp(g_all)
    pre = jnp.concatenate([q_ref[...], k_ref[...], v_ref[...]], axis=1)
    conv = wc_ref[3:4, :] * pre
    for j in range(B_CONV - 1):
        conv = conv + wc_ref[j:j + 1, :] * sc_ref[:, j, :]
    cnew_ref[:, 0, :] = sc_ref[:, 1, :]
    cnew_ref[:, 1, :] = sc_ref[:, 2, :]
    cnew_ref[:, 2, :] = pre
    qkv = jax.nn.silu(conv)
    zero_rows = jnp.zeros((LANES - GS_REQ, B_DK), F32)

    def l2n(t):
        return t * lax.rsqrt(jnp.sum(t * t, axis=-1, keepdims=True) + EPS)

    for h in range(B_HEADS):
        qh = l2n(qkv[:, h * B_DK:(h + 1) * B_DK]) * (B_DK ** -0.5)
        kh = l2n(qkv[:, D_MODEL + h * B_DK:D_MODEL + (h + 1) * B_DK])
        vh = qkv[:, 2 * D_MODEL + h * B_DK:2 * D_MODEL + (h + 1) * B_DK]
        q_t = jnp.concatenate([qh, zero_rows], axis=0).T
        k_t = jnp.concatenate([kh, zero_rows], axis=0).T
        o_rows = []
        for r in range(GS_REQ):
            kcol = k_t[:, r:r + 1]
            qcol = q_t[:, r:r + 1]
            s_old = sg_ref[r, h]
            beta = beta_all[r:r + 1, h:h + 1]
            eg = eg_all[r:r + 1, 8 + h:9 + h]
            ks = jnp.sum(s_old * kcol, axis=0, keepdims=True)
            u = beta * (vh[r:r + 1, :] - eg * ks)
            s_new = s_old * eg + kcol * u
            snew_ref[r, h] = s_new
            o_rows.append(jnp.sum(s_new * qcol, axis=0, keepdims=True))
        o = jnp.concatenate(o_rows, axis=0)
        zh = z_ref[:, h * B_DK:(h + 1) * B_DK]
        ob_ref[:, h * B_DK:(h + 1) * B_DK] = _rms(o, ng_ref[...]) * jax.nn.silu(zh)


def _gdn_step(p, ba, state_conv, state_gdn, w_conv, alog_row, dt_row, norm_gdn):
    n = p.shape[0]
    blk = lambda colblk: pl.BlockSpec((GS_REQ, D_MODEL), lambda i: (i, colblk))
    small = lambda shape: pl.BlockSpec(shape, lambda i: (0, 0))
    return pl.pallas_call(
        _gdn_step_kernel,
        grid=(n // GS_REQ,),
        in_specs=[blk(2), blk(3), blk(4), blk(5),
                  pl.BlockSpec((GS_REQ, LANES), lambda i: (i, 0)),
                  pl.BlockSpec((GS_REQ, B_CONV - 1, C_QKV), lambda i: (i, 0, 0)),
                  pl.BlockSpec((GS_REQ, B_HEADS, B_DK, B_DK), lambda i: (i, 0, 0, 0)),
                  small((B_CONV, C_QKV)), small((1, LANES)), small((1, LANES)), small((1, B_DK))],
        out_specs=[pl.BlockSpec((GS_REQ, D_MODEL), lambda i: (i, 0)),
                   pl.BlockSpec((GS_REQ, B_HEADS, B_DK, B_DK), lambda i: (i, 0, 0, 0)),
                   pl.BlockSpec((GS_REQ, B_CONV - 1, C_QKV), lambda i: (i, 0, 0))],
        out_shape=[jax.ShapeDtypeStruct((n, D_MODEL), F32),
                   jax.ShapeDtypeStruct((n, B_HEADS, B_DK, B_DK), F32),
                   jax.ShapeDtypeStruct((n, B_CONV - 1, C_QKV), F32)],
        compiler_params=_params(1),
        name="gdn_step",
    )(p, p, p, p, ba, state_conv, state_gdn, w_conv, alog_row, dt_row, norm_gdn)


XA_REQ = 4


def _xattn1_kernel(q_ref, ck_ref, cv_ref, oc_ref):
    scale = M_HEAD_DIM ** -0.5
    base = pl.program_id(0) * XA_REQ

    pairs = (N_MEM // 2, 2 * M_HEADS, M_HEAD_DIM)
    fold = lambda t, op: op(t[0:M_HEADS], t[M_HEADS:2 * M_HEADS])
    twice = lambda t: jnp.concatenate([t, t], axis=0)

    def body(r, carry):
        q = twice(q_ref[base + r])
        s = jnp.sum(ck_ref[r].reshape(pairs) * q[None], axis=-1, keepdims=True) * scale
        mx = twice(fold(jnp.max(s, axis=0), jnp.maximum))
        p = jnp.exp(s - mx[None])
        den = twice(fold(jnp.sum(p, axis=0), jnp.add))
        pr = p / den[None]
        oc_ref[base + r] = fold(jnp.sum(pr * cv_ref[r].reshape(pairs), axis=0), jnp.add)
        return carry

    lax.fori_loop(0, XA_REQ, body, 0)


def _xattn1(p, cache_k, cache_v):
    n = p.shape[0]
    q = p[:, 6 * D_MODEL:7 * D_MODEL].reshape(n, M_HEADS, M_HEAD_DIM)
    cache_spec = pl.BlockSpec((XA_REQ, N_MEM, M_HEADS, M_HEAD_DIM), lambda i: (i, 0, 0, 0))
    whole = pl.BlockSpec((n, M_HEADS, M_HEAD_DIM), lambda i: (0, 0, 0))
    return pl.pallas_call(
        _xattn1_kernel,
        grid=(n // XA_REQ,),
        in_specs=[whole, cache_spec, cache_spec],
        out_specs=whole,
        out_shape=jax.ShapeDtypeStruct((n, M_HEADS, M_HEAD_DIM), F32),
        compiler_params=_params(1),
        name="xattn1",
    )(q, cache_k, cache_v).reshape(n, D_MODEL)


def _route_and_dispatch(x1, nf_ref, wrt_ref, br_ref, xs_ref, route_ref, cnt_ref):
    tt = x1.shape[0]
    h2 = _rms(x1, nf_ref[...])
    h_hi, h_mid, h_lo = _split3(h2)
    w_hi, w_mid, w_lo = _split3(wrt_ref[...])
    yield

    def nt(a, b):
        return lax.dot_general(a, b, (((1,), (1,)), ((), ())), preferred_element_type=F32)

    logits = (nt(w_hi, h_hi) + (nt(w_hi, h_mid) + nt(w_mid, h_hi))
              + (nt(w_hi, h_lo) + nt(w_lo, h_hi) + nt(w_mid, h_mid))) + br_ref[:, 0:1]
    yield
    e_id = _iota((LANES, tt), 0).astype(F32)
    neg = jnp.float32(-jnp.inf)
    big = jnp.float32(1 << 20)
    is_grp = (e_id >= N_EXPERTS) & (e_id < N_EXPERTS + N_GROUPS)
    gl = jnp.where(is_grp, logits, neg)
    gmax = jnp.max(gl, axis=0, keepdims=True)
    g_lo = (jnp.min(jnp.where(gl == gmax, e_id, big), axis=0, keepdims=True) - N_EXPERTS) * EXP_PER_GROUP
    p_g = 1.0 / jnp.sum(jnp.exp(gl - gmax), axis=0, keepdims=True)
    yield
    in_grp = (e_id >= g_lo) & (e_id < g_lo + EXP_PER_GROUP)
    el = jnp.where(in_grp, logits, neg)
    v1 = jnp.max(el, axis=0, keepdims=True)
    i1 = jnp.min(jnp.where(el == v1, e_id, big), axis=0, keepdims=True)
    yield
    el2 = jnp.where(e_id == i1, neg, el)
    v2 = jnp.max(el2, axis=0, keepdims=True)
    i2 = jnp.min(jnp.where(el2 == v2, e_id, big), axis=0, keepdims=True)
    yield
    e2 = jnp.exp(v2 - v1)
    w1 = p_g / (1.0 + e2)
    w2 = p_g * e2 / (1.0 + e2)
    oh1 = (e_id == i1).astype(F32)
    oh2 = (e_id == i2).astype(F32)
    a_t = oh1 + oh2
    cnt = jnp.sum(a_t, axis=1, keepdims=True)
    pages = jnp.floor((cnt + (PAGE_ROWS - 1)) * (1.0 / PAGE_ROWS))
    yield
    lower = (_iota((LANES, LANES), 0) > _iota((LANES, LANES), 1)).astype(BF16)
    offp = jnp.dot(lower, jnp.broadcast_to(pages, (LANES, LANES)).astype(BF16),
                   preferred_element_type=F32)[:, 0:1]
    upper = (_iota((tt, tt), 0) < _iota((tt, tt), 1)).astype(BF16)
    rank = jnp.dot(a_t.astype(BF16), upper, preferred_element_type=F32)
    yield
    pos = offp * PAGE_ROWS + rank
    slot1 = jnp.sum(oh1 * pos, axis=0, keepdims=True)
    slot2 = jnp.sum(oh2 * pos, axis=0, keepdims=True)
    yield
    s_id = _iota((TILE_SLOTS, tt), 0).astype(F32)
    sel = jnp.where((s_id == slot1) | (s_id == slot2), 1.0, 0.0).astype(BF16)
    for r0 in range(0, TILE_SLOTS, TILE_SLOTS // 4):
        r1 = r0 + TILE_SLOTS // 4
        xs_ref[r0:r1, :] = jnp.dot(sel[r0:r1], h_hi, preferred_element_type=F32).astype(BF16)
        yield
    r_id = _iota((LANES, tt), 0)
    rows = (jnp.where(r_id == 0, slot1, 0.0) + jnp.where(r_id == 1, slot2, 0.0)
            + jnp.where(r_id == 2, w1, 0.0) + jnp.where(r_id == 3, w2, 0.0))
    route_ref[...] = rows.T
    cnt_ref[...] = jnp.broadcast_to(cnt, (LANES, LANES)).T[0:8, :]


def _merge(x, oa, ob, oc, ga, gb, gc, wo_ref):
    s = ga * oa + gb * ob + gc * oc
    return x + jnp.dot(s.astype(BF16), wo_ref[...], preferred_element_type=F32)


def _mix_kernel(x_ref, au_ref, av_ref, qm_ref, ga_ref, gb_ref, gc_ref, ob0_ref, ob1_ref, mk_ref, mv_ref,
                ws_ref, bst_ref, wo_ref, nf_ref, wrt_ref, br_ref,
                x1_ref, xs_ref, route_ref, cnt_ref, oa_ref, oc_ref, x1_prev_ref):
    i = pl.program_id(0)
    tt = x_ref.shape[0]

    @pl.when(i == 0)
    def _():
        x1_prev_ref[...] = jnp.zeros_like(x1_prev_ref)

    def mix_tile():
        causal = _iota((A_CHUNK, A_CHUNK), 0) >= _iota((A_CHUNK, A_CHUNK), 1)
        for g in range(A_GROUPS):
            wsg = jnp.where(causal, ws_ref[g], 0.0).astype(BF16)
            bcol = bst_ref[:, g:g + 1]
            lo = g * A_CHUNK
            for cc in range(tt // A_CHUNK):
                r0 = cc * A_CHUNK
                sv = jnp.dot(wsg, av_ref[r0:r0 + A_CHUNK, lo:lo + A_CHUNK].astype(BF16),
                             preferred_element_type=F32) + bcol
                oa_ref[r0:r0 + A_CHUNK, lo:lo + A_CHUNK] = au_ref[r0:r0 + A_CHUNK, lo:lo + A_CHUNK] * sv
            if g % 2 == 1:
                yield
        scale = M_HEAD_DIM ** -0.5
        for h in range(M_HEADS):
            lo = h * M_HEAD_DIM
            s = _dot_nt(qm_ref[:, lo:lo + M_HEAD_DIM], mk_ref[:, lo:lo + M_HEAD_DIM]) * scale
            yield
            p = jnp.exp(s - jnp.max(s, axis=-1, keepdims=True))
            pr = p / jnp.sum(p, axis=-1, keepdims=True)
            oc_ref[:, lo:lo + M_HEAD_DIM] = _dot(pr, mv_ref[:, lo:lo + M_HEAD_DIM])
            yield
        n_tiles = pl.num_programs(0) - 1
        in_first_half = jnp.minimum(i, n_tiles - 1) < n_tiles // GDN_SEQS
        ob = jnp.where(in_first_half, ob0_ref[...], ob1_ref[...])
        x1 = _merge(x_ref[...], oa_ref[...], ob, oc_ref[...],
                    ga_ref[...], gb_ref[...], gc_ref[...], wo_ref)
        x1_ref[...] = x1
        x1_prev_ref[i % 2] = x1

    _round_robin([_route_and_dispatch(x1_prev_ref[(i + 1) % 2], nf_ref, wrt_ref, br_ref, xs_ref,
                                      route_ref, cnt_ref),
                  mix_tile()])


def _mix1_kernel(x_ref, au_ref, av_ref, ga_ref, gb_ref, gc_ref, ob_ref, oc_ref, ws0_ref, bs0_ref,
                 wo_ref, nf_ref, wrt_ref, br_ref, xs_in_ref,
                 x1_ref, xs_ref, route_ref, cnt_ref):
    del xs_in_ref
    oa = au_ref[...] * (ws0_ref[...] * av_ref[...] + bs0_ref[...])
    x1 = _merge(x_ref[...], oa, ob_ref[...], oc_ref[...], ga_ref[...], gb_ref[...], gc_ref[...], wo_ref)
    x1_ref[...] = x1
    for _ in _route_and_dispatch(x1, nf_ref, wrt_ref, br_ref, xs_ref, route_ref, cnt_ref):
        pass


def _route_out_specs(tt, x1_of, slot_of, route_of):
    return [pl.BlockSpec((tt, D_MODEL), lambda i: (x1_of(i), 0)),
            pl.BlockSpec((None, TILE_SLOTS, D_MODEL), lambda i: (slot_of(i), 0, 0)),
            pl.BlockSpec((tt, LANES), lambda i: (route_of(i), 0)),
            pl.BlockSpec((None, 8, LANES), lambda i: (route_of(i), 0, 0))]


def _mix(x2d, p, ob, mk, mv, w_s, bs_t, wo_bf16, norm_ffn, wr_t, br_col, seq):
    rows = x2d.shape[0]
    tt = MIX_TILE
    nt = rows // tt
    per_b = seq // tt
    cur = lambda i: jnp.minimum(i, nt - 1)
    prev = lambda i: jnp.maximum(i - 1, 0)
    pblk = lambda colblk: pl.BlockSpec((tt, D_MODEL), lambda i: (cur(i), colblk))
    const = lambda shape: pl.BlockSpec(shape, lambda i: (0,) * len(shape))
    nh = nt // GDN_SEQS
    ob_blk = lambda g: pl.BlockSpec((tt, D_MODEL), lambda i: (jnp.clip(cur(i) - g * nh, 0, nh - 1), 0))
    return pl.pallas_call(
        _mix_kernel,
        grid=(nt + 1,),
        in_specs=[pblk(0), pblk(0), pblk(1), pblk(6), pblk(7), pblk(8), pblk(9), ob_blk(0), ob_blk(1),
                  pl.BlockSpec((N_MEM, D_MODEL), lambda i: (cur(i) // per_b, 0)),
                  pl.BlockSpec((N_MEM, D_MODEL), lambda i: (cur(i) // per_b, 0)),
                  const((A_GROUPS, A_CHUNK, A_CHUNK)), const((A_CHUNK, LANES)),
                  const((D_MODEL, D_MODEL)), const((1, D_MODEL)),
                  const((LANES, D_MODEL)), const((LANES, LANES))],
        out_specs=_route_out_specs(tt, cur, lambda i: i, prev),
        out_shape=[jax.ShapeDtypeStruct((rows, D_MODEL), F32),
                   jax.ShapeDtypeStruct((nt + 1, TILE_SLOTS, D_MODEL), BF16),
                   jax.ShapeDtypeStruct((rows, LANES), F32),
                   jax.ShapeDtypeStruct((nt, 8, LANES), F32)],
        scratch_shapes=[pltpu.VMEM((tt, D_MODEL), F32), pltpu.VMEM((tt, D_MODEL), F32),
                        pltpu.VMEM((2, tt, D_MODEL), F32)],
        compiler_params=_params(1),
        name="mix",
    )(x2d, p, p, p, p, p, p, ob[0], ob[1], mk, mv, w_s, bs_t, wo_bf16, norm_ffn, wr_t, br_col)


def _mix1(x2d, p, ob, oc, ws0_row, bs0_row, wo_bf16, norm_ffn, wr_t, br_col, xs_all):
    rows = x2d.shape[0]
    pblk = lambda colblk: pl.BlockSpec((rows, D_MODEL), lambda i: (0, colblk))
    const = lambda shape: pl.BlockSpec(shape, lambda i: (0,) * len(shape))
    zero = lambda i: 0
    return pl.pallas_call(
        _mix1_kernel,
        grid=(1,),
        in_specs=[pblk(0), pblk(0), pblk(1), pblk(7), pblk(8), pblk(9), pblk(0), pblk(0),
                  const((1, D_MODEL)), const((1, D_MODEL)),
                  const((D_MODEL, D_MODEL)), const((1, D_MODEL)),
                  const((LANES, D_MODEL)), const((LANES, LANES)),
                  pl.BlockSpec(memory_space=pl.ANY)],
        out_specs=_route_out_specs(rows, zero, zero, zero),
        out_shape=[jax.ShapeDtypeStruct((rows, D_MODEL), F32),
                   jax.ShapeDtypeStruct(xs_all.shape, BF16),
                   jax.ShapeDtypeStruct((rows, LANES), F32),
                   jax.ShapeDtypeStruct((1, 8, LANES), F32)],
        input_output_aliases={14: 1},
        compiler_params=_params(1),
        name="mix1",
    )(x2d, p, p, p, p, p, ob, oc, ws0_row, bs0_row, wo_bf16, norm_ffn, wr_t, br_col, xs_all)


def _experts_kernel(tbl_ref, se_ref, nu_ref, *refs):
    del tbl_ref
    pages = refs[:STEP_PAGES]
    wg_ref, wu_ref, wd_ref, o_ref, wgb_ref, wub_ref, wdb_ref = refs[STEP_PAGES:]
    s = pl.program_id(0)
    prev = se_ref[jnp.maximum(s - 1, 0)]

    @pl.when((s == 0) | (se_ref[s] != prev))
    def _():
        wgb_ref[...] = wg_ref[...].astype(BF16)
        wub_ref[...] = wu_ref[...].astype(BF16)
        wdb_ref[...] = wd_ref[...].astype(BF16)

    @pl.when(s < nu_ref[0])
    def _():
        x = jnp.concatenate([pg[...] for pg in pages], axis=0)
        gate = jnp.dot(x, wgb_ref[...], preferred_element_type=F32)
        up = jnp.dot(x, wub_ref[...], preferred_element_type=F32)
        act = (jax.nn.silu(gate) * up).astype(BF16)
        o_ref[...] = jnp.dot(act, wdb_ref[...], preferred_element_type=F32).astype(BF16)

    @pl.when(s >= nu_ref[0])
    def _():
        o_ref[...] = jnp.zeros_like(o_ref)


def _experts(tbl, step_e, n_used, xs_pages, w_gate, w_up, w_down, n_steps):
    def page_spec(k):
        return pl.BlockSpec((None, PAGE_ROWS, D_MODEL),
                            lambda s, tbl, se, nu: (tbl[s * STEP_PAGES + k], 0, 0))

    wspec = lambda shape: pl.BlockSpec((None,) + shape, lambda s, tbl, se, nu: (se[s], 0, 0))
    grid_spec = pltpu.PrefetchScalarGridSpec(
        num_scalar_prefetch=3,
        grid=(n_steps,),
        in_specs=[page_spec(k) for k in range(STEP_PAGES)]
        + [wspec((D_MODEL, D_EXPERT)), wspec((D_MODEL, D_EXPERT)), wspec((D_EXPERT, D_MODEL))],
        out_specs=pl.BlockSpec((STEP_ROWS, D_MODEL), lambda s, tbl, se, nu: (s, 0)),
        scratch_shapes=[pltpu.VMEM((D_MODEL, D_EXPERT), BF16), pltpu.VMEM((D_MODEL, D_EXPERT), BF16),
                        pltpu.VMEM((D_EXPERT, D_MODEL), BF16)],
    )
    return pl.pallas_call(
        _experts_kernel,
        grid_spec=grid_spec,
        out_shape=jax.ShapeDtypeStruct((n_steps * STEP_ROWS, D_MODEL), BF16),
        compiler_params=_params(1),
        name="experts",
    )(tbl, step_e, n_used, *([xs_pages] * STEP_PAGES), w_gate, w_up, w_down)


def _combine_kernel(inv_ref, *refs):
    del inv_ref
    pages = refs[:TILE_PAGES]
    x1_ref, route_ref, nfin_ref, y_ref = refs[TILE_PAGES:]
    out_loc = jnp.concatenate([pg[...] for pg in pages], axis=0)
    tt = x1_ref.shape[0]
    route = route_ref[...]
    s_id = _iota((tt, TILE_SLOTS), 1).astype(F32)
    sel1 = jnp.where(s_id == route[:, 0:1], 1.0, 0.0).astype(BF16)
    sel2 = jnp.where(s_id == route[:, 1:2], 1.0, 0.0).astype(BF16)
    moe = (route[:, 2:3] * jnp.dot(sel1, out_loc, preferred_element_type=F32)
           + route[:, 3:4] * jnp.dot(sel2, out_loc, preferred_element_type=F32))
    y_ref[...] = _rms(x1_ref[...] + moe, nfin_ref[...])


def _combine(inv, out_pages, x1, route, norm_final, tt):
    rows = x1.shape[0]

    def page_spec(k):
        return pl.BlockSpec((None, PAGE_ROWS, D_MODEL), lambda i, inv: (inv[i * TILE_PAGES + k], 0, 0))

    grid_spec = pltpu.PrefetchScalarGridSpec(
        num_scalar_prefetch=1,
        grid=(rows // tt,),
        in_specs=[page_spec(k) for k in range(TILE_PAGES)]
        + [pl.BlockSpec((tt, D_MODEL), lambda i, inv: (i, 0)),
           pl.BlockSpec((tt, LANES), lambda i, inv: (i, 0)),
           pl.BlockSpec((1, D_MODEL), lambda i, inv: (0, 0))],
        out_specs=pl.BlockSpec((tt, D_MODEL), lambda i, inv: (i, 0)),
    )
    return pl.pallas_call(
        _combine_kernel,
        grid_spec=grid_spec,
        out_shape=jax.ShapeDtypeStruct((rows, D_MODEL), F32),
        compiler_params=_params(1),
        name="combine",
    )(inv, *([out_pages] * TILE_PAGES), x1, route, norm_final)


def _page_tables(cnt, n_steps):
    n_tiles = cnt.shape[0]
    pg = (cnt + (PAGE_ROWS - 1)) // PAGE_ROWS
    lend = jnp.cumsum(pg, axis=1)
    loff = lend - pg
    cum_t = jnp.cumsum(pg, axis=0)
    pref = cum_t - pg
    tot = cum_t[-1]
    totp = ((tot + (STEP_PAGES - 1)) // STEP_PAGES) * STEP_PAGES
    gend = jnp.cumsum(totp)
    gstart = gend - totp
    sel = lambda onehot, vals: jnp.dot(onehot, vals.astype(F32), precision=lax.Precision.HIGHEST)
    gp = jnp.arange(n_steps * STEP_PAGES, dtype=jnp.int32)
    in_e = (gp[:, None] >= gstart[None, :]) & (gp[:, None] < gend[None, :])
    oh_e = in_e.astype(F32)
    r = gp.astype(F32) - sel(oh_e, gstart)
    valid = jnp.any(in_e, axis=1) & (r < sel(oh_e, tot))
    cum_col = sel(oh_e, cum_t.T)
    pref_col = sel(oh_e, pref.T)
    loff_col = sel(oh_e, loff.T)
    in_i = (r[:, None] >= pref_col) & (r[:, None] < cum_col)
    tile_base = (jnp.arange(n_tiles, dtype=jnp.int32) * TILE_PAGES).astype(F32)
    src = jnp.sum(jnp.where(in_i, tile_base[None, :] + loff_col + (r[:, None] - pref_col), 0.0), axis=1)
    tbl = jnp.where(valid, src, 0.0).astype(jnp.int32)
    in_step = in_e[::STEP_PAGES]
    e_ids = jnp.arange(N_EXPERTS, dtype=jnp.int32)
    step_e = jnp.where(jnp.any(in_step, axis=1), jnp.sum(jnp.where(in_step, e_ids[None, :], 0), axis=1),
                       N_EXPERTS - 1).astype(jnp.int32)
    n_used = (gend[-1] // STEP_PAGES).astype(jnp.int32).reshape(1)
    lp = jnp.arange(TILE_PAGES, dtype=jnp.int32)[None, :, None]
    in_l = (lp >= loff[:, None, :]) & (lp < lend[:, None, :])
    gpos = jnp.sum(jnp.where(in_l, gstart[None, None, :] + pref[:, None, :] + lp - loff[:, None, :], 0), axis=2)
    zero_page = (n_steps - 1) * STEP_PAGES
    inv = jnp.where(jnp.any(in_l, axis=2), gpos, zero_page).astype(jnp.int32).reshape(-1)
    return tbl, step_e, n_used, inv


def kernel(x_prompt, x_sample, mem_prompt, cache_mem_k, cache_mem_v, state_gdn, state_conv,
           norm_mix, w_in, b_gate, w_s, b_s, norm_a_v, w_conv, a_log, dt_bias, norm_gdn_out,
           norm_mem, w_mem_kv, w_o, norm_ffn, w_router_group, b_router_group, w_router_expert,
           b_router_expert, w_exp_gate, w_exp_up, w_exp_down, norm_final):
    depth = norm_mix.shape[0]
    assert depth == 1
    bsz, seq, _ = x_prompt.shape
    n_s = x_sample.shape[0]
    assert x_sample.shape[1] == 1 and seq % MIX_TILE == 0 and n_s % GS_REQ == 0 and n_s <= MIX_TILE
    assert bsz % GDN_SEQS == 0
    l = 0
    row = lambda v: v.reshape(1, -1)

    wi = w_in[l]
    o_z = 2 * D_MODEL + C_QKV
    o_beta = o_z + D_MODEL
    o_qm = o_beta + 2 * B_HEADS
    o_gate = o_qm + D_MODEL
    w_main = jnp.concatenate([wi[:, :o_beta], wi[:, o_qm:]], axis=1).astype(BF16)
    w_ba = jnp.pad(wi[:, o_beta:o_qm], ((0, 0), (0, LANES - 2 * B_HEADS))).astype(BF16)
    pad_heads = lambda v: jnp.pad(v.reshape(1, B_HEADS), ((0, 0), (B_HEADS, LANES - 2 * B_HEADS)))
    alog_row = pad_heads(a_log[l])
    dt_row = pad_heads(dt_bias[l])
    w_kv = w_mem_kv[l].astype(BF16)
    wo = w_o[l].astype(BF16)
    wr_t = jnp.pad(jnp.concatenate([w_router_expert[l], w_router_group[l]], axis=1).T,
                   ((0, LANES - N_EXPERTS - N_GROUPS), (0, 0)))
    br_col = jnp.pad(jnp.concatenate([b_router_expert[l], b_router_group[l]]).reshape(-1, 1),
                     ((0, LANES - N_EXPERTS - N_GROUPS), (0, LANES - 1)))
    bs_t = jnp.pad(b_s[l].T, ((0, 0), (0, LANES - A_GROUPS)))
    ws0_row = jnp.repeat(w_s[l][:, 0, 0], A_CHUNK).reshape(1, D_MODEL)
    bs0_row = jnp.repeat(b_s[l][:, 0], A_CHUNK).reshape(1, D_MODEL)

    xp = x_prompt.reshape(bsz * seq, D_MODEL)
    xs_ = x_sample.reshape(n_s, D_MODEL)
    n_tiles_p = (bsz * seq) // MIX_TILE
    n_tiles = n_tiles_p + 1

    mk, mv = _memkv(mem_prompt.reshape(bsz * N_MEM, D_MODEL), row(norm_mem[l]), w_kv)
    p_p, ba_p = _proj(xp, row(norm_mix[l]), w_main, w_ba, row(norm_a_v[l]), row(b_gate[l]),
                      tm=min(PROJ_ROWS, bsz * seq))
    ob_p, s_p = _gdn(p_p, ba_p, w_conv[l], alog_row, dt_row, row(norm_gdn_out[l]), bsz, seq)
    x1_p, xs_all, route_p, cnt_p = _mix(xp, p_p, ob_p, mk, mv, w_s[l], bs_t, wo, row(norm_ffn[l]),
                                        wr_t, br_col, seq)
    p_s, ba_s = _proj(xs_, row(norm_mix[l]), w_main, w_ba, row(norm_a_v[l]), row(b_gate[l]), tm=n_s)
    ob_s, s_s, c_s = _gdn_step(p_s, ba_s, state_conv[l], state_gdn[l], w_conv[l], alog_row, dt_row,
                               row(norm_gdn_out[l]))
    oc_s = _xattn1(p_s, cache_mem_k[l], cache_mem_v[l])
    x1_s, xs_all, route_s, cnt_s = _mix1(xs_, p_s, ob_s, oc_s, ws0_row, bs0_row, wo, row(norm_ffn[l]),
                                         wr_t, br_col, xs_all)
    cnt = jnp.concatenate([cnt_s[:, 0, :N_EXPERTS], cnt_p[:, 0, :N_EXPERTS]], axis=0).astype(jnp.int32)
    max_pages = n_tiles_p * TILE_PAGES + (2 * n_s) // PAGE_ROWS + N_EXPERTS
    n_steps = (max_pages + N_EXPERTS * (STEP_PAGES - 1)) // STEP_PAGES + 1
    tbl, step_e, n_used, inv = _page_tables(cnt, n_steps)
    ne = N_GROUPS * EXP_PER_GROUP
    out_sorted = _experts(tbl, step_e, n_used, xs_all.reshape(n_tiles * TILE_PAGES, PAGE_ROWS, D_MODEL),
                          w_exp_gate[l].reshape(ne, D_MODEL, D_EXPERT),
                          w_exp_up[l].reshape(ne, D_MODEL, D_EXPERT),
                          w_exp_down[l].reshape(ne, D_EXPERT, D_MODEL), n_steps)
    out_pages = out_sorted.reshape(n_steps * STEP_PAGES, PAGE_ROWS, D_MODEL)
    y_s = _combine(inv[:TILE_PAGES], out_pages, x1_s, route_s, row(norm_final), n_s)
    y_p = _combine(inv[TILE_PAGES:], out_pages, x1_p, route_p, row(norm_final), MIX_TILE)

    conv_tail = p_p.reshape(bsz, seq, PROJ_COLS)[:, seq - (B_CONV - 1):, 2 * D_MODEL:2 * D_MODEL + C_QKV]
    return (y_p.reshape(bsz, seq, D_MODEL),
            y_s.reshape(n_s, 1, D_MODEL),
            mk.reshape(1, bsz, N_MEM, M_HEADS, M_HEAD_DIM),
            mv.reshape(1, bsz, N_MEM, M_HEADS, M_HEAD_DIM),
            jnp.concatenate(s_p, axis=0)[None],
            conv_tail[None],
            s_s[None],
            c_s[None],
            p_s[:, D_MODEL:2 * D_MODEL].reshape(1, n_s, 1, D_MODEL))
```

```python
import functools
import math

import jax
import jax.numpy as jnp
from jax import lax
from jax.experimental import pallas as pl
from jax.experimental.pallas import tpu as pltpu

F32 = jnp.float32
BF16 = jnp.bfloat16

D_MODEL = 1024
A_GROUPS = 8
A_CHUNK = 128
B_HEADS = 8
B_DK = 128
C_QKV = 3 * D_MODEL
B_CONV = 4
N_MEM = 256
M_HEADS = 4
M_HEAD_DIM = 256
N_GROUPS = 4
EXP_PER_GROUP = 8
N_EXPERTS = 32
D_EXPERT = 512
EPS = 1e-6

LANES = 128
GDN_CHUNK = 128
GDN_SEQS = 2
MIX_TILE = 256
PAGE_ROWS = 16
TILE_PAGES = 64
TILE_SLOTS = PAGE_ROWS * TILE_PAGES
STEP_PAGES = 32
STEP_ROWS = PAGE_ROWS * STEP_PAGES
PROJ_COLS = 10 * D_MODEL
PROJ_CHUNK = 256
PROJ_ROWS = 2048
PROJ_PIECE_ROWS = 1024
VMEM_LIMIT = 56 * 1024 * 1024


def _params(n_grid):
    return pltpu.CompilerParams(dimension_semantics=("arbitrary",) * n_grid,
                                vmem_limit_bytes=VMEM_LIMIT)


def _rms(x, g):
    return x * lax.rsqrt(jnp.mean(x * x, axis=-1, keepdims=True) + EPS) * g


def _dot(a, b):
    return jnp.dot(a.astype(BF16), b.astype(BF16), preferred_element_type=F32)


def _dot_nt(a, b):
    return lax.dot_general(a.astype(BF16), b.astype(BF16), (((1,), (1,)), ((), ())),
                           preferred_element_type=F32)


def _dot_tn(a, b):
    return lax.dot_general(a.astype(BF16), b.astype(BF16), (((0,), (0,)), ((), ())),
                           preferred_element_type=F32)


def _split3(x):
    hi = x.astype(BF16)
    r = x - hi.astype(F32)
    mid = r.astype(BF16)
    lo = (r - mid.astype(F32)).astype(BF16)
    return hi, mid, lo


def _iota(shape, axis):
    return lax.broadcasted_iota(jnp.int32, shape, axis)


def _memkv_kernel(mem_ref, g_ref, w_ref, k_ref, v_ref):
    h = _rms(mem_ref[...], g_ref[...]).astype(BF16)
    kv = jnp.dot(h, w_ref[...], preferred_element_type=F32)
    k_ref[...] = kv[:, :D_MODEL]
    v_ref[...] = kv[:, D_MODEL:]


def _memkv(mem2d, norm_mem, w_kv_bf16):
    rows = mem2d.shape[0]
    nb = rows // N_MEM
    return pl.pallas_call(
        _memkv_kernel,
        grid=(nb,),
        in_specs=[pl.BlockSpec((N_MEM, D_MODEL), lambda i: (i, 0)),
                  pl.BlockSpec((1, D_MODEL), lambda i: (0, 0)),
                  pl.BlockSpec((D_MODEL, 2 * D_MODEL), lambda i: (0, 0))],
        out_specs=[pl.BlockSpec((N_MEM, D_MODEL), lambda i: (i, 0)),
                   pl.BlockSpec((N_MEM, D_MODEL), lambda i: (i, 0))],
        out_shape=[jax.ShapeDtypeStruct((rows, D_MODEL), F32)] * 2,
        compiler_params=_params(1),
        name="memkv",
    )(mem2d, norm_mem, w_kv_bf16)


def _proj_kernel(x_ref, nm_ref, w_ref, wba_ref, nav_ref, bg_ref, p_ref, ba_ref, hn_ref):
    j = pl.program_id(1)

    @pl.when(j == 0)
    def _():
        hn = _rms(x_ref[...], nm_ref[...]).astype(BF16)
        hn_ref[...] = hn
        ba_ref[...] = jnp.dot(hn, wba_ref[...], preferred_element_type=F32)

    tm = x_ref.shape[0]
    rows = [(r0, min(tm, r0 + PROJ_PIECE_ROWS)) for r0 in range(0, tm, PROJ_PIECE_ROWS)]
    pieces = [(r, c0) for r in rows for c0 in range(0, D_MODEL, PROJ_CHUNK)]

    def chunks(epilogue):
        def matmul(piece):
            (r0, r1), c0 = piece
            return jnp.dot(hn_ref[r0:r1, :], w_ref[:, c0:c0 + PROJ_CHUNK], preferred_element_type=F32)

        def finish(piece, acc):
            (r0, r1), c0 = piece
            p_ref[r0:r1, c0:c0 + PROJ_CHUNK] = epilogue(acc, piece)

        acc = matmul(pieces[0])
        for n in range(1, len(pieces)):
            nxt = matmul(pieces[n])
            finish(pieces[n - 1], acc)
            acc = nxt
        finish(pieces[-1], acc)

    @pl.when(j == 0)
    def _():
        chunks(lambda acc, piece: jax.nn.gelu(acc))

    @pl.when(j == 1)
    def _():
        ssq = {r: [] for r in rows}

        def gelu_ssq(acc, piece):
            a = jax.nn.gelu(acc)
            ssq[piece[0]].append(jnp.sum(a * a, axis=-1, keepdims=True))
            return a

        chunks(gelu_ssq)
        for r0, r1 in rows:
            inv = lax.rsqrt(sum(ssq[(r0, r1)]) * (1.0 / D_MODEL) + EPS)
            p_ref[r0:r1, :] = p_ref[r0:r1, :] * inv * nav_ref[...]

    @pl.when((j >= 2) & (j <= 6))
    def _():
        chunks(lambda acc, piece: acc)

    @pl.when(j >= 7)
    def _():
        chunks(lambda acc, piece: jax.nn.sigmoid(acc + bg_ref[:, piece[1]:piece[1] + PROJ_CHUNK]))


def _proj(x2d, norm_mix, w_bf16, wba_bf16, norm_a_v, b_gate, tm):
    rows = x2d.shape[0]
    ncol = PROJ_COLS // D_MODEL
    return pl.pallas_call(
        _proj_kernel,
        grid=(rows // tm, ncol),
        in_specs=[pl.BlockSpec((tm, D_MODEL), lambda i, j: (i, 0)),
                  pl.BlockSpec((1, D_MODEL), lambda i, j: (0, 0)),
                  pl.BlockSpec((D_MODEL, D_MODEL), lambda i, j: (0, j)),
                  pl.BlockSpec((D_MODEL, LANES), lambda i, j: (0, 0)),
                  pl.BlockSpec((1, D_MODEL), lambda i, j: (0, 0)),
                  pl.BlockSpec((1, D_MODEL), lambda i, j: (0, jnp.maximum(j - 7, 0)))],
        out_specs=[pl.BlockSpec((tm, D_MODEL), lambda i, j: (i, j)),
                   pl.BlockSpec((tm, LANES), lambda i, j: (i, 0))],
        out_shape=[jax.ShapeDtypeStruct((rows, PROJ_COLS), F32),
                   jax.ShapeDtypeStruct((rows, LANES), F32)],
        scratch_shapes=[pltpu.VMEM((tm, D_MODEL), BF16)],
        compiler_params=_params(2),
        name="proj",
    )(x2d, norm_mix, w_bf16, wba_bf16, norm_a_v, b_gate)


def _gate_terms(ba, alog_row, dt_row):
    beta = jax.nn.sigmoid(ba)
    g = -jnp.exp(alog_row) * jax.nn.softplus(ba + dt_row)
    return beta, g


class _Lazy:
    def __init__(self, make):
        self._make = make

    def __getitem__(self, index):
        return self._make(index)


def _round_robin(tasks):
    done = object()
    while tasks:
        tasks = [t for t in tasks if next(t, done) is not done]


def _gdn_kernel(n_c, *refs):
    per = 5
    q_refs, k_refs, v_refs, ba_refs, z_refs = (
        [refs[per * g + j] for g in range(GDN_SEQS)] for j in range(per))
    wc_ref, alog_ref, dt_ref, ng_ref = refs[per * GDN_SEQS:per * GDN_SEQS + 4]
    ob_refs = refs[per * GDN_SEQS + 4:per * GDN_SEQS + 4 + GDN_SEQS]
    s_out_refs = refs[per * GDN_SEQS + 4 + GDN_SEQS:per * GDN_SEQS + 4 + 2 * GDN_SEQS]
    s_ref, ext_ref, prep_ref, gate_ref = refs[per * GDN_SEQS + 4 + 2 * GDN_SEQS:]
    s = pl.program_id(0)
    C = GDN_CHUNK
    par = s % 2
    heads = range(B_HEADS)

    @pl.when(s == 0)
    def _():
        s_ref[...] = jnp.zeros_like(s_ref)
        ext_ref[:, 0:8, :] = jnp.zeros((GDN_SEQS, 8, C_QKV), F32)
        prep_ref[...] = jnp.zeros_like(prep_ref)
        gate_ref[...] = jnp.zeros_like(gate_ref)

    row = _iota((C, C), 0)
    col = _iota((C, C), 1)
    incl = row >= col
    pace = []

    def mark(results):
        pace.append(results[-1][0:1, 0:B_DK] * 0.0)

    def prepare(g):
        q_ref, k_ref, v_ref, ba_ref = q_refs[g], k_refs[g], v_refs[g], ba_refs[g]
        keep = (s % n_c != 0).astype(F32)
        ext_ref[g, 0:8, :] = ext_ref[g, 0:8, :] * keep
        ext_ref[g, 8:8 + C, 0:D_MODEL] = q_ref[...]
        ext_ref[g, 8:8 + C, D_MODEL:2 * D_MODEL] = k_ref[...]
        ext_ref[g, 8:8 + C, 2 * D_MODEL:3 * D_MODEL] = v_ref[...]
        beta_all, g_all = _gate_terms(ba_ref[...], alog_ref[...], dt_ref[...])
        g_hi, g_mid, g_lo = _split3(g_all)
        tri = incl.astype(BF16)
        gcum = (jnp.dot(tri, g_hi, preferred_element_type=F32)
                + jnp.dot(tri, g_mid, preferred_element_type=F32)
                + jnp.dot(tri, g_lo, preferred_element_type=F32))
        gate_ref[par, g, 0] = beta_all
        gate_ref[par, g, 1] = gcum
        gate_ref[par, g, 2] = gcum.T
        yield

        def conv_silu(lo):
            pace_row = pace[-1] if pace else 0.0
            acc = (wc_ref[3:4, lo:lo + B_DK] + pace_row) * ext_ref[g, 8:8 + C, lo:lo + B_DK]
            for j in range(1, B_CONV):
                acc = acc + ((wc_ref[3 - j:4 - j, lo:lo + B_DK] + pace_row)
                             * ext_ref[g, 8 - j:8 - j + C, lo:lo + B_DK])
            return jax.nn.silu(acc)

        def l2n(t):
            return t * lax.rsqrt(jnp.sum(t * t, axis=-1, keepdims=True) + EPS)

        for h in heads:
            lo = h * B_DK
            prep_ref[par, g, :, lo:lo + B_DK] = l2n(conv_silu(lo)) * (B_DK ** -0.5)
            yield
            prep_ref[par, g, :, D_MODEL + lo:D_MODEL + lo + B_DK] = l2n(conv_silu(D_MODEL + lo))
            yield
            prep_ref[par, g, :, 2 * D_MODEL + lo:2 * D_MODEL + lo + B_DK] = conv_silu(2 * D_MODEL + lo)
            yield
        ext_ref[g, 0:8, :] = ext_ref[g, C:C + 8, :]

    def chain():
        heads = range(GDN_SEQS * B_HEADS)
        half = [u // B_HEADS for u in heads]
        head = [u % B_HEADS for u in heads]
        old = 1 - par
        strict = row > col
        eye = (row == col).astype(F32)
        same_blk = [(row // b) == (col // b) for b in (8, 16, 32, 64, 128)]
        beta_all = [gate_ref[old, g, 0] for g in range(GDN_SEQS)]
        gcum = [gate_ref[old, g, 1] for g in range(GDN_SEQS)]
        gcum_t = [gate_ref[old, g, 2] for g in range(GDN_SEQS)]
        lanes = lambda part, u: slice(part * D_MODEL + head[u] * B_DK, part * D_MODEL + (head[u] + 1) * B_DK)
        q = _Lazy(lambda u: prep_ref[old, half[u], :, lanes(0, u)])
        k = _Lazy(lambda u: prep_ref[old, half[u], :, lanes(1, u)])
        v = _Lazy(lambda u: prep_ref[old, half[u], :, lanes(2, u)])
        bcol = [beta_all[half[u]][:, head[u]:head[u] + 1] for u in heads]
        gcol = [gcum[half[u]][:, 8 + head[u]:9 + head[u]] for u in heads]
        glast = [gcum[half[u]][C - 1:C, 8 + head[u]:9 + head[u]] for u in heads]
        eg = [jnp.exp(gcol[h]) for h in heads]
        kb = _Lazy(lambda u: k[u] * bcol[u])
        yield
        a1 = [_dot_nt(jnp.concatenate([kb[h], q[h]], axis=0), k[h]) for h in heads]
        mark(a1)
        yield
        m, attn = [], []
        for u in heads:
            grow = gcum_t[half[u]][8 + head[u]:9 + head[u], :]
            dec = jnp.where(incl, jnp.exp(jnp.where(incl, gcol[u] - grow, 0.0)), 0.0)
            m.append(jnp.where(strict, a1[u][:C] * dec, 0.0))
            attn.append(a1[u][C:] * dec)
        m0 = [jnp.where(same_blk[0], m[h], 0.0) for h in heads]
        x = [eye - m0[h] for h in heads]
        pw = [_dot(m0[h], m0[h]) for h in heads]
        mark(pw)
        yield
        x = [x[h] + _dot(x[h], pw[h]) for h in heads]
        mark(x)
        yield
        pw = [_dot(pw[h], pw[h]) for h in heads]
        mark(pw)
        yield
        x = [x[h] + _dot(x[h], pw[h]) for h in heads]
        mark(x)
        yield
        for lvl in range(1, len(same_blk)):
            in_lvl = same_blk[lvl] & jnp.logical_not(same_blk[lvl - 1])
            xb = [x[h].astype(BF16) for h in heads]
            t = [_dot(xb[h], jnp.where(in_lvl, m[h], 0.0)) for h in heads]
            mark(t)
            yield
            x = [x[h] - _dot(t[h], xb[h]) for h in heads]
            mark(x)
            yield
        rhs = [jnp.concatenate([v[h] * bcol[h], kb[h] * eg[h]], axis=1) for h in heads]
        sol = [_dot(x[h], rhs[h]) for h in heads]
        mark(sol)
        yield
        keep = ((s + n_c - 1) % n_c != 0).astype(F32)
        s_old = [s_ref[half[u], head[u]] * keep for u in heads]
        b1 = [_dot(jnp.concatenate([sol[h][:, B_DK:], q[h] * eg[h]], axis=0), s_old[h]) for h in heads]
        mark(b1)
        yield
        u = [sol[h][:, :B_DK] - b1[h][:C] for h in heads]
        o = [b1[h][C:] + _dot(attn[h], u[h]) for h in heads]
        kd = [k[h] * jnp.exp(glast[h] - gcol[h]) for h in heads]
        mark(o)
        yield
        s_new = [s_old[h] * jnp.exp(glast[h]) + _dot_tn(kd[h], u[h]) for h in heads]
        mark(s_new)
        yield
        for u in heads:
            g, h = half[u], head[u]
            s_ref[g, h] = s_new[u]
            s_out_refs[g][h] = s_new[u]
            zh = z_refs[g][:, h * B_DK:(h + 1) * B_DK]
            ob_refs[g][:, h * B_DK:(h + 1) * B_DK] = _rms(o[u], ng_ref[...]) * jax.nn.silu(zh)
            yield

    done = object()
    chain_task = chain()
    prep_tasks = [prepare(g) for g in range(GDN_SEQS)]
    for _ in range(3):
        next(chain_task)
    alive = True
    while alive:
        alive = False
        for task in prep_tasks + [chain_task] + prep_tasks + prep_tasks + [chain_task]:
            alive = (next(task, done) is not done) or alive


def _gdn(p, ba, w_conv, alog_row, dt_row, norm_gdn, bsz, seq):
    nc = seq // GDN_CHUNK
    total = (bsz // GDN_SEQS) * nc
    cur = lambda s: jnp.minimum(s, total - 1)
    prev = lambda s: jnp.maximum(s - 1, 0)
    small = lambda shape: pl.BlockSpec(shape, lambda s: (0, 0))

    def half_specs(g):
        blk = lambda colblk: pl.BlockSpec((GDN_CHUNK, D_MODEL), lambda s: (g * total + cur(s), colblk))
        return [blk(2), blk(3), blk(4),
                pl.BlockSpec((GDN_CHUNK, LANES), lambda s: (g * total + cur(s), 0)),
                pl.BlockSpec((GDN_CHUNK, D_MODEL), lambda s: (g * total + prev(s), 5))]

    halves = range(GDN_SEQS)
    outs = pl.pallas_call(
        functools.partial(_gdn_kernel, nc),
        grid=(total + 1,),
        in_specs=[spec for g in halves for spec in half_specs(g)]
        + [small((B_CONV, C_QKV)), small((1, LANES)), small((1, LANES)), small((1, B_DK))],
        out_specs=[pl.BlockSpec((GDN_CHUNK, D_MODEL), lambda s: (prev(s), 0)) for g in halves]
        + [pl.BlockSpec((None, B_HEADS, B_DK, B_DK), lambda s: (prev(s) // nc, 0, 0, 0)) for g in halves],
        out_shape=[jax.ShapeDtypeStruct((total * GDN_CHUNK, D_MODEL), F32) for g in halves]
        + [jax.ShapeDtypeStruct((bsz // GDN_SEQS, B_HEADS, B_DK, B_DK), F32) for g in halves],
        scratch_shapes=[pltpu.VMEM((GDN_SEQS, B_HEADS, B_DK, B_DK), F32),
                        pltpu.VMEM((GDN_SEQS, 8 + GDN_CHUNK, C_QKV), F32),
                        pltpu.VMEM((2, GDN_SEQS, GDN_CHUNK, C_QKV), F32),
                        pltpu.VMEM((2, GDN_SEQS, 3, GDN_CHUNK, LANES), F32)],
        compiler_params=_params(1),
        name="gdn",
    )(*([p, p, p, ba, p] * GDN_SEQS), w_conv, alog_row, dt_row, norm_gdn)
    return outs[:GDN_SEQS], outs[GDN_SEQS:]


GS_REQ = 8


def _gdn_step_kernel(q_ref, k_ref, v_ref, z_ref, ba_ref, sc_ref, sg_ref, wc_ref, alog_ref, dt_ref,
                     ng_ref, ob_ref, snew_ref, cnew_ref):
    beta_all, g_all = _gate_terms(ba_ref[...], alog_ref[...], dt_ref[...])
    eg_all = jnp.exp(g_all)
    pre = jnp.concatenate([q_ref[...], k_ref[...], v_ref[...]], axis=1)
    conv = wc_ref[3:4, :] * pre
    for j in range(B_CONV - 1):
        conv = conv + wc_ref[j:j + 1, :] * sc_ref[:, j, :]
    cnew_ref[:, 0, :] = sc_ref[:, 1, :]
    cnew_ref[:, 1, :] = sc_ref[:, 2, :]
    cnew_ref[:, 2, :] = pre
    qkv = jax.nn.silu(conv)
    zero_rows = jnp.zeros((LANES - GS_REQ, B_DK), F32)

    def l2n(t):
        return t * lax.rsqrt(jnp.sum(t * t, axis=-1, keepdims=True) + EPS)

    for h in range(B_HEADS):
        qh = l2n(qkv[:, h * B_DK:(h + 1) * B_DK]) * (B_DK ** -0.5)
        kh = l2n(qkv[:, D_MODEL + h * B_DK:D_MODEL + (h + 1) * B_DK])
        vh = qkv[:, 2 * D_MODEL + h * B_DK:2 * D_MODEL + (h + 1) * B_DK]
        q_t = jnp.concatenate([qh, zero_rows], axis=0).T
        k_t = jnp.concatenate([kh, zero_rows], axis=0).T
        o_rows = []
        for r in range(GS_REQ):
            kcol = k_t[:, r:r + 1]
            qcol = q_t[:, r:r + 1]
            s_old = sg_ref[r, h]
            beta = beta_all[r:r + 1, h:h + 1]
            eg = eg_all[r:r + 1, 8 + h:9 + h]
            ks = jnp.sum(s_old * kcol, axis=0, keepdims=True)
            u = beta * (vh[r:r + 1, :] - eg * ks)
            s_new = s_old * eg + kcol * u
            snew_ref[r, h] = s_new
            o_rows.append(jnp.sum(s_new * qcol, axis=0, keepdims=True))
        o = jnp.concatenate(o_rows, axis=0)
        zh = z_ref[:, h * B_DK:(h + 1) * B_DK]
        ob_ref[:, h * B_DK:(h + 1) * B_DK] = _rms(o, ng_ref[...]) * jax.nn.silu(zh)


def _gdn_step(p, ba, state_conv, state_gdn, w_conv, alog_row, dt_row, norm_gdn):
    n = p.shape[0]
    blk = lambda colblk: pl.BlockSpec((GS_REQ, D_MODEL), lambda i: (i, colblk))
    small = lambda shape: pl.BlockSpec(shape, lambda i: (0, 0))
    return pl.pallas_call(
        _gdn_step_kernel,
        grid=(n // GS_REQ,),
        in_specs=[blk(2), blk(3), blk(4), blk(5),
                  pl.BlockSpec((GS_REQ, LANES), lambda i: (i, 0)),
                  pl.BlockSpec((GS_REQ, B_CONV - 1, C_QKV), lambda i: (i, 0, 0)),
                  pl.BlockSpec((GS_REQ, B_HEADS, B_DK, B_DK), lambda i: (i, 0, 0, 0)),
                  small((B_CONV, C_QKV)), small((1, LANES)), small((1, LANES)), small((1, B_DK))],
        out_specs=[pl.BlockSpec((GS_REQ, D_MODEL), lambda i: (i, 0)),
                   pl.BlockSpec((GS_REQ, B_HEADS, B_DK, B_DK), lambda i: (i, 0, 0, 0)),
                   pl.BlockSpec((GS_REQ, B_CONV - 1, C_QKV), lambda i: (i, 0, 0))],
        out_shape=[jax.ShapeDtypeStruct((n, D_MODEL), F32),
                   jax.ShapeDtypeStruct((n, B_HEADS, B_DK, B_DK), F32),
                   jax.ShapeDtypeStruct((n, B_CONV - 1, C_QKV), F32)],
        compiler_params=_params(1),
        name="gdn_step",
    )(p, p, p, p, ba, state_conv, state_gdn, w_conv, alog_row, dt_row, norm_gdn)


XA_REQ = 2


def _xattn1_requests(q_ref, ck_ref, cv_ref, oc_ref, base):
    scale = M_HEAD_DIM ** -0.5
    pairs = (N_MEM // 2, 2 * M_HEADS, M_HEAD_DIM)
    fold = lambda t, op: op(t[0:M_HEADS], t[M_HEADS:2 * M_HEADS])
    twice = lambda t: jnp.concatenate([t, t], axis=0)
    for r in range(XA_REQ):
        q = twice(q_ref[base + r])
        s = jnp.sum(ck_ref[r].reshape(pairs) * q[None], axis=-1, keepdims=True) * scale
        yield
        mx = twice(fold(jnp.max(s, axis=0), jnp.maximum))
        p = jnp.exp(s - mx[None])
        den = twice(fold(jnp.sum(p, axis=0), jnp.add))
        pr = p / den[None]
        yield
        oc_ref[base + r] = fold(jnp.sum(pr * cv_ref[r].reshape(pairs), axis=0), jnp.add)
        yield


def _route_and_dispatch(x1, nf_ref, wrt_ref, br_ref, xs_ref, route_ref, cnt_ref):
    tt = x1.shape[0]
    h2 = _rms(x1, nf_ref[...])
    h_hi, h_mid, h_lo = _split3(h2)
    w_hi, w_mid, w_lo = _split3(wrt_ref[...])
    yield

    def nt(a, b):
        return lax.dot_general(a, b, (((1,), (1,)), ((), ())), preferred_element_type=F32)

    logits = (nt(w_hi, h_hi) + (nt(w_hi, h_mid) + nt(w_mid, h_hi))
              + (nt(w_hi, h_lo) + nt(w_lo, h_hi) + nt(w_mid, h_mid))) + br_ref[:, 0:1]
    yield
    e_id = _iota((LANES, tt), 0).astype(F32)
    neg = jnp.float32(-jnp.inf)
    big = jnp.float32(1 << 20)
    is_grp = (e_id >= N_EXPERTS) & (e_id < N_EXPERTS + N_GROUPS)
    gl = jnp.where(is_grp, logits, neg)
    gmax = jnp.max(gl, axis=0, keepdims=True)
    g_lo = (jnp.min(jnp.where(gl == gmax, e_id, big), axis=0, keepdims=True) - N_EXPERTS) * EXP_PER_GROUP
    p_g = 1.0 / jnp.sum(jnp.exp(gl - gmax), axis=0, keepdims=True)
    yield
    in_grp = (e_id >= g_lo) & (e_id < g_lo + EXP_PER_GROUP)
    el = jnp.where(in_grp, logits, neg)
    v1 = jnp.max(el, axis=0, keepdims=True)
    i1 = jnp.min(jnp.where(el == v1, e_id, big), axis=0, keepdims=True)
    yield
    el2 = jnp.where(e_id == i1, neg, el)
    v2 = jnp.max(el2, axis=0, keepdims=True)
    i2 = jnp.min(jnp.where(el2 == v2, e_id, big), axis=0, keepdims=True)
    yield
    e2 = jnp.exp(v2 - v1)
    w1 = p_g / (1.0 + e2)
    w2 = p_g * e2 / (1.0 + e2)
    oh1 = (e_id == i1).astype(F32)
    oh2 = (e_id == i2).astype(F32)
    a_t = oh1 + oh2
    cnt = jnp.sum(a_t, axis=1, keepdims=True)
    pages = jnp.floor((cnt + (PAGE_ROWS - 1)) * (1.0 / PAGE_ROWS))
    yield
    lower = (_iota((LANES, LANES), 0) > _iota((LANES, LANES), 1)).astype(BF16)
    offp = jnp.dot(lower, jnp.broadcast_to(pages, (LANES, LANES)).astype(BF16),
                   preferred_element_type=F32)[:, 0:1]
    upper = (_iota((tt, tt), 0) < _iota((tt, tt), 1)).astype(BF16)
    rank = jnp.dot(a_t.astype(BF16), upper, preferred_element_type=F32)
    yield
    pos = offp * PAGE_ROWS + rank
    slot1 = jnp.sum(oh1 * pos, axis=0, keepdims=True)
    slot2 = jnp.sum(oh2 * pos, axis=0, keepdims=True)
    yield
    s_id = _iota((TILE_SLOTS, tt), 0).astype(F32)
    sel = jnp.where((s_id == slot1) | (s_id == slot2), 1.0, 0.0).astype(BF16)
    for r0 in range(0, TILE_SLOTS, TILE_SLOTS // 4):
        r1 = r0 + TILE_SLOTS // 4
        xs_ref[r0:r1, :] = jnp.dot(sel[r0:r1], h_hi, preferred_element_type=F32).astype(BF16)
        yield
    r_id = _iota((LANES, tt), 0)
    rows = (jnp.where(r_id == 0, slot1, 0.0) + jnp.where(r_id == 1, slot2, 0.0)
            + jnp.where(r_id == 2, w1, 0.0) + jnp.where(r_id == 3, w2, 0.0))
    route_ref[...] = rows.T
    cnt_ref[...] = jnp.broadcast_to(cnt, (LANES, LANES)).T[0:8, :]


def _merge(x, oa, ob, oc, ga, gb, gc, wo_ref):
    s = ga * oa + gb * ob + gc * oc
    return x + jnp.dot(s.astype(BF16), wo_ref[...], preferred_element_type=F32)


def _mix_kernel(x_ref, au_ref, av_ref, qm_ref, ga_ref, gb_ref, gc_ref, ob0_ref, ob1_ref, mk_ref, mv_ref,
                ws_ref, bst_ref, wo_ref, nf_ref, wrt_ref, br_ref, xq_ref, ck_ref, cv_ref,
                x1_ref, xs_ref, route_ref, cnt_ref, xoc_ref, oa_ref, oc_ref, x1_prev_ref):
    i = pl.program_id(0)
    tt = x_ref.shape[0]

    @pl.when(i == 0)
    def _():
        x1_prev_ref[...] = jnp.zeros_like(x1_prev_ref)

    def mix_tile():
        causal = _iota((A_CHUNK, A_CHUNK), 0) >= _iota((A_CHUNK, A_CHUNK), 1)
        for g in range(A_GROUPS):
            wsg = jnp.where(causal, ws_ref[g], 0.0).astype(BF16)
            bcol = bst_ref[:, g:g + 1]
            lo = g * A_CHUNK
            for cc in range(tt // A_CHUNK):
                r0 = cc * A_CHUNK
                sv = jnp.dot(wsg, av_ref[r0:r0 + A_CHUNK, lo:lo + A_CHUNK].astype(BF16),
                             preferred_element_type=F32) + bcol
                oa_ref[r0:r0 + A_CHUNK, lo:lo + A_CHUNK] = au_ref[r0:r0 + A_CHUNK, lo:lo + A_CHUNK] * sv
            if g % 2 == 1:
                yield
        scale = M_HEAD_DIM ** -0.5
        for h in range(M_HEADS):
            lo = h * M_HEAD_DIM
            s = _dot_nt(qm_ref[:, lo:lo + M_HEAD_DIM], mk_ref[:, lo:lo + M_HEAD_DIM]) * scale
            yield
            p = jnp.exp(s - jnp.max(s, axis=-1, keepdims=True))
            pr = p / jnp.sum(p, axis=-1, keepdims=True)
            oc_ref[:, lo:lo + M_HEAD_DIM] = _dot(pr, mv_ref[:, lo:lo + M_HEAD_DIM])
            yield
        n_tiles = pl.num_programs(0) - 1
        in_first_half = jnp.minimum(i, n_tiles - 1) < n_tiles // GDN_SEQS
        ob = jnp.where(in_first_half, ob0_ref[...], ob1_ref[...])
        x1 = _merge(x_ref[...], oa_ref[...], ob, oc_ref[...],
                    ga_ref[...], gb_ref[...], gc_ref[...], wo_ref)
        x1_ref[...] = x1
        x1_prev_ref[i % 2] = x1

    n_req_blocks = xq_ref.shape[0] // XA_REQ
    xa_base = jnp.minimum(i, n_req_blocks - 1) * XA_REQ
    _round_robin([_route_and_dispatch(x1_prev_ref[(i + 1) % 2], nf_ref, wrt_ref, br_ref, xs_ref,
                                      route_ref, cnt_ref),
                  mix_tile(),
                  _xattn1_requests(xq_ref, ck_ref, cv_ref, xoc_ref, xa_base)])


def _mix1_kernel(x_ref, au_ref, av_ref, ga_ref, gb_ref, gc_ref, ob_ref, oc_ref, ws0_ref, bs0_ref,
                 wo_ref, nf_ref, wrt_ref, br_ref, xs_in_ref,
                 x1_ref, xs_ref, route_ref, cnt_ref):
    del xs_in_ref
    oa = au_ref[...] * (ws0_ref[...] * av_ref[...] + bs0_ref[...])
    x1 = _merge(x_ref[...], oa, ob_ref[...], oc_ref[...], ga_ref[...], gb_ref[...], gc_ref[...], wo_ref)
    x1_ref[...] = x1
    for _ in _route_and_dispatch(x1, nf_ref, wrt_ref, br_ref, xs_ref, route_ref, cnt_ref):
        pass


def _route_out_specs(tt, x1_of, slot_of, route_of):
    return [pl.BlockSpec((tt, D_MODEL), lambda i: (x1_of(i), 0)),
            pl.BlockSpec((None, TILE_SLOTS, D_MODEL), lambda i: (slot_of(i), 0, 0)),
            pl.BlockSpec((tt, LANES), lambda i: (route_of(i), 0)),
            pl.BlockSpec((None, 8, LANES), lambda i: (route_of(i), 0, 0))]


def _mix(x2d, p, ob, mk, mv, w_s, bs_t, wo_bf16, norm_ffn, wr_t, br_col, seq, p_sample, cache_k, cache_v):
    n_s = p_sample.shape[0]
    xq = p_sample[:, 6 * D_MODEL:7 * D_MODEL].reshape(n_s, M_HEADS, M_HEAD_DIM)
    n_req_blocks = n_s // XA_REQ
    cache_spec = pl.BlockSpec((XA_REQ, N_MEM, M_HEADS, M_HEAD_DIM),
                              lambda i: (jnp.minimum(i, n_req_blocks - 1), 0, 0, 0))
    whole_q = pl.BlockSpec((n_s, M_HEADS, M_HEAD_DIM), lambda i: (0, 0, 0))
    rows = x2d.shape[0]
    assert n_req_blocks <= rows // MIX_TILE + 1 and n_s % XA_REQ == 0
    tt = MIX_TILE
    nt = rows // tt
    per_b = seq // tt
    cur = lambda i: jnp.minimum(i, nt - 1)
    prev = lambda i: jnp.maximum(i - 1, 0)
    pblk = lambda colblk: pl.BlockSpec((tt, D_MODEL), lambda i: (cur(i), colblk))
    const = lambda shape: pl.BlockSpec(shape, lambda i: (0,) * len(shape))
    nh = nt // GDN_SEQS
    ob_blk = lambda g: pl.BlockSpec((tt, D_MODEL), lambda i: (jnp.clip(cur(i) - g * nh, 0, nh - 1), 0))
    x1, xs_all, route, cnt, xoc = pl.pallas_call(
        _mix_kernel,
        grid=(nt + 1,),
        in_specs=[pblk(0), pblk(0), pblk(1), pblk(6), pblk(7), pblk(8), pblk(9), ob_blk(0), ob_blk(1),
                  pl.BlockSpec((N_MEM, D_MODEL), lambda i: (cur(i) // per_b, 0)),
                  pl.BlockSpec((N_MEM, D_MODEL), lambda i: (cur(i) // per_b, 0)),
                  const((A_GROUPS, A_CHUNK, A_CHUNK)), const((A_CHUNK, LANES)),
                  const((D_MODEL, D_MODEL)), const((1, D_MODEL)),
                  const((LANES, D_MODEL)), const((LANES, LANES)),
                  whole_q, cache_spec, cache_spec],
        out_specs=_route_out_specs(tt, cur, lambda i: i, prev) + [whole_q],
        out_shape=[jax.ShapeDtypeStruct((rows, D_MODEL), F32),
                   jax.ShapeDtypeStruct((nt + 1, TILE_SLOTS, D_MODEL), BF16),
                   jax.ShapeDtypeStruct((rows, LANES), F32),
                   jax.ShapeDtypeStruct((nt, 8, LANES), F32),
                   jax.ShapeDtypeStruct((n_s, M_HEADS, M_HEAD_DIM), F32)],
        scratch_shapes=[pltpu.VMEM((tt, D_MODEL), F32), pltpu.VMEM((tt, D_MODEL), F32),
                        pltpu.VMEM((2, tt, D_MODEL), F32)],
        compiler_params=_params(1),
        name="mix",
    )(x2d, p, p, p, p, p, p, ob[0], ob[1], mk, mv, w_s, bs_t, wo_bf16, norm_ffn, wr_t, br_col,
      xq, cache_k, cache_v)
    return x1, xs_all, route, cnt, xoc.reshape(n_s, D_MODEL)


def _mix1(x2d, p, ob, oc, ws0_row, bs0_row, wo_bf16, norm_ffn, wr_t, br_col, xs_all):
    rows = x2d.shape[0]
    pblk = lambda colblk: pl.BlockSpec((rows, D_MODEL), lambda i: (0, colblk))
    const = lambda shape: pl.BlockSpec(shape, lambda i: (0,) * len(shape))
    zero = lambda i: 0
    return pl.pallas_call(
        _mix1_kernel,
        grid=(1,),
        in_specs=[pblk(0), pblk(0), pblk(1), pblk(7), pblk(8), pblk(9), pblk(0), pblk(0),
                  const((1, D_MODEL)), const((1, D_MODEL)),
                  const((D_MODEL, D_MODEL)), const((1, D_MODEL)),
                  const((LANES, D_MODEL)), const((LANES, LANES)),
                  pl.BlockSpec(memory_space=pl.ANY)],
        out_specs=_route_out_specs(rows, zero, zero, zero),
        out_shape=[jax.ShapeDtypeStruct((rows, D_MODEL), F32),
                   jax.ShapeDtypeStruct(xs_all.shape, BF16),
                   jax.ShapeDtypeStruct((rows, LANES), F32),
                   jax.ShapeDtypeStruct((1, 8, LANES), F32)],
        input_output_aliases={14: 1},
        compiler_params=_params(1),
        name="mix1",
    )(x2d, p, p, p, p, p, ob, oc, ws0_row, bs0_row, wo_bf16, norm_ffn, wr_t, br_col, xs_all)


def _experts_kernel(tbl_ref, se_ref, nu_ref, *refs):
    del tbl_ref
    pages = refs[:STEP_PAGES]
    wg_ref, wu_ref, wd_ref, o_ref, wgb_ref, wub_ref, wdb_ref = refs[STEP_PAGES:]
    s = pl.program_id(0)
    prev = se_ref[jnp.maximum(s - 1, 0)]

    @pl.when((s == 0) | (se_ref[s] != prev))
    def _():
        wgb_ref[...] = wg_ref[...].astype(BF16)
        wub_ref[...] = wu_ref[...].astype(BF16)
        wdb_ref[...] = wd_ref[...].astype(BF16)

    @pl.when(s < nu_ref[0])
    def _():
        x = jnp.concatenate([pg[...] for pg in pages], axis=0)
        gate = jnp.dot(x, wgb_ref[...], preferred_element_type=F32)
        up = jnp.dot(x, wub_ref[...], preferred_element_type=F32)
        act = (jax.nn.silu(gate) * up).astype(BF16)
        o_ref[...] = jnp.dot(act, wdb_ref[...], preferred_element_type=F32).astype(BF16)

    @pl.when(s >= nu_ref[0])
    def _():
        o_ref[...] = jnp.zeros_like(o_ref)


def _experts(tbl, step_e, n_used, xs_pages, w_gate, w_up, w_down, n_steps):
    def page_spec(k):
        return pl.BlockSpec((None, PAGE_ROWS, D_MODEL),
                            lambda s, tbl, se, nu: (tbl[s * STEP_PAGES + k], 0, 0))

    wspec = lambda shape: pl.BlockSpec((None,) + shape, lambda s, tbl, se, nu: (se[s], 0, 0))
    grid_spec = pltpu.PrefetchScalarGridSpec(
        num_scalar_prefetch=3,
        grid=(n_steps,),
        in_specs=[page_spec(k) for k in range(STEP_PAGES)]
        + [wspec((D_MODEL, D_EXPERT)), wspec((D_MODEL, D_EXPERT)), wspec((D_EXPERT, D_MODEL))],
        out_specs=pl.BlockSpec((STEP_ROWS, D_MODEL), lambda s, tbl, se, nu: (s, 0)),
        scratch_shapes=[pltpu.VMEM((D_MODEL, D_EXPERT), BF16), pltpu.VMEM((D_MODEL, D_EXPERT), BF16),
                        pltpu.VMEM((D_EXPERT, D_MODEL), BF16)],
    )
    return pl.pallas_call(
        _experts_kernel,
        grid_spec=grid_spec,
        out_shape=jax.ShapeDtypeStruct((n_steps * STEP_ROWS, D_MODEL), BF16),
        compiler_params=_params(1),
        name="experts",
    )(tbl, step_e, n_used, *([xs_pages] * STEP_PAGES), w_gate, w_up, w_down)


def _combine_kernel(inv_ref, *refs):
    del inv_ref
    pages = refs[:TILE_PAGES]
    x1_ref, route_ref, nfin_ref, y_ref = refs[TILE_PAGES:]
    out_loc = jnp.concatenate([pg[...] for pg in pages], axis=0)
    tt = x1_ref.shape[0]
    route = route_ref[...]
    s_id = _iota((tt, TILE_SLOTS), 1).astype(F32)
    sel = (jnp.where(s_id == route[:, 0:1], route[:, 2:3], 0.0)
           + jnp.where(s_id == route[:, 1:2], route[:, 3:4], 0.0)).astype(BF16)
    moe = jnp.dot(sel, out_loc, preferred_element_type=F32)
    y_ref[...] = _rms(x1_ref[...] + moe, nfin_ref[...])


def _combine(inv, out_pages, x1, route, norm_final, tt):
    rows = x1.shape[0]

    def page_spec(k):
        return pl.BlockSpec((None, PAGE_ROWS, D_MODEL), lambda i, inv: (inv[i * TILE_PAGES + k], 0, 0))

    grid_spec = pltpu.PrefetchScalarGridSpec(
        num_scalar_prefetch=1,
        grid=(rows // tt,),
        in_specs=[page_spec(k) for k in range(TILE_PAGES)]
        + [pl.BlockSpec((tt, D_MODEL), lambda i, inv: (i, 0)),
           pl.BlockSpec((tt, LANES), lambda i, inv: (i, 0)),
           pl.BlockSpec((1, D_MODEL), lambda i, inv: (0, 0))],
        out_specs=pl.BlockSpec((tt, D_MODEL), lambda i, inv: (i, 0)),
    )
    return pl.pallas_call(
        _combine_kernel,
        grid_spec=grid_spec,
        out_shape=jax.ShapeDtypeStruct((rows, D_MODEL), F32),
        compiler_params=_params(1),
        name="combine",
    )(inv, *([out_pages] * TILE_PAGES), x1, route, norm_final)


def _page_tables(cnt, n_steps):
    n_tiles = cnt.shape[0]
    pg = (cnt + (PAGE_ROWS - 1)) // PAGE_ROWS
    lend = jnp.cumsum(pg, axis=1)
    loff = lend - pg
    cum_t = jnp.cumsum(pg, axis=0)
    pref = cum_t - pg
    tot = cum_t[-1]
    totp = ((tot + (STEP_PAGES - 1)) // STEP_PAGES) * STEP_PAGES
    gend = jnp.cumsum(totp)
    gstart = gend - totp
    sel = lambda onehot, vals: jnp.dot(onehot, vals.astype(F32), precision=lax.Precision.HIGHEST)
    gp = jnp.arange(n_steps * STEP_PAGES, dtype=jnp.int32)
    in_e = (gp[:, None] >= gstart[None, :]) & (gp[:, None] < gend[None, :])
    oh_e = in_e.astype(F32)
    r = gp.astype(F32) - sel(oh_e, gstart)
    valid = jnp.any(in_e, axis=1) & (r < sel(oh_e, tot))
    cum_col = sel(oh_e, cum_t.T)
    pref_col = sel(oh_e, pref.T)
    loff_col = sel(oh_e, loff.T)
    in_i = (r[:, None] >= pref_col) & (r[:, None] < cum_col)
    tile_base = (jnp.arange(n_tiles, dtype=jnp.int32) * TILE_PAGES).astype(F32)
    src = jnp.sum(jnp.where(in_i, tile_base[None, :] + loff_col + (r[:, None] - pref_col), 0.0), axis=1)
    tbl = jnp.where(valid, src, 0.0).astype(jnp.int32)
    in_step = in_e[::STEP_PAGES]
    e_ids = jnp.arange(N_EXPERTS, dtype=jnp.int32)
    step_e = jnp.where(jnp.any(in_step, axis=1), jnp.sum(jnp.where(in_step, e_ids[None, :], 0), axis=1),
                       N_EXPERTS - 1).astype(jnp.int32)
    n_used = (gend[-1] // STEP_PAGES).astype(jnp.int32).reshape(1)
    lp = jnp.arange(TILE_PAGES, dtype=jnp.int32)[None, :, None]
    in_l = (lp >= loff[:, None, :]) & (lp < lend[:, None, :])
    gpos = jnp.sum(jnp.where(in_l, gstart[None, None, :] + pref[:, None, :] + lp - loff[:, None, :], 0), axis=2)
    zero_page = (n_steps - 1) * STEP_PAGES
    inv = jnp.where(jnp.any(in_l, axis=2), gpos, zero_page).astype(jnp.int32).reshape(-1)
    return tbl, step_e, n_used, inv


def kernel(x_prompt, x_sample, mem_prompt, cache_mem_k, cache_mem_v, state_gdn, state_conv,
           norm_mix, w_in, b_gate, w_s, b_s, norm_a_v, w_conv, a_log, dt_bias, norm_gdn_out,
           norm_mem, w_mem_kv, w_o, norm_ffn, w_router_group, b_router_group, w_router_expert,
           b_router_expert, w_exp_gate, w_exp_up, w_exp_down, norm_final):
    depth = norm_mix.shape[0]
    assert depth == 1
    bsz, seq, _ = x_prompt.shape
    n_s = x_sample.shape[0]
    assert x_sample.shape[1] == 1 and seq % MIX_TILE == 0 and n_s % GS_REQ == 0 and n_s <= MIX_TILE
    assert bsz % GDN_SEQS == 0
    l = 0
    row = lambda v: v.reshape(1, -1)

    wi = w_in[l]
    o_z = 2 * D_MODEL + C_QKV
    o_beta = o_z + D_MODEL
    o_qm = o_beta + 2 * B_HEADS
    o_gate = o_qm + D_MODEL
    w_main = jnp.concatenate([wi[:, :o_beta], wi[:, o_qm:]], axis=1).astype(BF16)
    w_ba = jnp.pad(wi[:, o_beta:o_qm], ((0, 0), (0, LANES - 2 * B_HEADS))).astype(BF16)
    pad_heads = lambda v: jnp.pad(v.reshape(1, B_HEADS), ((0, 0), (B_HEADS, LANES - 2 * B_HEADS)))
    alog_row = pad_heads(a_log[l])
    dt_row = pad_heads(dt_bias[l])
    w_kv = w_mem_kv[l].astype(BF16)
    wo = w_o[l].astype(BF16)
    wr_t = jnp.pad(jnp.concatenate([w_router_expert[l], w_router_group[l]], axis=1).T,
                   ((0, LANES - N_EXPERTS - N_GROUPS), (0, 0)))
    br_col = jnp.pad(jnp.concatenate([b_router_expert[l], b_router_group[l]]).reshape(-1, 1),
                     ((0, LANES - N_EXPERTS - N_GROUPS), (0, LANES - 1)))
    bs_t = jnp.pad(b_s[l].T, ((0, 0), (0, LANES - A_GROUPS)))
    ws0_row = jnp.repeat(w_s[l][:, 0, 0], A_CHUNK).reshape(1, D_MODEL)
    bs0_row = jnp.repeat(b_s[l][:, 0], A_CHUNK).reshape(1, D_MODEL)

    xp = x_prompt.reshape(bsz * seq, D_MODEL)
    xs_ = x_sample.reshape(n_s, D_MODEL)
    n_tiles_p = (bsz * seq) // MIX_TILE
    n_tiles = n_tiles_p + 1

    p_s, ba_s = _proj(xs_, row(norm_mix[l]), w_main, w_ba, row(norm_a_v[l]), row(b_gate[l]), tm=n_s)
    ob_s, s_s, c_s = _gdn_step(p_s, ba_s, state_conv[l], state_gdn[l], w_conv[l], alog_row, dt_row,
                               row(norm_gdn_out[l]))
    mk, mv = _memkv(mem_prompt.reshape(bsz * N_MEM, D_MODEL), row(norm_mem[l]), w_kv)
    p_p, ba_p = _proj(xp, row(norm_mix[l]), w_main, w_ba, row(norm_a_v[l]), row(b_gate[l]),
                      tm=min(PROJ_ROWS, bsz * seq))
    ob_p, s_p = _gdn(p_p, ba_p, w_conv[l], alog_row, dt_row, row(norm_gdn_out[l]), bsz, seq)
    x1_p, xs_all, route_p, cnt_p, oc_s = _mix(xp, p_p, ob_p, mk, mv, w_s[l], bs_t, wo, row(norm_ffn[l]),
                                              wr_t, br_col, seq, p_s, cache_mem_k[l], cache_mem_v[l])
    x1_s, xs_all, route_s, cnt_s = _mix1(xs_, p_s, ob_s, oc_s, ws0_row, bs0_row, wo, row(norm_ffn[l]),
                                         wr_t, br_col, xs_all)
    cnt = jnp.concatenate([cnt_s[:, 0, :N_EXPERTS], cnt_p[:, 0, :N_EXPERTS]], axis=0).astype(jnp.int32)
    max_pages = n_tiles_p * TILE_PAGES + (2 * n_s) // PAGE_ROWS + N_EXPERTS
    n_steps = (max_pages + N_EXPERTS * (STEP_PAGES - 1)) // STEP_PAGES + 1
    tbl, step_e, n_used, inv = _page_tables(cnt, n_steps)
    ne = N_GROUPS * EXP_PER_GROUP
    out_sorted = _experts(tbl, step_e, n_used, xs_all.reshape(n_tiles * TILE_PAGES, PAGE_ROWS, D_MODEL),
                          w_exp_gate[l].reshape(ne, D_MODEL, D_EXPERT),
                          w_exp_up[l].reshape(ne, D_MODEL, D_EXPERT),
                          w_exp_down[l].reshape(ne, D_EXPERT, D_MODEL), n_steps)
    out_pages = out_sorted.reshape(n_steps * STEP_PAGES, PAGE_ROWS, D_MODEL)
    y_s = _combine(inv[:TILE_PAGES], out_pages, x1_s, route_s, row(norm_final), n_s)
    y_p = _combine(inv[TILE_PAGES:], out_pages, x1_p, route_p, row(norm_final), MIX_TILE)

    conv_tail = p_p.reshape(bsz, seq, PROJ_COLS)[:, seq - (B_CONV - 1):, 2 * D_MODEL:2 * D_MODEL + C_QKV]
    return (y_p.reshape(bsz, seq, D_MODEL),
            y_s.reshape(n_s, 1, D_MODEL),
            mk.reshape(1, bsz, N_MEM, M_HEADS, M_HEAD_DIM),
            mv.reshape(1, bsz, N_MEM, M_HEADS, M_HEAD_DIM),
            jnp.concatenate(s_p, axis=0)[None],
            conv_tail[None],
            s_s[None],
            c_s[None],
            p_s[:, D_MODEL:2 * D_MODEL].reshape(1, n_s, 1, D_MODEL))
```

```python
import functools
import math

import jax
import jax.numpy as jnp
from jax import lax
from jax.experimental import pallas as pl
from jax.experimental.pallas import tpu as pltpu

F32 = jnp.float32
BF16 = jnp.bfloat16

D_MODEL = 1024
A_GROUPS = 8
A_CHUNK = 128
B_HEADS = 8
B_DK = 128
C_QKV = 3 * D_MODEL
B_CONV = 4
N_MEM = 256
M_HEADS = 4
M_HEAD_DIM = 256
N_GROUPS = 4
EXP_PER_GROUP = 8
N_EXPERTS = 32
D_EXPERT = 512
EPS = 1e-6

LANES = 128
GDN_CHUNK = 128
GDN_SEQS = 2
MIX_TILE = 256
PAGE_ROWS = 16
TILE_PAGES = 64
TILE_SLOTS = PAGE_ROWS * TILE_PAGES
STEP_PAGES = 32
STEP_ROWS = PAGE_ROWS * STEP_PAGES
PROJ_COLS = 10 * D_MODEL
PROJ_CHUNK = 256
PROJ_ROWS = 2048
PROJ_PIECE_ROWS = 1024
VMEM_LIMIT = 56 * 1024 * 1024


def _params(n_grid):
    return pltpu.CompilerParams(dimension_semantics=("arbitrary",) * n_grid,
                                vmem_limit_bytes=VMEM_LIMIT)


def _rms(x, g):
    return x * lax.rsqrt(jnp.mean(x * x, axis=-1, keepdims=True) + EPS) * g


def _dot(a, b):
    return jnp.dot(a.astype(BF16), b.astype(BF16), preferred_element_type=F32)


def _dot_nt(a, b):
    return lax.dot_general(a.astype(BF16), b.astype(BF16), (((1,), (1,)), ((), ())),
                           preferred_element_type=F32)


def _dot_tn(a, b):
    return lax.dot_general(a.astype(BF16), b.astype(BF16), (((0,), (0,)), ((), ())),
                           preferred_element_type=F32)


def _split3(x):
    hi = x.astype(BF16)
    r = x - hi.astype(F32)
    mid = r.astype(BF16)
    lo = (r - mid.astype(F32)).astype(BF16)
    return hi, mid, lo


def _iota(shape, axis):
    return lax.broadcasted_iota(jnp.int32, shape, axis)


def _memkv_kernel(mem_ref, g_ref, w_ref, k_ref, v_ref):
    h = _rms(mem_ref[...], g_ref[...]).astype(BF16)
    kv = jnp.dot(h, w_ref[...], preferred_element_type=F32)
    k_ref[...] = kv[:, :D_MODEL]
    v_ref[...] = kv[:, D_MODEL:]


def _memkv(mem2d, norm_mem, w_kv_bf16):
    rows = mem2d.shape[0]
    nb = rows // N_MEM
    return pl.pallas_call(
        _memkv_kernel,
        grid=(nb,),
        in_specs=[pl.BlockSpec((N_MEM, D_MODEL), lambda i: (i, 0)),
                  pl.BlockSpec((1, D_MODEL), lambda i: (0, 0)),
                  pl.BlockSpec((D_MODEL, 2 * D_MODEL), lambda i: (0, 0))],
        out_specs=[pl.BlockSpec((N_MEM, D_MODEL), lambda i: (i, 0)),
                   pl.BlockSpec((N_MEM, D_MODEL), lambda i: (i, 0))],
        out_shape=[jax.ShapeDtypeStruct((rows, D_MODEL), F32)] * 2,
        compiler_params=_params(1),
        name="memkv",
    )(mem2d, norm_mem, w_kv_bf16)


def _proj_kernel(x_ref, nm_ref, w_ref, wba_ref, nav_ref, bg_ref, p_ref, ba_ref, hn_ref):
    j = pl.program_id(1)

    @pl.when(j == 0)
    def _():
        hn = _rms(x_ref[...], nm_ref[...]).astype(BF16)
        hn_ref[...] = hn
        ba_ref[...] = jnp.dot(hn, wba_ref[...], preferred_element_type=F32)

    tm = x_ref.shape[0]
    rows = [(r0, min(tm, r0 + PROJ_PIECE_ROWS)) for r0 in range(0, tm, PROJ_PIECE_ROWS)]
    pieces = [(r, c0) for r in rows for c0 in range(0, D_MODEL, PROJ_CHUNK)]

    def chunks(epilogue):
        def matmul(piece):
            (r0, r1), c0 = piece
            return jnp.dot(hn_ref[r0:r1, :], w_ref[:, c0:c0 + PROJ_CHUNK], preferred_element_type=F32)

        def finish(piece, acc):
            (r0, r1), c0 = piece
            p_ref[r0:r1, c0:c0 + PROJ_CHUNK] = epilogue(acc, piece)

        acc = matmul(pieces[0])
        for n in range(1, len(pieces)):
            nxt = matmul(pieces[n])
            finish(pieces[n - 1], acc)
            acc = nxt
        finish(pieces[-1], acc)

    @pl.when(j == 0)
    def _():
        chunks(lambda acc, piece: jax.nn.gelu(acc))

    @pl.when(j == 1)
    def _():
        ssq = {r: [] for r in rows}

        def gelu_ssq(acc, piece):
            a = jax.nn.gelu(acc)
            ssq[piece[0]].append(jnp.sum(a * a, axis=-1, keepdims=True))
            return a

        chunks(gelu_ssq)
        for r0, r1 in rows:
            inv = lax.rsqrt(sum(ssq[(r0, r1)]) * (1.0 / D_MODEL) + EPS)
            p_ref[r0:r1, :] = p_ref[r0:r1, :] * inv * nav_ref[...]

    @pl.when((j >= 2) & (j <= 6))
    def _():
        chunks(lambda acc, piece: acc)

    @pl.when(j >= 7)
    def _():
        chunks(lambda acc, piece: jax.nn.sigmoid(acc + bg_ref[:, piece[1]:piece[1] + PROJ_CHUNK]))


def _proj(x2d, norm_mix, w_bf16, wba_bf16, norm_a_v, b_gate, tm):
    rows = x2d.shape[0]
    ncol = PROJ_COLS // D_MODEL
    return pl.pallas_call(
        _proj_kernel,
        grid=(rows // tm, ncol),
        in_specs=[pl.BlockSpec((tm, D_MODEL), lambda i, j: (i, 0)),
                  pl.BlockSpec((1, D_MODEL), lambda i, j: (0, 0)),
                  pl.BlockSpec((D_MODEL, D_MODEL), lambda i, j: (0, j)),
                  pl.BlockSpec((D_MODEL, LANES), lambda i, j: (0, 0)),
                  pl.BlockSpec((1, D_MODEL), lambda i, j: (0, 0)),
                  pl.BlockSpec((1, D_MODEL), lambda i, j: (0, jnp.maximum(j - 7, 0)))],
        out_specs=[pl.BlockSpec((tm, D_MODEL), lambda i, j: (i, j)),
                   pl.BlockSpec((tm, LANES), lambda i, j: (i, 0))],
        out_shape=[jax.ShapeDtypeStruct((rows, PROJ_COLS), F32),
                   jax.ShapeDtypeStruct((rows, LANES), F32)],
        scratch_shapes=[pltpu.VMEM((tm, D_MODEL), BF16)],
        compiler_params=_params(2),
        name="proj",
    )(x2d, norm_mix, w_bf16, wba_bf16, norm_a_v, b_gate)


def _gate_terms(ba, alog_row, dt_row):
    beta = jax.nn.sigmoid(ba)
    g = -jnp.exp(alog_row) * jax.nn.softplus(ba + dt_row)
    return beta, g


class _Lazy:
    def __init__(self, make):
        self._make = make

    def __getitem__(self, index):
        return self._make(index)


def _round_robin(tasks):
    done = object()
    while tasks:
        tasks = [t for t in tasks if next(t, done) is not done]


def _gdn_kernel(n_c, *refs):
    per = 5
    q_refs, k_refs, v_refs, ba_refs, z_refs = (
        [refs[per * g + j] for g in range(GDN_SEQS)] for j in range(per))
    wc_ref, alog_ref, dt_ref, ng_ref = refs[per * GDN_SEQS:per * GDN_SEQS + 4]
    ob_refs = refs[per * GDN_SEQS + 4:per * GDN_SEQS + 4 + GDN_SEQS]
    s_out_refs = refs[per * GDN_SEQS + 4 + GDN_SEQS:per * GDN_SEQS + 4 + 2 * GDN_SEQS]
    s_ref, ext_ref, prep_ref, gate_ref = refs[per * GDN_SEQS + 4 + 2 * GDN_SEQS:]
    s = pl.program_id(0)
    C = GDN_CHUNK
    par = s % 2
    heads = range(B_HEADS)

    @pl.when(s == 0)
    def _():
        s_ref[...] = jnp.zeros_like(s_ref)
        ext_ref[:, 0:8, :] = jnp.zeros((GDN_SEQS, 8, C_QKV), F32)
        prep_ref[...] = jnp.zeros_like(prep_ref)
        gate_ref[...] = jnp.zeros_like(gate_ref)

    row = _iota((C, C), 0)
    col = _iota((C, C), 1)
    incl = row >= col
    pace = []

    def mark(results):
        pace.append(results[-1][0:1, 0:B_DK] * 0.0)

    def prepare(g):
        q_ref, k_ref, v_ref, ba_ref = q_refs[g], k_refs[g], v_refs[g], ba_refs[g]
        keep = (s % n_c != 0).astype(F32)
        ext_ref[g, 0:8, :] = ext_ref[g, 0:8, :] * keep
        ext_ref[g, 8:8 + C, 0:D_MODEL] = q_ref[...]
        ext_ref[g, 8:8 + C, D_MODEL:2 * D_MODEL] = k_ref[...]
        ext_ref[g, 8:8 + C, 2 * D_MODEL:3 * D_MODEL] = v_ref[...]
        beta_all, g_all = _gate_terms(ba_ref[...], alog_ref[...], dt_ref[...])
        g_hi, g_mid, g_lo = _split3(g_all)
        tri = incl.astype(BF16)
        gcum = (jnp.dot(tri, g_hi, preferred_element_type=F32)
                + jnp.dot(tri, g_mid, preferred_element_type=F32)
                + jnp.dot(tri, g_lo, preferred_element_type=F32))
        gate_ref[par, g, 0] = beta_all
        gate_ref[par, g, 1] = gcum
        gate_ref[par, g, 2] = gcum.T
        yield

        def conv_silu(lo):
            pace_row = pace[-1] if pace else 0.0
            acc = (wc_ref[3:4, lo:lo + B_DK] + pace_row) * ext_ref[g, 8:8 + C, lo:lo + B_DK]
            for j in range(1, B_CONV):
                acc = acc + ((wc_ref[3 - j:4 - j, lo:lo + B_DK] + pace_row)
                             * ext_ref[g, 8 - j:8 - j + C, lo:lo + B_DK])
            return jax.nn.silu(acc)

        def l2n(t):
            return t * lax.rsqrt(jnp.sum(t * t, axis=-1, keepdims=True) + EPS)

        for h in heads:
            lo = h * B_DK
            prep_ref[par, g, :, lo:lo + B_DK] = l2n(conv_silu(lo)) * (B_DK ** -0.5)
            yield
            prep_ref[par, g, :, D_MODEL + lo:D_MODEL + lo + B_DK] = l2n(conv_silu(D_MODEL + lo))
            yield
            prep_ref[par, g, :, 2 * D_MODEL + lo:2 * D_MODEL + lo + B_DK] = conv_silu(2 * D_MODEL + lo)
            yield
        ext_ref[g, 0:8, :] = ext_ref[g, C:C + 8, :]

    def chain():
        heads = range(GDN_SEQS * B_HEADS)
        half = [u // B_HEADS for u in heads]
        head = [u % B_HEADS for u in heads]
        old = 1 - par
        strict = row > col
        eye = (row == col).astype(F32)
        same_blk = [(row // b) == (col // b) for b in (8, 16, 32, 64, 128)]
        beta_all = [gate_ref[old, g, 0] for g in range(GDN_SEQS)]
        gcum = [gate_ref[old, g, 1] for g in range(GDN_SEQS)]
        gcum_t = [gate_ref[old, g, 2] for g in range(GDN_SEQS)]
        lanes = lambda part, u: slice(part * D_MODEL + head[u] * B_DK, part * D_MODEL + (head[u] + 1) * B_DK)
        q = _Lazy(lambda u: prep_ref[old, half[u], :, lanes(0, u)])
        k = _Lazy(lambda u: prep_ref[old, half[u], :, lanes(1, u)])
        v = _Lazy(lambda u: prep_ref[old, half[u], :, lanes(2, u)])
        bcol = [beta_all[half[u]][:, head[u]:head[u] + 1] for u in heads]
        gcol = [gcum[half[u]][:, 8 + head[u]:9 + head[u]] for u in heads]
        glast = [gcum[half[u]][C - 1:C, 8 + head[u]:9 + head[u]] for u in heads]
        eg = [jnp.exp(gcol[h]) for h in heads]
        kb = _Lazy(lambda u: k[u] * bcol[u])
        yield
        a1 = [_dot_nt(jnp.concatenate([kb[h], q[h]], axis=0), k[h]) for h in heads]
        mark(a1)
        yield
        m, attn = [], []
        for u in heads:
            grow = gcum_t[half[u]][8 + head[u]:9 + head[u], :]
            dec = jnp.where(incl, jnp.exp(jnp.where(incl, gcol[u] - grow, 0.0)), 0.0)
            m.append(jnp.where(strict, a1[u][:C] * dec, 0.0))
            attn.append(a1[u][C:] * dec)
        m0 = [jnp.where(same_blk[0], m[h], 0.0) for h in heads]
        x = [eye - m0[h] for h in heads]
        pw = [_dot(m0[h], m0[h]) for h in heads]
        mark(pw)
        yield
        x = [x[h] + _dot(x[h], pw[h]) for h in heads]
        mark(x)
        yield
        pw = [_dot(pw[h], pw[h]) for h in heads]
        mark(pw)
        yield
        x = [x[h] + _dot(x[h], pw[h]) for h in heads]
        mark(x)
        yield
        for lvl in range(1, len(same_blk)):
            in_lvl = same_blk[lvl] & jnp.logical_not(same_blk[lvl - 1])
            xb = [x[h].astype(BF16) for h in heads]
            t = [_dot(xb[h], jnp.where(in_lvl, m[h], 0.0)) for h in heads]
            mark(t)
            yield
            x = [x[h] - _dot(t[h], xb[h]) for h in heads]
            mark(x)
            yield
        rhs = [jnp.concatenate([v[h] * bcol[h], kb[h] * eg[h]], axis=1) for h in heads]
        sol = [_dot(x[h], rhs[h]) for h in heads]
        mark(sol)
        yield
        keep = ((s + n_c - 1) % n_c != 0).astype(F32)
        s_old = [s_ref[half[u], head[u]] * keep for u in heads]
        b1 = [_dot(jnp.concatenate([sol[h][:, B_DK:], q[h] * eg[h]], axis=0), s_old[h]) for h in heads]
        mark(b1)
        yield
        u = [sol[h][:, :B_DK] - b1[h][:C] for h in heads]
        o = [b1[h][C:] + _dot(attn[h], u[h]) for h in heads]
        kd = [k[h] * jnp.exp(glast[h] - gcol[h]) for h in heads]
        mark(o)
        yield
        s_new = [s_old[h] * jnp.exp(glast[h]) + _dot_tn(kd[h], u[h]) for h in heads]
        mark(s_new)
        yield
        for u in heads:
            g, h = half[u], head[u]
            s_ref[g, h] = s_new[u]
            s_out_refs[g][h] = s_new[u]
            zh = z_refs[g][:, h * B_DK:(h + 1) * B_DK]
            ob_refs[g][:, h * B_DK:(h + 1) * B_DK] = _rms(o[u], ng_ref[...]) * jax.nn.silu(zh)
            yield

    done = object()
    chain_task = chain()
    prep_tasks = [prepare(g) for g in range(GDN_SEQS)]
    for _ in range(3):
        next(chain_task)
    alive = True
    while alive:
        alive = False
        for task in prep_tasks + [chain_task] + prep_tasks + prep_tasks + [chain_task]:
            alive = (next(task, done) is not done) or alive


def _gdn(p, ba, w_conv, alog_row, dt_row, norm_gdn, bsz, seq):
    nc = seq // GDN_CHUNK
    total = (bsz // GDN_SEQS) * nc
    cur = lambda s: jnp.minimum(s, total - 1)
    prev = lambda s: jnp.maximum(s - 1, 0)
    small = lambda shape: pl.BlockSpec(shape, lambda s: (0, 0))

    def half_specs(g):
        blk = lambda colblk: pl.BlockSpec((GDN_CHUNK, D_MODEL), lambda s: (g * total + cur(s), colblk))
        return [blk(2), blk(3), blk(4),
                pl.BlockSpec((GDN_CHUNK, LANES), lambda s: (g * total + cur(s), 0)),
                pl.BlockSpec((GDN_CHUNK, D_MODEL), lambda s: (g * total + prev(s), 5))]

    halves = range(GDN_SEQS)
    outs = pl.pallas_call(
        functools.partial(_gdn_kernel, nc),
        grid=(total + 1,),
        in_specs=[spec for g in halves for spec in half_specs(g)]
        + [small((B_CONV, C_QKV)), small((1, LANES)), small((1, LANES)), small((1, B_DK))],
        out_specs=[pl.BlockSpec((GDN_CHUNK, D_MODEL), lambda s: (prev(s), 0)) for g in halves]
        + [pl.BlockSpec((None, B_HEADS, B_DK, B_DK), lambda s: (prev(s) // nc, 0, 0, 0)) for g in halves],
        out_shape=[jax.ShapeDtypeStruct((total * GDN_CHUNK, D_MODEL), F32) for g in halves]
        + [jax.ShapeDtypeStruct((bsz // GDN_SEQS, B_HEADS, B_DK, B_DK), F32) for g in halves],
        scratch_shapes=[pltpu.VMEM((GDN_SEQS, B_HEADS, B_DK, B_DK), F32),
                        pltpu.VMEM((GDN_SEQS, 8 + GDN_CHUNK, C_QKV), F32),
                        pltpu.VMEM((2, GDN_SEQS, GDN_CHUNK, C_QKV), F32),
                        pltpu.VMEM((2, GDN_SEQS, 3, GDN_CHUNK, LANES), F32)],
        compiler_params=_params(1),
        name="gdn",
    )(*([p, p, p, ba, p] * GDN_SEQS), w_conv, alog_row, dt_row, norm_gdn)
    return outs[:GDN_SEQS], outs[GDN_SEQS:]


def _gdn_step_request(q_ref, k_ref, v_ref, z_ref, ba_ref, sc_ref, sg_ref, wc_ref, alog_ref, dt_ref,
                      ng_ref, ob_ref, snew_ref, cnew_ref, base, r):
    def l2n(t):
        return t * lax.rsqrt(jnp.sum(t * t, axis=-1, keepdims=True) + EPS)

    def column(t):
        return jnp.broadcast_to(t, (B_DK, B_DK)).T

    row = pl.ds(base + r, 1)
    beta_all, g_all = _gate_terms(ba_ref[row, :], alog_ref[...], dt_ref[...])
    eg_all = jnp.exp(g_all)
    pre = jnp.concatenate([q_ref[row, :], k_ref[row, :], v_ref[row, :]], axis=1)
    conv = wc_ref[3:4, :] * pre
    for j in range(B_CONV - 1):
        conv = conv + wc_ref[j:j + 1, :] * sc_ref[r, j:j + 1, :]
    cnew_ref[r, 0:1, :] = sc_ref[r, 1:2, :]
    cnew_ref[r, 1:2, :] = sc_ref[r, 2:3, :]
    cnew_ref[r, 2:3, :] = pre
    qkv = jax.nn.silu(conv)
    gate = jax.nn.silu(z_ref[row, :])
    out = []
    yield
    for h in range(B_HEADS):
        lo = h * B_DK
        qcol = column(l2n(qkv[:, lo:lo + B_DK]) * (B_DK ** -0.5))
        kcol = column(l2n(qkv[:, D_MODEL + lo:D_MODEL + lo + B_DK]))
        vh = qkv[:, 2 * D_MODEL + lo:2 * D_MODEL + lo + B_DK]
        s_old = sg_ref[r, h]
        beta = beta_all[:, h:h + 1]
        eg = eg_all[:, 8 + h:9 + h]
        ks = jnp.sum(s_old * kcol, axis=0, keepdims=True)
        u = beta * (vh - eg * ks)
        s_new = s_old * eg + kcol * u
        snew_ref[r, h] = s_new
        yield
        o = jnp.sum(s_new * qcol, axis=0, keepdims=True)
        out.append(_rms(o, ng_ref[...]) * gate[:, lo:lo + B_DK])
        yield
    ob_ref[row, :] = jnp.concatenate(out, axis=1)


XA_REQ = 2


def _xattn1_requests(q_ref, ck_ref, cv_ref, oc_ref, base):
    scale = M_HEAD_DIM ** -0.5
    pairs = (N_MEM // 2, 2 * M_HEADS, M_HEAD_DIM)
    fold = lambda t, op: op(t[0:M_HEADS], t[M_HEADS:2 * M_HEADS])
    twice = lambda t: jnp.concatenate([t, t], axis=0)
    for r in range(XA_REQ):
        q = twice(q_ref[base + r])
        s = jnp.sum(ck_ref[r].reshape(pairs) * q[None], axis=-1, keepdims=True) * scale
        yield
        mx = twice(fold(jnp.max(s, axis=0), jnp.maximum))
        p = jnp.exp(s - mx[None])
        den = twice(fold(jnp.sum(p, axis=0), jnp.add))
        pr = p / den[None]
        yield
        oc_ref[base + r] = fold(jnp.sum(pr * cv_ref[r].reshape(pairs), axis=0), jnp.add)
        yield


def _route_and_dispatch(x1, nf_ref, wrt_ref, br_ref, xs_ref, route_ref, cnt_ref):
    tt = x1.shape[0]
    h2 = _rms(x1, nf_ref[...])
    h_hi, h_mid, h_lo = _split3(h2)
    w_hi, w_mid, w_lo = _split3(wrt_ref[...])
    yield

    def nt(a, b):
        return lax.dot_general(a, b, (((1,), (1,)), ((), ())), preferred_element_type=F32)

    logits = (nt(w_hi, h_hi) + (nt(w_hi, h_mid) + nt(w_mid, h_hi))
              + (nt(w_hi, h_lo) + nt(w_lo, h_hi) + nt(w_mid, h_mid))) + br_ref[:, 0:1]
    yield
    e_id = _iota((LANES, tt), 0).astype(F32)
    neg = jnp.float32(-jnp.inf)
    big = jnp.float32(1 << 20)
    is_grp = (e_id >= N_EXPERTS) & (e_id < N_EXPERTS + N_GROUPS)
    gl = jnp.where(is_grp, logits, neg)
    gmax = jnp.max(gl, axis=0, keepdims=True)
    g_lo = (jnp.min(jnp.where(gl == gmax, e_id, big), axis=0, keepdims=True) - N_EXPERTS) * EXP_PER_GROUP
    p_g = 1.0 / jnp.sum(jnp.exp(gl - gmax), axis=0, keepdims=True)
    yield
    in_grp = (e_id >= g_lo) & (e_id < g_lo + EXP_PER_GROUP)
    el = jnp.where(in_grp, logits, neg)
    v1 = jnp.max(el, axis=0, keepdims=True)
    i1 = jnp.min(jnp.where(el == v1, e_id, big), axis=0, keepdims=True)
    yield
    el2 = jnp.where(e_id == i1, neg, el)
    v2 = jnp.max(el2, axis=0, keepdims=True)
    i2 = jnp.min(jnp.where(el2 == v2, e_id, big), axis=0, keepdims=True)
    yield
    e2 = jnp.exp(v2 - v1)
    w1 = p_g / (1.0 + e2)
    w2 = p_g * e2 / (1.0 + e2)
    oh1 = (e_id == i1).astype(F32)
    oh2 = (e_id == i2).astype(F32)
    a_t = oh1 + oh2
    cnt = jnp.sum(a_t, axis=1, keepdims=True)
    pages = jnp.floor((cnt + (PAGE_ROWS - 1)) * (1.0 / PAGE_ROWS))
    yield
    lower = (_iota((LANES, LANES), 0) > _iota((LANES, LANES), 1)).astype(BF16)
    offp = jnp.dot(lower, jnp.broadcast_to(pages, (LANES, LANES)).astype(BF16),
                   preferred_element_type=F32)[:, 0:1]
    upper = (_iota((tt, tt), 0) < _iota((tt, tt), 1)).astype(BF16)
    rank = jnp.dot(a_t.astype(BF16), upper, preferred_element_type=F32)
    yield
    pos = offp * PAGE_ROWS + rank
    slot1 = jnp.sum(oh1 * pos, axis=0, keepdims=True)
    slot2 = jnp.sum(oh2 * pos, axis=0, keepdims=True)
    yield
    s_id = _iota((TILE_SLOTS, tt), 0).astype(F32)
    sel = jnp.where((s_id == slot1) | (s_id == slot2), 1.0, 0.0).astype(BF16)
    for r0 in range(0, TILE_SLOTS, TILE_SLOTS // 4):
        r1 = r0 + TILE_SLOTS // 4
        xs_ref[r0:r1, :] = jnp.dot(sel[r0:r1], h_hi, preferred_element_type=F32).astype(BF16)
        yield
    r_id = _iota((LANES, tt), 0)
    rows = (jnp.where(r_id == 0, slot1, 0.0) + jnp.where(r_id == 1, slot2, 0.0)
            + jnp.where(r_id == 2, w1, 0.0) + jnp.where(r_id == 3, w2, 0.0))
    route_ref[...] = rows.T
    cnt_ref[...] = jnp.broadcast_to(cnt, (LANES, LANES)).T[0:8, :]


def _merge(x, oa, ob, oc, ga, gb, gc, wo_ref):
    s = ga * oa + gb * ob + gc * oc
    return x + jnp.dot(s.astype(BF16), wo_ref[...], preferred_element_type=F32)


def _mix_kernel(x_ref, au_ref, av_ref, qm_ref, ga_ref, gb_ref, gc_ref, ob0_ref, ob1_ref, mk_ref, mv_ref,
                ws_ref, bst_ref, wo_ref, nf_ref, wrt_ref, br_ref, xq_ref, ck_ref, cv_ref,
                sq_ref, sk_ref, sv_ref, sz_ref, sba_ref, sc_ref, sg_ref, wc_ref, alog_ref, dt_ref, ng_ref,
                x1_ref, xs_ref, route_ref, cnt_ref, xoc_ref, sob_ref, snew_ref, cnew_ref,
                oa_ref, oc_ref, x1_prev_ref):
    i = pl.program_id(0)
    tt = x_ref.shape[0]

    @pl.when(i == 0)
    def _():
        x1_prev_ref[...] = jnp.zeros_like(x1_prev_ref)

    def mix_tile():
        causal = _iota((A_CHUNK, A_CHUNK), 0) >= _iota((A_CHUNK, A_CHUNK), 1)
        for g in range(A_GROUPS):
            wsg = jnp.where(causal, ws_ref[g], 0.0).astype(BF16)
            bcol = bst_ref[:, g:g + 1]
            lo = g * A_CHUNK
            for cc in range(tt // A_CHUNK):
                r0 = cc * A_CHUNK
                sv = jnp.dot(wsg, av_ref[r0:r0 + A_CHUNK, lo:lo + A_CHUNK].astype(BF16),
                             preferred_element_type=F32) + bcol
                oa_ref[r0:r0 + A_CHUNK, lo:lo + A_CHUNK] = au_ref[r0:r0 + A_CHUNK, lo:lo + A_CHUNK] * sv
            if g % 2 == 1:
                yield
        scale = M_HEAD_DIM ** -0.5
        for h in range(M_HEADS):
            lo = h * M_HEAD_DIM
            s = _dot_nt(qm_ref[:, lo:lo + M_HEAD_DIM], mk_ref[:, lo:lo + M_HEAD_DIM]) * scale
            yield
            p = jnp.exp(s - jnp.max(s, axis=-1, keepdims=True))
            pr = p / jnp.sum(p, axis=-1, keepdims=True)
            oc_ref[:, lo:lo + M_HEAD_DIM] = _dot(pr, mv_ref[:, lo:lo + M_HEAD_DIM])
            yield
        n_tiles = pl.num_programs(0) - 1
        in_first_half = jnp.minimum(i, n_tiles - 1) < n_tiles // GDN_SEQS
        ob = jnp.where(in_first_half, ob0_ref[...], ob1_ref[...])
        x1 = _merge(x_ref[...], oa_ref[...], ob, oc_ref[...],
                    ga_ref[...], gb_ref[...], gc_ref[...], wo_ref)
        x1_ref[...] = x1
        x1_prev_ref[i % 2] = x1

    n_req_blocks = xq_ref.shape[0] // XA_REQ
    xa_base = jnp.minimum(i, n_req_blocks - 1) * XA_REQ
    _round_robin([_route_and_dispatch(x1_prev_ref[(i + 1) % 2], nf_ref, wrt_ref, br_ref, xs_ref,
                                      route_ref, cnt_ref),
                  mix_tile(),
                  _xattn1_requests(xq_ref, ck_ref, cv_ref, xoc_ref, xa_base)]
                 + [_gdn_step_request(sq_ref, sk_ref, sv_ref, sz_ref, sba_ref, sc_ref, sg_ref, wc_ref,
                                      alog_ref, dt_ref, ng_ref, sob_ref, snew_ref, cnew_ref, xa_base, r)
                    for r in range(XA_REQ)])


def _mix1_kernel(x_ref, au_ref, av_ref, ga_ref, gb_ref, gc_ref, ob_ref, oc_ref, ws0_ref, bs0_ref,
                 wo_ref, nf_ref, wrt_ref, br_ref, xs_in_ref,
                 x1_ref, xs_ref, route_ref, cnt_ref):
    del xs_in_ref
    oa = au_ref[...] * (ws0_ref[...] * av_ref[...] + bs0_ref[...])
    x1 = _merge(x_ref[...], oa, ob_ref[...], oc_ref[...], ga_ref[...], gb_ref[...], gc_ref[...], wo_ref)
    x1_ref[...] = x1
    for _ in _route_and_dispatch(x1, nf_ref, wrt_ref, br_ref, xs_ref, route_ref, cnt_ref):
        pass


def _route_out_specs(tt, x1_of, slot_of, route_of):
    return [pl.BlockSpec((tt, D_MODEL), lambda i: (x1_of(i), 0)),
            pl.BlockSpec((None, TILE_SLOTS, D_MODEL), lambda i: (slot_of(i), 0, 0)),
            pl.BlockSpec((tt, LANES), lambda i: (route_of(i), 0)),
            pl.BlockSpec((None, 8, LANES), lambda i: (route_of(i), 0, 0))]


def _mix(x2d, p, ob, mk, mv, w_s, bs_t, wo_bf16, norm_ffn, wr_t, br_col, seq,
         p_sample, ba_sample, cache_k, cache_v, state_conv, state_gdn, w_conv, alog_row, dt_row, norm_gdn):
    n_s = p_sample.shape[0]
    xq = p_sample[:, 6 * D_MODEL:7 * D_MODEL].reshape(n_s, M_HEADS, M_HEAD_DIM)
    n_req_blocks = n_s // XA_REQ
    req_blk = lambda i: jnp.minimum(i, n_req_blocks - 1)
    cache_spec = pl.BlockSpec((XA_REQ, N_MEM, M_HEADS, M_HEAD_DIM), lambda i: (req_blk(i), 0, 0, 0))
    whole_q = pl.BlockSpec((n_s, M_HEADS, M_HEAD_DIM), lambda i: (0, 0, 0))
    s_col = lambda colblk: pl.BlockSpec((n_s, D_MODEL), lambda i: (0, colblk))
    conv_spec = pl.BlockSpec((XA_REQ, B_CONV - 1, C_QKV), lambda i: (req_blk(i), 0, 0))
    state_spec = pl.BlockSpec((XA_REQ, B_HEADS, B_DK, B_DK), lambda i: (req_blk(i), 0, 0, 0))
    rows = x2d.shape[0]
    assert n_req_blocks <= rows // MIX_TILE + 1 and n_s % XA_REQ == 0
    tt = MIX_TILE
    nt = rows // tt
    per_b = seq // tt
    cur = lambda i: jnp.minimum(i, nt - 1)
    prev = lambda i: jnp.maximum(i - 1, 0)
    pblk = lambda colblk: pl.BlockSpec((tt, D_MODEL), lambda i: (cur(i), colblk))
    const = lambda shape: pl.BlockSpec(shape, lambda i: (0,) * len(shape))
    nh = nt // GDN_SEQS
    ob_blk = lambda g: pl.BlockSpec((tt, D_MODEL), lambda i: (jnp.clip(cur(i) - g * nh, 0, nh - 1), 0))
    x1, xs_all, route, cnt, xoc, sob, snew, cnew = pl.pallas_call(
        _mix_kernel,
        grid=(nt + 1,),
        in_specs=[pblk(0), pblk(0), pblk(1), pblk(6), pblk(7), pblk(8), pblk(9), ob_blk(0), ob_blk(1),
                  pl.BlockSpec((N_MEM, D_MODEL), lambda i: (cur(i) // per_b, 0)),
                  pl.BlockSpec((N_MEM, D_MODEL), lambda i: (cur(i) // per_b, 0)),
                  const((A_GROUPS, A_CHUNK, A_CHUNK)), const((A_CHUNK, LANES)),
                  const((D_MODEL, D_MODEL)), const((1, D_MODEL)),
                  const((LANES, D_MODEL)), const((LANES, LANES)),
                  whole_q, cache_spec, cache_spec,
                  s_col(2), s_col(3), s_col(4), s_col(5), const((n_s, LANES)), conv_spec, state_spec,
                  const((B_CONV, C_QKV)), const((1, LANES)), const((1, LANES)), const((1, B_DK))],
        out_specs=_route_out_specs(tt, cur, lambda i: i, prev)
        + [whole_q, const((n_s, D_MODEL)), state_spec, conv_spec],
        out_shape=[jax.ShapeDtypeStruct((rows, D_MODEL), F32),
                   jax.ShapeDtypeStruct((nt + 1, TILE_SLOTS, D_MODEL), BF16),
                   jax.ShapeDtypeStruct((rows, LANES), F32),
                   jax.ShapeDtypeStruct((nt, 8, LANES), F32),
                   jax.ShapeDtypeStruct((n_s, M_HEADS, M_HEAD_DIM), F32),
                   jax.ShapeDtypeStruct((n_s, D_MODEL), F32),
                   jax.ShapeDtypeStruct((n_s, B_HEADS, B_DK, B_DK), F32),
                   jax.ShapeDtypeStruct((n_s, B_CONV - 1, C_QKV), F32)],
        scratch_shapes=[pltpu.VMEM((tt, D_MODEL), F32), pltpu.VMEM((tt, D_MODEL), F32),
                        pltpu.VMEM((2, tt, D_MODEL), F32)],
        compiler_params=_params(1),
        name="mix",
    )(x2d, p, p, p, p, p, p, ob[0], ob[1], mk, mv, w_s, bs_t, wo_bf16, norm_ffn, wr_t, br_col,
      xq, cache_k, cache_v,
      p_sample, p_sample, p_sample, p_sample, ba_sample, state_conv, state_gdn,
      w_conv, alog_row, dt_row, norm_gdn)
    return x1, xs_all, route, cnt, xoc.reshape(n_s, D_MODEL), sob, snew, cnew


def _mix1(x2d, p, ob, oc, ws0_row, bs0_row, wo_bf16, norm_ffn, wr_t, br_col, xs_all):
    rows = x2d.shape[0]
    pblk = lambda colblk: pl.BlockSpec((rows, D_MODEL), lambda i: (0, colblk))
    const = lambda shape: pl.BlockSpec(shape, lambda i: (0,) * len(shape))
    zero = lambda i: 0
    return pl.pallas_call(
        _mix1_kernel,
        grid=(1,),
        in_specs=[pblk(0), pblk(0), pblk(1), pblk(7), pblk(8), pblk(9), pblk(0), pblk(0),
                  const((1, D_MODEL)), const((1, D_MODEL)),
                  const((D_MODEL, D_MODEL)), const((1, D_MODEL)),
                  const((LANES, D_MODEL)), const((LANES, LANES)),
                  pl.BlockSpec(memory_space=pl.ANY)],
        out_specs=_route_out_specs(rows, zero, zero, zero),
        out_shape=[jax.ShapeDtypeStruct((rows, D_MODEL), F32),
                   jax.ShapeDtypeStruct(xs_all.shape, BF16),
                   jax.ShapeDtypeStruct((rows, LANES), F32),
                   jax.ShapeDtypeStruct((1, 8, LANES), F32)],
        input_output_aliases={14: 1},
        compiler_params=_params(1),
        name="mix1",
    )(x2d, p, p, p, p, p, ob, oc, ws0_row, bs0_row, wo_bf16, norm_ffn, wr_t, br_col, xs_all)


def _experts_kernel(tbl_ref, se_ref, nu_ref, *refs):
    del tbl_ref
    pages = refs[:STEP_PAGES]
    wg_ref, wu_ref, wd_ref, o_ref, wgb_ref, wub_ref, wdb_ref = refs[STEP_PAGES:]
    s = pl.program_id(0)
    prev = se_ref[jnp.maximum(s - 1, 0)]

    @pl.when((s == 0) | (se_ref[s] != prev))
    def _():
        wgb_ref[...] = wg_ref[...].astype(BF16)
        wub_ref[...] = wu_ref[...].astype(BF16)
        wdb_ref[...] = wd_ref[...].astype(BF16)

    @pl.when(s < nu_ref[0])
    def _():
        x = jnp.concatenate([pg[...] for pg in pages], axis=0)
        gate = jnp.dot(x, wgb_ref[...], preferred_element_type=F32)
        up = jnp.dot(x, wub_ref[...], preferred_element_type=F32)
        act = (jax.nn.silu(gate) * up).astype(BF16)
        o_ref[...] = jnp.dot(act, wdb_ref[...], preferred_element_type=F32).astype(BF16)

    @pl.when(s >= nu_ref[0])
    def _():
        o_ref[...] = jnp.zeros_like(o_ref)


def _experts(tbl, step_e, n_used, xs_pages, w_gate, w_up, w_down, n_steps):
    def page_spec(k):
        return pl.BlockSpec((None, PAGE_ROWS, D_MODEL),
                            lambda s, tbl, se, nu: (tbl[s * STEP_PAGES + k], 0, 0))

    wspec = lambda shape: pl.BlockSpec((None,) + shape, lambda s, tbl, se, nu: (se[s], 0, 0))
    grid_spec = pltpu.PrefetchScalarGridSpec(
        num_scalar_prefetch=3,
        grid=(n_steps,),
        in_specs=[page_spec(k) for k in range(STEP_PAGES)]
        + [wspec((D_MODEL, D_EXPERT)), wspec((D_MODEL, D_EXPERT)), wspec((D_EXPERT, D_MODEL))],
        out_specs=pl.BlockSpec((STEP_ROWS, D_MODEL), lambda s, tbl, se, nu: (s, 0)),
        scratch_shapes=[pltpu.VMEM((D_MODEL, D_EXPERT), BF16), pltpu.VMEM((D_MODEL, D_EXPERT), BF16),
                        pltpu.VMEM((D_EXPERT, D_MODEL), BF16)],
    )
    return pl.pallas_call(
        _experts_kernel,
        grid_spec=grid_spec,
        out_shape=jax.ShapeDtypeStruct((n_steps * STEP_ROWS, D_MODEL), BF16),
        compiler_params=_params(1),
        name="experts",
    )(tbl, step_e, n_used, *([xs_pages] * STEP_PAGES), w_gate, w_up, w_down)


def _combine_kernel(inv_ref, *refs):
    del inv_ref
    pages = refs[:TILE_PAGES]
    x1_ref, route_ref, nfin_ref, y_ref = refs[TILE_PAGES:]
    out_loc = jnp.concatenate([pg[...] for pg in pages], axis=0)
    tt = x1_ref.shape[0]
    route = route_ref[...]
    s_id = _iota((tt, TILE_SLOTS), 1).astype(F32)
    sel = (jnp.where(s_id == route[:, 0:1], route[:, 2:3], 0.0)
           + jnp.where(s_id == route[:, 1:2], route[:, 3:4], 0.0)).astype(BF16)
    moe = jnp.dot(sel, out_loc, preferred_element_type=F32)
    y_ref[...] = _rms(x1_ref[...] + moe, nfin_ref[...])


def _combine(inv, out_pages, x1, route, norm_final, tt):
    rows = x1.shape[0]

    def page_spec(k):
        return pl.BlockSpec((None, PAGE_ROWS, D_MODEL), lambda i, inv: (inv[i * TILE_PAGES + k], 0, 0))

    grid_spec = pltpu.PrefetchScalarGridSpec(
        num_scalar_prefetch=1,
        grid=(rows // tt,),
        in_specs=[page_spec(k) for k in range(TILE_PAGES)]
        + [pl.BlockSpec((tt, D_MODEL), lambda i, inv: (i, 0)),
           pl.BlockSpec((tt, LANES), lambda i, inv: (i, 0)),
           pl.BlockSpec((1, D_MODEL), lambda i, inv: (0, 0))],
        out_specs=pl.BlockSpec((tt, D_MODEL), lambda i, inv: (i, 0)),
    )
    return pl.pallas_call(
        _combine_kernel,
        grid_spec=grid_spec,
        out_shape=jax.ShapeDtypeStruct((rows, D_MODEL), F32),
        compiler_params=_params(1),
        name="combine",
    )(inv, *([out_pages] * TILE_PAGES), x1, route, norm_final)


def _page_tables(cnt, n_steps):
    n_tiles = cnt.shape[0]
    pg = (cnt + (PAGE_ROWS - 1)) // PAGE_ROWS
    lend = jnp.cumsum(pg, axis=1)
    loff = lend - pg
    cum_t = jnp.cumsum(pg, axis=0)
    pref = cum_t - pg
    tot = cum_t[-1]
    totp = ((tot + (STEP_PAGES - 1)) // STEP_PAGES) * STEP_PAGES
    gend = jnp.cumsum(totp)
    gstart = gend - totp
    sel = lambda onehot, vals: jnp.dot(onehot, vals.astype(F32), precision=lax.Precision.HIGHEST)
    gp = jnp.arange(n_steps * STEP_PAGES, dtype=jnp.int32)
    in_e = (gp[:, None] >= gstart[None, :]) & (gp[:, None] < gend[None, :])
    oh_e = in_e.astype(F32)
    r = gp.astype(F32) - sel(oh_e, gstart)
    valid = jnp.any(in_e, axis=1) & (r < sel(oh_e, tot))
    cum_col = sel(oh_e, cum_t.T)
    pref_col = sel(oh_e, pref.T)
    loff_col = sel(oh_e, loff.T)
    in_i = (r[:, None] >= pref_col) & (r[:, None] < cum_col)
    tile_base = (jnp.arange(n_tiles, dtype=jnp.int32) * TILE_PAGES).astype(F32)
    src = jnp.sum(jnp.where(in_i, tile_base[None, :] + loff_col + (r[:, None] - pref_col), 0.0), axis=1)
    tbl = jnp.where(valid, src, 0.0).astype(jnp.int32)
    in_step = in_e[::STEP_PAGES]
    e_ids = jnp.arange(N_EXPERTS, dtype=jnp.int32)
    step_e = jnp.where(jnp.any(in_step, axis=1), jnp.sum(jnp.where(in_step, e_ids[None, :], 0), axis=1),
                       N_EXPERTS - 1).astype(jnp.int32)
    n_used = (gend[-1] // STEP_PAGES).astype(jnp.int32).reshape(1)
    lp = jnp.arange(TILE_PAGES, dtype=jnp.int32)[None, :, None]
    in_l = (lp >= loff[:, None, :]) & (lp < lend[:, None, :])
    gpos = jnp.sum(jnp.where(in_l, gstart[None, None, :] + pref[:, None, :] + lp - loff[:, None, :], 0), axis=2)
    zero_page = (n_steps - 1) * STEP_PAGES
    inv = jnp.where(jnp.any(in_l, axis=2), gpos, zero_page).astype(jnp.int32).reshape(-1)
    return tbl, step_e, n_used, inv


def kernel(x_prompt, x_sample, mem_prompt, cache_mem_k, cache_mem_v, state_gdn, state_conv,
           norm_mix, w_in, b_gate, w_s, b_s, norm_a_v, w_conv, a_log, dt_bias, norm_gdn_out,
           norm_mem, w_mem_kv, w_o, norm_ffn, w_router_group, b_router_group, w_router_expert,
           b_router_expert, w_exp_gate, w_exp_up, w_exp_down, norm_final):
    depth = norm_mix.shape[0]
    assert depth == 1
    bsz, seq, _ = x_prompt.shape
    n_s = x_sample.shape[0]
    assert x_sample.shape[1] == 1 and seq % MIX_TILE == 0 and n_s % 8 == 0 and n_s <= MIX_TILE
    assert bsz % GDN_SEQS == 0
    l = 0
    row = lambda v: v.reshape(1, -1)

    wi = w_in[l]
    o_z = 2 * D_MODEL + C_QKV
    o_beta = o_z + D_MODEL
    o_qm = o_beta + 2 * B_HEADS
    o_gate = o_qm + D_MODEL
    w_main = jnp.concatenate([wi[:, :o_beta], wi[:, o_qm:]], axis=1).astype(BF16)
    w_ba = jnp.pad(wi[:, o_beta:o_qm], ((0, 0), (0, LANES - 2 * B_HEADS))).astype(BF16)
    pad_heads = lambda v: jnp.pad(v.reshape(1, B_HEADS), ((0, 0), (B_HEADS, LANES - 2 * B_HEADS)))
    alog_row = pad_heads(a_log[l])
    dt_row = pad_heads(dt_bias[l])
    w_kv = w_mem_kv[l].astype(BF16)
    wo = w_o[l].astype(BF16)
    wr_t = jnp.pad(jnp.concatenate([w_router_expert[l], w_router_group[l]], axis=1).T,
                   ((0, LANES - N_EXPERTS - N_GROUPS), (0, 0)))
    br_col = jnp.pad(jnp.concatenate([b_router_expert[l], b_router_group[l]]).reshape(-1, 1),
                     ((0, LANES - N_EXPERTS - N_GROUPS), (0, LANES - 1)))
    bs_t = jnp.pad(b_s[l].T, ((0, 0), (0, LANES - A_GROUPS)))
    ws0_row = jnp.repeat(w_s[l][:, 0, 0], A_CHUNK).reshape(1, D_MODEL)
    bs0_row = jnp.repeat(b_s[l][:, 0], A_CHUNK).reshape(1, D_MODEL)

    xp = x_prompt.reshape(bsz * seq, D_MODEL)
    xs_ = x_sample.reshape(n_s, D_MODEL)
    n_tiles_p = (bsz * seq) // MIX_TILE
    n_tiles = n_tiles_p + 1

    p_s, ba_s = _proj(xs_, row(norm_mix[l]), w_main, w_ba, row(norm_a_v[l]), row(b_gate[l]), tm=n_s)
    mk, mv = _memkv(mem_prompt.reshape(bsz * N_MEM, D_MODEL), row(norm_mem[l]), w_kv)
    p_p, ba_p = _proj(xp, row(norm_mix[l]), w_main, w_ba, row(norm_a_v[l]), row(b_gate[l]),
                      tm=min(PROJ_ROWS, bsz * seq))
    ob_p, s_p = _gdn(p_p, ba_p, w_conv[l], alog_row, dt_row, row(norm_gdn_out[l]), bsz, seq)
    x1_p, xs_all, route_p, cnt_p, oc_s, ob_s, s_s, c_s = _mix(
        xp, p_p, ob_p, mk, mv, w_s[l], bs_t, wo, row(norm_ffn[l]), wr_t, br_col, seq,
        p_s, ba_s, cache_mem_k[l], cache_mem_v[l], state_conv[l], state_gdn[l], w_conv[l], alog_row, dt_row,
        row(norm_gdn_out[l]))
    x1_s, xs_all, route_s, cnt_s = _mix1(xs_, p_s, ob_s, oc_s, ws0_row, bs0_row, wo, row(norm_ffn[l]),
                                         wr_t, br_col, xs_all)
    cnt = jnp.concatenate([cnt_s[:, 0, :N_EXPERTS], cnt_p[:, 0, :N_EXPERTS]], axis=0).astype(jnp.int32)
    max_pages = n_tiles_p * TILE_PAGES + (2 * n_s) // PAGE_ROWS + N_EXPERTS
    n_steps = (max_pages + N_EXPERTS * (STEP_PAGES - 1)) // STEP_PAGES + 1
    tbl, step_e, n_used, inv = _page_tables(cnt, n_steps)
    ne = N_GROUPS * EXP_PER_GROUP
    out_sorted = _experts(tbl, step_e, n_used, xs_all.reshape(n_tiles * TILE_PAGES, PAGE_ROWS, D_MODEL),
                          w_exp_gate[l].reshape(ne, D_MODEL, D_EXPERT),
                          w_exp_up[l].reshape(ne, D_MODEL, D_EXPERT),
                          w_exp_down[l].reshape(ne, D_EXPERT, D_MODEL), n_steps)
    out_pages = out_sorted.reshape(n_steps * STEP_PAGES, PAGE_ROWS, D_MODEL)
    y_s = _combine(inv[:TILE_PAGES], out_pages, x1_s, route_s, row(norm_final), n_s)
    y_p = _combine(inv[TILE_PAGES:], out_pages, x1_p, route_p, row(norm_final), MIX_TILE)

    conv_tail = p_p.reshape(bsz, seq, PROJ_COLS)[:, seq - (B_CONV - 1):, 2 * D_MODEL:2 * D_MODEL + C_QKV]
    return (y_p.reshape(bsz, seq, D_MODEL),
            y_s.reshape(n_s, 1, D_MODEL),
            mk.reshape(1, bsz, N_MEM, M_HEADS, M_HEAD_DIM),
            mv.reshape(1, bsz, N_MEM, M_HEADS, M_HEAD_DIM),
            jnp.concatenate(s_p, axis=0)[None],
            conv_tail[None],
            s_s[None],
            c_s[None],
            p_s[:, D_MODEL:2 * D_MODEL].reshape(1, n_s, 1, D_MODEL))
```

```python
import functools
import math

import jax
import jax.numpy as jnp
from jax import lax
from jax.experimental import pallas as pl
from jax.experimental.pallas import tpu as pltpu

F32 = jnp.float32
BF16 = jnp.bfloat16

D_MODEL = 1024
A_GROUPS = 8
A_CHUNK = 128
B_HEADS = 8
B_DK = 128
C_QKV = 3 * D_MODEL
B_CONV = 4
N_MEM = 256
M_HEADS = 4
M_HEAD_DIM = 256
N_GROUPS = 4
EXP_PER_GROUP = 8
N_EXPERTS = 32
D_EXPERT = 512
EPS = 1e-6

LANES = 128
GDN_CHUNK = 128
GDN_SEQS = 2
MIX_TILE = 256
PAGE_ROWS = 16
TILE_PAGES = 64
TILE_SLOTS = PAGE_ROWS * TILE_PAGES
STEP_PAGES = 32
STEP_ROWS = PAGE_ROWS * STEP_PAGES
PROJ_COLS = 10 * D_MODEL
PROJ_CHUNK = 256
PROJ_ROWS = 2048
PROJ_PIECE_ROWS = 1024
VMEM_LIMIT = 56 * 1024 * 1024


def _params(n_grid):
    return pltpu.CompilerParams(dimension_semantics=("arbitrary",) * n_grid,
                                vmem_limit_bytes=VMEM_LIMIT)


def _rms(x, g):
    return x * lax.rsqrt(jnp.mean(x * x, axis=-1, keepdims=True) + EPS) * g


def _dot(a, b):
    return jnp.dot(a.astype(BF16), b.astype(BF16), preferred_element_type=F32)


def _dot_nt(a, b):
    return lax.dot_general(a.astype(BF16), b.astype(BF16), (((1,), (1,)), ((), ())),
                           preferred_element_type=F32)


def _dot_tn(a, b):
    return lax.dot_general(a.astype(BF16), b.astype(BF16), (((0,), (0,)), ((), ())),
                           preferred_element_type=F32)


def _split3(x):
    hi = x.astype(BF16)
    r = x - hi.astype(F32)
    mid = r.astype(BF16)
    lo = (r - mid.astype(F32)).astype(BF16)
    return hi, mid, lo


def _iota(shape, axis):
    return lax.broadcasted_iota(jnp.int32, shape, axis)


def _memkv_kernel(mem_ref, g_ref, w_ref, k_ref, v_ref):
    h = _rms(mem_ref[...], g_ref[...]).astype(BF16)
    kv = jnp.dot(h, w_ref[...], preferred_element_type=F32)
    k_ref[...] = kv[:, :D_MODEL]
    v_ref[...] = kv[:, D_MODEL:]


def _memkv(mem2d, norm_mem, w_kv_bf16):
    rows = mem2d.shape[0]
    nb = rows // N_MEM
    return pl.pallas_call(
        _memkv_kernel,
        grid=(nb,),
        in_specs=[pl.BlockSpec((N_MEM, D_MODEL), lambda i: (i, 0)),
                  pl.BlockSpec((1, D_MODEL), lambda i: (0, 0)),
                  pl.BlockSpec((D_MODEL, 2 * D_MODEL), lambda i: (0, 0))],
        out_specs=[pl.BlockSpec((N_MEM, D_MODEL), lambda i: (i, 0)),
                   pl.BlockSpec((N_MEM, D_MODEL), lambda i: (i, 0))],
        out_shape=[jax.ShapeDtypeStruct((rows, D_MODEL), F32)] * 2,
        compiler_params=_params(1),
        name="memkv",
    )(mem2d, norm_mem, w_kv_bf16)


def _proj_kernel(x_ref, nm_ref, w_ref, wba_ref, nav_ref, bg_ref, p_ref, ba_ref, hn_ref):
    j = pl.program_id(1)

    @pl.when(j == 0)
    def _():
        hn = _rms(x_ref[...], nm_ref[...]).astype(BF16)
        hn_ref[...] = hn
        ba_ref[...] = jnp.dot(hn, wba_ref[...], preferred_element_type=F32)

    tm = x_ref.shape[0]
    rows = [(r0, min(tm, r0 + PROJ_PIECE_ROWS)) for r0 in range(0, tm, PROJ_PIECE_ROWS)]
    pieces = [(r, c0) for r in rows for c0 in range(0, D_MODEL, PROJ_CHUNK)]

    def chunks(epilogue):
        def matmul(piece):
            (r0, r1), c0 = piece
            return jnp.dot(hn_ref[r0:r1, :], w_ref[:, c0:c0 + PROJ_CHUNK], preferred_element_type=F32)

        def finish(piece, acc):
            (r0, r1), c0 = piece
            p_ref[r0:r1, c0:c0 + PROJ_CHUNK] = epilogue(acc, piece)

        acc = matmul(pieces[0])
        for n in range(1, len(pieces)):
            nxt = matmul(pieces[n])
            finish(pieces[n - 1], acc)
            acc = nxt
        finish(pieces[-1], acc)

    @pl.when(j == 0)
    def _():
        chunks(lambda acc, piece: jax.nn.gelu(acc))

    @pl.when(j == 1)
    def _():
        ssq = {r: [] for r in rows}

        def gelu_ssq(acc, piece):
            a = jax.nn.gelu(acc)
            ssq[piece[0]].append(jnp.sum(a * a, axis=-1, keepdims=True))
            return a

        chunks(gelu_ssq)
        for r0, r1 in rows:
            inv = lax.rsqrt(sum(ssq[(r0, r1)]) * (1.0 / D_MODEL) + EPS)
            p_ref[r0:r1, :] = p_ref[r0:r1, :] * inv * nav_ref[...]

    @pl.when((j >= 2) & (j <= 6))
    def _():
        chunks(lambda acc, piece: acc)

    @pl.when(j >= 7)
    def _():
        chunks(lambda acc, piece: jax.nn.sigmoid(acc + bg_ref[:, piece[1]:piece[1] + PROJ_CHUNK]))


def _proj(x2d, norm_mix, w_bf16, wba_bf16, norm_a_v, b_gate, tm):
    rows = x2d.shape[0]
    ncol = PROJ_COLS // D_MODEL
    return pl.pallas_call(
        _proj_kernel,
        grid=(rows // tm, ncol),
        in_specs=[pl.BlockSpec((tm, D_MODEL), lambda i, j: (i, 0)),
                  pl.BlockSpec((1, D_MODEL), lambda i, j: (0, 0)),
                  pl.BlockSpec((D_MODEL, D_MODEL), lambda i, j: (0, j)),
                  pl.BlockSpec((D_MODEL, LANES), lambda i, j: (0, 0)),
                  pl.BlockSpec((1, D_MODEL), lambda i, j: (0, 0)),
                  pl.BlockSpec((1, D_MODEL), lambda i, j: (0, jnp.maximum(j - 7, 0)))],
        out_specs=[pl.BlockSpec((tm, D_MODEL), lambda i, j: (i, j)),
                   pl.BlockSpec((tm, LANES), lambda i, j: (i, 0))],
        out_shape=[jax.ShapeDtypeStruct((rows, PROJ_COLS), F32),
                   jax.ShapeDtypeStruct((rows, LANES), F32)],
        scratch_shapes=[pltpu.VMEM((tm, D_MODEL), BF16)],
        compiler_params=_params(2),
        name="proj",
    )(x2d, norm_mix, w_bf16, wba_bf16, norm_a_v, b_gate)


def _proj2_kernel(x_ref, nm_ref, w_ref, wl_ref, wba_ref, wbal_ref, nav_ref, bg_ref, p_ref, ba_ref,
                  hh_ref, hl_ref):
    j = pl.program_id(1)

    def dot2(w_hi_ref, w_lo_ref):
        return (jnp.dot(hh_ref[...], w_hi_ref[...], preferred_element_type=F32)
                + (jnp.dot(hl_ref[...], w_hi_ref[...], preferred_element_type=F32)
                   + jnp.dot(hh_ref[...], w_lo_ref[...], preferred_element_type=F32)))

    @pl.when(j == 0)
    def _():
        h = _rms(x_ref[...], nm_ref[...])
        hi = h.astype(BF16)
        hh_ref[...] = hi
        hl_ref[...] = (h - hi.astype(F32)).astype(BF16)
        ba_ref[...] = dot2(wba_ref, wbal_ref)

    acc = dot2(w_ref, wl_ref)

    @pl.when(j == 0)
    def _():
        p_ref[...] = jax.nn.gelu(acc)

    @pl.when(j == 1)
    def _():
        p_ref[...] = _rms(jax.nn.gelu(acc), nav_ref[...])

    @pl.when((j >= 2) & (j <= 6))
    def _():
        p_ref[...] = acc

    @pl.when(j >= 7)
    def _():
        p_ref[...] = jax.nn.sigmoid(acc + bg_ref[...])


def _proj2(x2d, norm_mix, w_hi, w_lo, wba_hi, wba_lo, norm_a_v, b_gate):
    rows = x2d.shape[0]
    ncol = PROJ_COLS // D_MODEL
    wcol = pl.BlockSpec((D_MODEL, D_MODEL), lambda i, j: (0, j))
    wba = pl.BlockSpec((D_MODEL, LANES), lambda i, j: (0, 0))
    vec = pl.BlockSpec((1, D_MODEL), lambda i, j: (0, 0))
    return pl.pallas_call(
        _proj2_kernel,
        grid=(1, ncol),
        in_specs=[pl.BlockSpec((rows, D_MODEL), lambda i, j: (0, 0)), vec, wcol, wcol, wba, wba, vec,
                  pl.BlockSpec((1, D_MODEL), lambda i, j: (0, jnp.maximum(j - 7, 0)))],
        out_specs=[pl.BlockSpec((rows, D_MODEL), lambda i, j: (0, j)),
                   pl.BlockSpec((rows, LANES), lambda i, j: (0, 0))],
        out_shape=[jax.ShapeDtypeStruct((rows, PROJ_COLS), F32),
                   jax.ShapeDtypeStruct((rows, LANES), F32)],
        scratch_shapes=[pltpu.VMEM((rows, D_MODEL), BF16), pltpu.VMEM((rows, D_MODEL), BF16)],
        compiler_params=_params(2),
        name="proj2",
    )(x2d, norm_mix, w_hi, w_lo, wba_hi, wba_lo, norm_a_v, b_gate)


def _gate_terms(ba, alog_row, dt_row):
    beta = jax.nn.sigmoid(ba)
    g = -jnp.exp(alog_row) * jax.nn.softplus(ba + dt_row)
    return beta, g


class _Lazy:
    def __init__(self, make):
        self._make = make

    def __getitem__(self, index):
        return self._make(index)


def _round_robin(tasks):
    done = object()
    while tasks:
        tasks = [t for t in tasks if next(t, done) is not done]


def _gdn_kernel(n_c, *refs):
    per = 5
    q_refs, k_refs, v_refs, ba_refs, z_refs = (
        [refs[per * g + j] for g in range(GDN_SEQS)] for j in range(per))
    wc_ref, alog_ref, dt_ref, ng_ref = refs[per * GDN_SEQS:per * GDN_SEQS + 4]
    ob_refs = refs[per * GDN_SEQS + 4:per * GDN_SEQS + 4 + GDN_SEQS]
    s_out_refs = refs[per * GDN_SEQS + 4 + GDN_SEQS:per * GDN_SEQS + 4 + 2 * GDN_SEQS]
    s_ref, ext_ref, prep_ref, gate_ref = refs[per * GDN_SEQS + 4 + 2 * GDN_SEQS:]
    s = pl.program_id(0)
    C = GDN_CHUNK
    par = s % 2
    heads = range(B_HEADS)

    @pl.when(s == 0)
    def _():
        s_ref[...] = jnp.zeros_like(s_ref)
        ext_ref[:, 0:8, :] = jnp.zeros((GDN_SEQS, 8, C_QKV), F32)
        prep_ref[...] = jnp.zeros_like(prep_ref)
        gate_ref[...] = jnp.zeros_like(gate_ref)

    row = _iota((C, C), 0)
    col = _iota((C, C), 1)
    incl = row >= col
    pace = []

    def mark(results):
        pace.append(results[-1][0:1, 0:B_DK] * 0.0)

    def prepare(g):
        q_ref, k_ref, v_ref, ba_ref = q_refs[g], k_refs[g], v_refs[g], ba_refs[g]
        keep = (s % n_c != 0).astype(F32)
        ext_ref[g, 0:8, :] = ext_ref[g, 0:8, :] * keep
        ext_ref[g, 8:8 + C, 0:D_MODEL] = q_ref[...]
        ext_ref[g, 8:8 + C, D_MODEL:2 * D_MODEL] = k_ref[...]
        ext_ref[g, 8:8 + C, 2 * D_MODEL:3 * D_MODEL] = v_ref[...]
        beta_all, g_all = _gate_terms(ba_ref[...], alog_ref[...], dt_ref[...])
        g_hi, g_mid, g_lo = _split3(g_all)
        tri = incl.astype(BF16)
        gcum = (jnp.dot(tri, g_hi, preferred_element_type=F32)
                + jnp.dot(tri, g_mid, preferred_element_type=F32)
                + jnp.dot(tri, g_lo, preferred_element_type=F32))
        gate_ref[par, g, 0] = beta_all
        gate_ref[par, g, 1] = gcum
        gate_ref[par, g, 2] = gcum.T
        yield

        def conv_silu(lo):
            pace_row = pace[-1] if pace else 0.0
            acc = (wc_ref[3:4, lo:lo + B_DK] + pace_row) * ext_ref[g, 8:8 + C, lo:lo + B_DK]
            for j in range(1, B_CONV):
                acc = acc + ((wc_ref[3 - j:4 - j, lo:lo + B_DK] + pace_row)
                             * ext_ref[g, 8 - j:8 - j + C, lo:lo + B_DK])
            return jax.nn.silu(acc)

        def l2n(t):
            return t * lax.rsqrt(jnp.sum(t * t, axis=-1, keepdims=True) + EPS)

        for h in heads:
            lo = h * B_DK
            prep_ref[par, g, :, lo:lo + B_DK] = l2n(conv_silu(lo)) * (B_DK ** -0.5)
            yield
            prep_ref[par, g, :, D_MODEL + lo:D_MODEL + lo + B_DK] = l2n(conv_silu(D_MODEL + lo))
            yield
            prep_ref[par, g, :, 2 * D_MODEL + lo:2 * D_MODEL + lo + B_DK] = conv_silu(2 * D_MODEL + lo)
            yield
        ext_ref[g, 0:8, :] = ext_ref[g, C:C + 8, :]

    def chain():
        heads = range(GDN_SEQS * B_HEADS)
        half = [u // B_HEADS for u in heads]
        head = [u % B_HEADS for u in heads]
        old = 1 - par
        strict = row > col
        eye = (row == col).astype(F32)
        same_blk = [(row // b) == (col // b) for b in (8, 16, 32, 64, 128)]
        beta_all = [gate_ref[old, g, 0] for g in range(GDN_SEQS)]
        gcum = [gate_ref[old, g, 1] for g in range(GDN_SEQS)]
        gcum_t = [gate_ref[old, g, 2] for g in range(GDN_SEQS)]
        lanes = lambda part, u: slice(part * D_MODEL + head[u] * B_DK, part * D_MODEL + (head[u] + 1) * B_DK)
        q = _Lazy(lambda u: prep_ref[old, half[u], :, lanes(0, u)])
        k = _Lazy(lambda u: prep_ref[old, half[u], :, lanes(1, u)])
        v = _Lazy(lambda u: prep_ref[old, half[u], :, lanes(2, u)])
        bcol = [beta_all[half[u]][:, head[u]:head[u] + 1] for u in heads]
        gcol = [gcum[half[u]][:, 8 + head[u]:9 + head[u]] for u in heads]
        glast = [gcum[half[u]][C - 1:C, 8 + head[u]:9 + head[u]] for u in heads]
        eg = [jnp.exp(gcol[h]) for h in heads]
        kb = _Lazy(lambda u: k[u] * bcol[u])
        yield
        a1 = [_dot_nt(jnp.concatenate([kb[h], q[h]], axis=0), k[h]) for h in heads]
        mark(a1)
        yield
        m, attn = [], []
        for u in heads:
            grow = gcum_t[half[u]][8 + head[u]:9 + head[u], :]
            dec = jnp.where(incl, jnp.exp(jnp.where(incl, gcol[u] - grow, 0.0)), 0.0)
            m.append(jnp.where(strict, a1[u][:C] * dec, 0.0))
            attn.append(a1[u][C:] * dec)
        m0 = [jnp.where(same_blk[0], m[h], 0.0) for h in heads]
        x = [eye - m0[h] for h in heads]
        pw = [_dot(m0[h], m0[h]) for h in heads]
        mark(pw)
        yield
        x = [x[h] + _dot(x[h], pw[h]) for h in heads]
        mark(x)
        yield
        pw = [_dot(pw[h], pw[h]) for h in heads]
        mark(pw)
        yield
        x = [x[h] + _dot(x[h], pw[h]) for h in heads]
        mark(x)
        yield
        for lvl in range(1, len(same_blk)):
            in_lvl = same_blk[lvl] & jnp.logical_not(same_blk[lvl - 1])
            xb = [x[h].astype(BF16) for h in heads]
            t = [_dot(xb[h], jnp.where(in_lvl, m[h], 0.0)) for h in heads]
            mark(t)
            yield
            x = [x[h] - _dot(t[h], xb[h]) for h in heads]
            mark(x)
            yield
        rhs = [jnp.concatenate([v[h] * bcol[h], kb[h] * eg[h]], axis=1) for h in heads]
        sol = [_dot(x[h], rhs[h]) for h in heads]
        mark(sol)
        yield
        keep = ((s + n_c - 1) % n_c != 0).astype(F32)
        s_old = [s_ref[half[u], head[u]] * keep for u in heads]
        b1 = [_dot(jnp.concatenate([sol[h][:, B_DK:], q[h] * eg[h]], axis=0), s_old[h]) for h in heads]
        mark(b1)
        yield
        u = [sol[h][:, :B_DK] - b1[h][:C] for h in heads]
        o = [b1[h][C:] + _dot(attn[h], u[h]) for h in heads]
        kd = [k[h] * jnp.exp(glast[h] - gcol[h]) for h in heads]
        mark(o)
        yield
        s_new = [s_old[h] * jnp.exp(glast[h]) + _dot_tn(kd[h], u[h]) for h in heads]
        mark(s_new)
        yield
        for u in heads:
            g, h = half[u], head[u]
            s_ref[g, h] = s_new[u]
            s_out_refs[g][h] = s_new[u]
            zh = z_refs[g][:, h * B_DK:(h + 1) * B_DK]
            ob_refs[g][:, h * B_DK:(h + 1) * B_DK] = _rms(o[u], ng_ref[...]) * jax.nn.silu(zh)
            yield

    done = object()
    chain_task = chain()
    prep_tasks = [prepare(g) for g in range(GDN_SEQS)]
    for _ in range(3):
        next(chain_task)
    alive = True
    while alive:
        alive = False
        for task in prep_tasks + [chain_task] + prep_tasks + prep_tasks + [chain_task]:
            alive = (next(task, done) is not done) or alive


def _gdn(p, ba, w_conv, alog_row, dt_row, norm_gdn, bsz, seq):
    nc = seq // GDN_CHUNK
    total = (bsz // GDN_SEQS) * nc
    cur = lambda s: jnp.minimum(s, total - 1)
    prev = lambda s: jnp.maximum(s - 1, 0)
    small = lambda shape: pl.BlockSpec(shape, lambda s: (0, 0))

    def half_specs(g):
        blk = lambda colblk: pl.BlockSpec((GDN_CHUNK, D_MODEL), lambda s: (g * total + cur(s), colblk))
        return [blk(2), blk(3), blk(4),
                pl.BlockSpec((GDN_CHUNK, LANES), lambda s: (g * total + cur(s), 0)),
                pl.BlockSpec((GDN_CHUNK, D_MODEL), lambda s: (g * total + prev(s), 5))]

    halves = range(GDN_SEQS)
    outs = pl.pallas_call(
        functools.partial(_gdn_kernel, nc),
        grid=(total + 1,),
        in_specs=[spec for g in halves for spec in half_specs(g)]
        + [small((B_CONV, C_QKV)), small((1, LANES)), small((1, LANES)), small((1, B_DK))],
        out_specs=[pl.BlockSpec((GDN_CHUNK, D_MODEL), lambda s: (prev(s), 0)) for g in halves]
        + [pl.BlockSpec((None, B_HEADS, B_DK, B_DK), lambda s: (prev(s) // nc, 0, 0, 0)) for g in halves],
        out_shape=[jax.ShapeDtypeStruct((total * GDN_CHUNK, D_MODEL), F32) for g in halves]
        + [jax.ShapeDtypeStruct((bsz // GDN_SEQS, B_HEADS, B_DK, B_DK), F32) for g in halves],
        scratch_shapes=[pltpu.VMEM((GDN_SEQS, B_HEADS, B_DK, B_DK), F32),
                        pltpu.VMEM((GDN_SEQS, 8 + GDN_CHUNK, C_QKV), F32),
                        pltpu.VMEM((2, GDN_SEQS, GDN_CHUNK, C_QKV), F32),
                        pltpu.VMEM((2, GDN_SEQS, 3, GDN_CHUNK, LANES), F32)],
        compiler_params=_params(1),
        name="gdn",
    )(*([p, p, p, ba, p] * GDN_SEQS), w_conv, alog_row, dt_row, norm_gdn)
    return outs[:GDN_SEQS], outs[GDN_SEQS:]


def _gdn_step_request(q_ref, k_ref, v_ref, z_ref, ba_ref, sc_ref, sg_ref, wc_ref, alog_ref, dt_ref,
                      ng_ref, ob_ref, snew_ref, cnew_ref, base, r):
    def l2n(t):
        return t * lax.rsqrt(jnp.sum(t * t, axis=-1, keepdims=True) + EPS)

    def column(t):
        return jnp.broadcast_to(t, (B_DK, B_DK)).T

    row = pl.ds(base + r, 1)
    beta_all, g_all = _gate_terms(ba_ref[row, :], alog_ref[...], dt_ref[...])
    eg_all = jnp.exp(g_all)
    pre = jnp.concatenate([q_ref[row, :], k_ref[row, :], v_ref[row, :]], axis=1)
    conv = wc_ref[3:4, :] * pre
    for j in range(B_CONV - 1):
        conv = conv + wc_ref[j:j + 1, :] * sc_ref[r, j:j + 1, :]
    cnew_ref[r, 0:1, :] = sc_ref[r, 1:2, :]
    cnew_ref[r, 1:2, :] = sc_ref[r, 2:3, :]
    cnew_ref[r, 2:3, :] = pre
    qkv = jax.nn.silu(conv)
    gate = jax.nn.silu(z_ref[row, :])
    out = []
    yield
    for h in range(B_HEADS):
        lo = h * B_DK
        qcol = column(l2n(qkv[:, lo:lo + B_DK]) * (B_DK ** -0.5))
        kcol = column(l2n(qkv[:, D_MODEL + lo:D_MODEL + lo + B_DK]))
        vh = qkv[:, 2 * D_MODEL + lo:2 * D_MODEL + lo + B_DK]
        s_old = sg_ref[r, h]
        beta = beta_all[:, h:h + 1]
        eg = eg_all[:, 8 + h:9 + h]
        ks = jnp.sum(s_old * kcol, axis=0, keepdims=True)
        u = beta * (vh - eg * ks)
        s_new = s_old * eg + kcol * u
        snew_ref[r, h] = s_new
        yield
        o = jnp.sum(s_new * qcol, axis=0, keepdims=True)
        out.append(_rms(o, ng_ref[...]) * gate[:, lo:lo + B_DK])
        yield
    ob_ref[row, :] = jnp.concatenate(out, axis=1)


XA_REQ = 2


def _xattn1_requests(q_ref, ck_ref, cv_ref, oc_ref, base):
    scale = M_HEAD_DIM ** -0.5
    pairs = (N_MEM // 2, 2 * M_HEADS, M_HEAD_DIM)
    fold = lambda t, op: op(t[0:M_HEADS], t[M_HEADS:2 * M_HEADS])
    twice = lambda t: jnp.concatenate([t, t], axis=0)
    for r in range(XA_REQ):
        q = twice(q_ref[base + r])
        s = jnp.sum(ck_ref[r].reshape(pairs) * q[None], axis=-1, keepdims=True) * scale
        yield
        mx = twice(fold(jnp.max(s, axis=0), jnp.maximum))
        p = jnp.exp(s - mx[None])
        den = twice(fold(jnp.sum(p, axis=0), jnp.add))
        pr = p / den[None]
        yield
        oc_ref[base + r] = fold(jnp.sum(pr * cv_ref[r].reshape(pairs), axis=0), jnp.add)
        yield


def _route_and_dispatch(x1, nf_ref, wrt_ref, br_ref, xs_ref, route_ref, cnt_ref):
    tt = x1.shape[0]
    h2 = _rms(x1, nf_ref[...])
    h_hi, h_mid, h_lo = _split3(h2)
    w_hi, w_mid, w_lo = _split3(wrt_ref[...])
    yield

    def nt(a, b):
        return lax.dot_general(a, b, (((1,), (1,)), ((), ())), preferred_element_type=F32)

    logits = (nt(w_hi, h_hi) + (nt(w_hi, h_mid) + nt(w_mid, h_hi))
              + (nt(w_hi, h_lo) + nt(w_lo, h_hi) + nt(w_mid, h_mid))) + br_ref[:, 0:1]
    yield
    e_id = _iota((LANES, tt), 0).astype(F32)
    neg = jnp.float32(-jnp.inf)
    big = jnp.float32(1 << 20)
    is_grp = (e_id >= N_EXPERTS) & (e_id < N_EXPERTS + N_GROUPS)
    gl = jnp.where(is_grp, logits, neg)
    gmax = jnp.max(gl, axis=0, keepdims=True)
    g_lo = (jnp.min(jnp.where(gl == gmax, e_id, big), axis=0, keepdims=True) - N_EXPERTS) * EXP_PER_GROUP
    p_g = 1.0 / jnp.sum(jnp.exp(gl - gmax), axis=0, keepdims=True)
    yield
    in_grp = (e_id >= g_lo) & (e_id < g_lo + EXP_PER_GROUP)
    el = jnp.where(in_grp, logits, neg)
    v1 = jnp.max(el, axis=0, keepdims=True)
    i1 = jnp.min(jnp.where(el == v1, e_id, big), axis=0, keepdims=True)
    yield
    el2 = jnp.where(e_id == i1, neg, el)
    v2 = jnp.max(el2, axis=0, keepdims=True)
    i2 = jnp.min(jnp.where(el2 == v2, e_id, big), axis=0, keepdims=True)
    yield
    e2 = jnp.exp(v2 - v1)
    w1 = p_g / (1.0 + e2)
    w2 = p_g * e2 / (1.0 + e2)
    oh1 = (e_id == i1).astype(F32)
    oh2 = (e_id == i2).astype(F32)
    a_t = oh1 + oh2
    cnt = jnp.sum(a_t, axis=1, keepdims=True)
    pages = jnp.floor((cnt + (PAGE_ROWS - 1)) * (1.0 / PAGE_ROWS))
    yield
    lower = (_iota((LANES, LANES), 0) > _iota((LANES, LANES), 1)).astype(BF16)
    offp = jnp.dot(lower, jnp.broadcast_to(pages, (LANES, LANES)).astype(BF16),
                   preferred_element_type=F32)[:, 0:1]
    upper = (_iota((tt, tt), 0) < _iota((tt, tt), 1)).astype(BF16)
    rank = jnp.dot(a_t.astype(BF16), upper, preferred_element_type=F32)
    yield
    pos = offp * PAGE_ROWS + rank
    slot1 = jnp.sum(oh1 * pos, axis=0, keepdims=True)
    slot2 = jnp.sum(oh2 * pos, axis=0, keepdims=True)
    yield
    s_id = _iota((TILE_SLOTS, tt), 0).astype(F32)
    sel = jnp.where((s_id == slot1) | (s_id == slot2), 1.0, 0.0).astype(BF16)
    for r0 in range(0, TILE_SLOTS, TILE_SLOTS // 4):
        r1 = r0 + TILE_SLOTS // 4
        xs_ref[r0:r1, :] = jnp.dot(sel[r0:r1], h_hi, preferred_element_type=F32).astype(BF16)
        yield
    r_id = _iota((LANES, tt), 0)
    rows = (jnp.where(r_id == 0, slot1, 0.0) + jnp.where(r_id == 1, slot2, 0.0)
            + jnp.where(r_id == 2, w1, 0.0) + jnp.where(r_id == 3, w2, 0.0))
    route_ref[...] = rows.T
    cnt_ref[...] = jnp.broadcast_to(cnt, (LANES, LANES)).T[0:8, :]


def _merge(x, oa, ob, oc, ga, gb, gc, wo_ref):
    s = ga * oa + gb * ob + gc * oc
    return x + jnp.dot(s.astype(BF16), wo_ref[...], preferred_element_type=F32)


def _mix_kernel(x_ref, au_ref, av_ref, qm_ref, ga_ref, gb_ref, gc_ref, ob0_ref, ob1_ref, mk_ref, mv_ref,
                ws_ref, bst_ref, wo_ref, nf_ref, wrt_ref, br_ref, xq_ref, ck_ref, cv_ref,
                sq_ref, sk_ref, sv_ref, sz_ref, sba_ref, sc_ref, sg_ref, wc_ref, alog_ref, dt_ref, ng_ref,
                x1_ref, xs_ref, route_ref, cnt_ref, xoc_ref, sob_ref, snew_ref, cnew_ref,
                oa_ref, oc_ref, x1_prev_ref):
    i = pl.program_id(0)
    tt = x_ref.shape[0]

    @pl.when(i == 0)
    def _():
        x1_prev_ref[...] = jnp.zeros_like(x1_prev_ref)

    def mix_tile():
        causal = _iota((A_CHUNK, A_CHUNK), 0) >= _iota((A_CHUNK, A_CHUNK), 1)
        for g in range(A_GROUPS):
            wsg = jnp.where(causal, ws_ref[g], 0.0).astype(BF16)
            bcol = bst_ref[:, g:g + 1]
            lo = g * A_CHUNK
            for cc in range(tt // A_CHUNK):
                r0 = cc * A_CHUNK
                sv = jnp.dot(wsg, av_ref[r0:r0 + A_CHUNK, lo:lo + A_CHUNK].astype(BF16),
                             preferred_element_type=F32) + bcol
                oa_ref[r0:r0 + A_CHUNK, lo:lo + A_CHUNK] = au_ref[r0:r0 + A_CHUNK, lo:lo + A_CHUNK] * sv
            if g % 2 == 1:
                yield
        scale = M_HEAD_DIM ** -0.5
        for h in range(M_HEADS):
            lo = h * M_HEAD_DIM
            s = _dot_nt(qm_ref[:, lo:lo + M_HEAD_DIM], mk_ref[:, lo:lo + M_HEAD_DIM]) * scale
            yield
            p = jnp.exp(s - jnp.max(s, axis=-1, keepdims=True))
            pr = p / jnp.sum(p, axis=-1, keepdims=True)
            oc_ref[:, lo:lo + M_HEAD_DIM] = _dot(pr, mv_ref[:, lo:lo + M_HEAD_DIM])
            yield
        n_tiles = pl.num_programs(0) - 1
        in_first_half = jnp.minimum(i, n_tiles - 1) < n_tiles // GDN_SEQS
        ob = jnp.where(in_first_half, ob0_ref[...], ob1_ref[...])
        x1 = _merge(x_ref[...], oa_ref[...], ob, oc_ref[...],
                    ga_ref[...], gb_ref[...], gc_ref[...], wo_ref)
        x1_ref[...] = x1
        x1_prev_ref[i % 2] = x1

    n_req_blocks = xq_ref.shape[0] // XA_REQ
    xa_base = jnp.minimum(i, n_req_blocks - 1) * XA_REQ
    _round_robin([_route_and_dispatch(x1_prev_ref[(i + 1) % 2], nf_ref, wrt_ref, br_ref, xs_ref,
                                      route_ref, cnt_ref),
                  mix_tile(),
                  _xattn1_requests(xq_ref, ck_ref, cv_ref, xoc_ref, xa_base)]
                 + [_gdn_step_request(sq_ref, sk_ref, sv_ref, sz_ref, sba_ref, sc_ref, sg_ref, wc_ref,
                                      alog_ref, dt_ref, ng_ref, sob_ref, snew_ref, cnew_ref, xa_base, r)
                    for r in range(XA_REQ)])


def _mix1_kernel(x_ref, au_ref, av_ref, ga_ref, gb_ref, gc_ref, ob_ref, oc_ref, ws0_ref, bs0_ref,
                 wo_ref, nf_ref, wrt_ref, br_ref, xs_in_ref,
                 x1_ref, xs_ref, route_ref, cnt_ref):
    del xs_in_ref
    oa = au_ref[...] * (ws0_ref[...] * av_ref[...] + bs0_ref[...])
    s = ga_ref[...] * oa + gb_ref[...] * ob_ref[...] + gc_ref[...] * oc_ref[...]
    s_hi = s.astype(BF16)
    s_lo = (s - s_hi.astype(F32)).astype(BF16)
    w = wo_ref[...]
    w_hi = w.astype(BF16)
    w_lo = (w - w_hi.astype(F32)).astype(BF16)
    x1 = x_ref[...] + (jnp.dot(s_hi, w_hi, preferred_element_type=F32)
                       + (jnp.dot(s_lo, w_hi, preferred_element_type=F32)
                          + jnp.dot(s_hi, w_lo, preferred_element_type=F32)))
    x1_ref[...] = x1
    for _ in _route_and_dispatch(x1, nf_ref, wrt_ref, br_ref, xs_ref, route_ref, cnt_ref):
        pass


def _route_out_specs(tt, x1_of, slot_of, route_of):
    return [pl.BlockSpec((tt, D_MODEL), lambda i: (x1_of(i), 0)),
            pl.BlockSpec((None, TILE_SLOTS, D_MODEL), lambda i: (slot_of(i), 0, 0)),
            pl.BlockSpec((tt, LANES), lambda i: (route_of(i), 0)),
            pl.BlockSpec((None, 8, LANES), lambda i: (route_of(i), 0, 0))]


def _mix(x2d, p, ob, mk, mv, w_s, bs_t, wo_bf16, norm_ffn, wr_t, br_col, seq,
         p_sample, ba_sample, cache_k, cache_v, state_conv, state_gdn, w_conv, alog_row, dt_row, norm_gdn):
    n_s = p_sample.shape[0]
    xq = p_sample[:, 6 * D_MODEL:7 * D_MODEL].reshape(n_s, M_HEADS, M_HEAD_DIM)
    n_req_blocks = n_s // XA_REQ
    req_blk = lambda i: jnp.minimum(i, n_req_blocks - 1)
    cache_spec = pl.BlockSpec((XA_REQ, N_MEM, M_HEADS, M_HEAD_DIM), lambda i: (req_blk(i), 0, 0, 0))
    whole_q = pl.BlockSpec((n_s, M_HEADS, M_HEAD_DIM), lambda i: (0, 0, 0))
    s_col = lambda colblk: pl.BlockSpec((n_s, D_MODEL), lambda i: (0, colblk))
    conv_spec = pl.BlockSpec((XA_REQ, B_CONV - 1, C_QKV), lambda i: (req_blk(i), 0, 0))
    state_spec = pl.BlockSpec((XA_REQ, B_HEADS, B_DK, B_DK), lambda i: (req_blk(i), 0, 0, 0))
    rows = x2d.shape[0]
    assert n_req_blocks <= rows // MIX_TILE + 1 and n_s % XA_REQ == 0
    tt = MIX_TILE
    nt = rows // tt
    per_b = seq // tt
    cur = lambda i: jnp.minimum(i, nt - 1)
    prev = lambda i: jnp.maximum(i - 1, 0)
    pblk = lambda colblk: pl.BlockSpec((tt, D_MODEL), lambda i: (cur(i), colblk))
    const = lambda shape: pl.BlockSpec(shape, lambda i: (0,) * len(shape))
    nh = nt // GDN_SEQS
    ob_blk = lambda g: pl.BlockSpec((tt, D_MODEL), lambda i: (jnp.clip(cur(i) - g * nh, 0, nh - 1), 0))
    x1, xs_all, route, cnt, xoc, sob, snew, cnew = pl.pallas_call(
        _mix_kernel,
        grid=(nt + 1,),
        in_specs=[pblk(0), pblk(0), pblk(1), pblk(6), pblk(7), pblk(8), pblk(9), ob_blk(0), ob_blk(1),
                  pl.BlockSpec((N_MEM, D_MODEL), lambda i: (cur(i) // per_b, 0)),
                  pl.BlockSpec((N_MEM, D_MODEL), lambda i: (cur(i) // per_b, 0)),
                  const((A_GROUPS, A_CHUNK, A_CHUNK)), const((A_CHUNK, LANES)),
                  const((D_MODEL, D_MODEL)), const((1, D_MODEL)),
                  const((LANES, D_MODEL)), const((LANES, LANES)),
                  whole_q, cache_spec, cache_spec,
                  s_col(2), s_col(3), s_col(4), s_col(5), const((n_s, LANES)), conv_spec, state_spec,
                  const((B_CONV, C_QKV)), const((1, LANES)), const((1, LANES)), const((1, B_DK))],
        out_specs=_route_out_specs(tt, cur, lambda i: i, prev)
        + [whole_q, const((n_s, D_MODEL)), state_spec, conv_spec],
        out_shape=[jax.ShapeDtypeStruct((rows, D_MODEL), F32),
                   jax.ShapeDtypeStruct((nt + 1, TILE_SLOTS, D_MODEL), BF16),
                   jax.ShapeDtypeStruct((rows, LANES), F32),
                   jax.ShapeDtypeStruct((nt, 8, LANES), F32),
                   jax.ShapeDtypeStruct((n_s, M_HEADS, M_HEAD_DIM), F32),
                   jax.ShapeDtypeStruct((n_s, D_MODEL), F32),
                   jax.ShapeDtypeStruct((n_s, B_HEADS, B_DK, B_DK), F32),
                   jax.ShapeDtypeStruct((n_s, B_CONV - 1, C_QKV), F32)],
        scratch_shapes=[pltpu.VMEM((tt, D_MODEL), F32), pltpu.VMEM((tt, D_MODEL), F32),
                        pltpu.VMEM((2, tt, D_MODEL), F32)],
        compiler_params=_params(1),
        name="mix",
    )(x2d, p, p, p, p, p, p, ob[0], ob[1], mk, mv, w_s, bs_t, wo_bf16, norm_ffn, wr_t, br_col,
      xq, cache_k, cache_v,
      p_sample, p_sample, p_sample, p_sample, ba_sample, state_conv, state_gdn,
      w_conv, alog_row, dt_row, norm_gdn)
    return x1, xs_all, route, cnt, xoc.reshape(n_s, D_MODEL), sob, snew, cnew


def _mix1(x2d, p, ob, oc, ws0_row, bs0_row, wo_f32, norm_ffn, wr_t, br_col, xs_all):
    rows = x2d.shape[0]
    pblk = lambda colblk: pl.BlockSpec((rows, D_MODEL), lambda i: (0, colblk))
    const = lambda shape: pl.BlockSpec(shape, lambda i: (0,) * len(shape))
    zero = lambda i: 0
    return pl.pallas_call(
        _mix1_kernel,
        grid=(1,),
        in_specs=[pblk(0), pblk(0), pblk(1), pblk(7), pblk(8), pblk(9), pblk(0), pblk(0),
                  const((1, D_MODEL)), const((1, D_MODEL)),
                  const((D_MODEL, D_MODEL)), const((1, D_MODEL)),
                  const((LANES, D_MODEL)), const((LANES, LANES)),
                  pl.BlockSpec(memory_space=pl.ANY)],
        out_specs=_route_out_specs(rows, zero, zero, zero),
        out_shape=[jax.ShapeDtypeStruct((rows, D_MODEL), F32),
                   jax.ShapeDtypeStruct(xs_all.shape, BF16),
                   jax.ShapeDtypeStruct((rows, LANES), F32),
                   jax.ShapeDtypeStruct((1, 8, LANES), F32)],
        input_output_aliases={14: 1},
        compiler_params=_params(1),
        name="mix1",
    )(x2d, p, p, p, p, p, ob, oc, ws0_row, bs0_row, wo_f32, norm_ffn, wr_t, br_col, xs_all)


def _experts_kernel(tbl_ref, se_ref, nu_ref, *refs):
    del tbl_ref
    pages = refs[:STEP_PAGES]
    wg_ref, wu_ref, wd_ref, o_ref, wgb_ref, wub_ref, wdb_ref = refs[STEP_PAGES:]
    s = pl.program_id(0)
    prev = se_ref[jnp.maximum(s - 1, 0)]

    @pl.when((s == 0) | (se_ref[s] != prev))
    def _():
        wgb_ref[...] = wg_ref[...].astype(BF16)
        wub_ref[...] = wu_ref[...].astype(BF16)
        wdb_ref[...] = wd_ref[...].astype(BF16)

    @pl.when(s < nu_ref[0])
    def _():
        x = jnp.concatenate([pg[...] for pg in pages], axis=0)
        gate = jnp.dot(x, wgb_ref[...], preferred_element_type=F32)
        up = jnp.dot(x, wub_ref[...], preferred_element_type=F32)
        act = (jax.nn.silu(gate) * up).astype(BF16)
        o_ref[...] = jnp.dot(act, wdb_ref[...], preferred_element_type=F32).astype(BF16)

    @pl.when(s >= nu_ref[0])
    def _():
        o_ref[...] = jnp.zeros_like(o_ref)


def _experts(tbl, step_e, n_used, xs_pages, w_gate, w_up, w_down, n_steps):
    def page_spec(k):
        return pl.BlockSpec((None, PAGE_ROWS, D_MODEL),
                            lambda s, tbl, se, nu: (tbl[s * STEP_PAGES + k], 0, 0))

    wspec = lambda shape: pl.BlockSpec((None,) + shape, lambda s, tbl, se, nu: (se[s], 0, 0))
    grid_spec = pltpu.PrefetchScalarGridSpec(
        num_scalar_prefetch=3,
        grid=(n_steps,),
        in_specs=[page_spec(k) for k in range(STEP_PAGES)]
        + [wspec((D_MODEL, D_EXPERT)), wspec((D_MODEL, D_EXPERT)), wspec((D_EXPERT, D_MODEL))],
        out_specs=pl.BlockSpec((STEP_ROWS, D_MODEL), lambda s, tbl, se, nu: (s, 0)),
        scratch_shapes=[pltpu.VMEM((D_MODEL, D_EXPERT), BF16), pltpu.VMEM((D_MODEL, D_EXPERT), BF16),
                        pltpu.VMEM((D_EXPERT, D_MODEL), BF16)],
    )
    return pl.pallas_call(
        _experts_kernel,
        grid_spec=grid_spec,
        out_shape=jax.ShapeDtypeStruct((n_steps * STEP_ROWS, D_MODEL), BF16),
        compiler_params=_params(1),
        name="experts",
    )(tbl, step_e, n_used, *([xs_pages] * STEP_PAGES), w_gate, w_up, w_down)


def _combine_kernel(inv_ref, *refs):
    del inv_ref
    pages = refs[:TILE_PAGES]
    x1_ref, route_ref, nfin_ref, y_ref = refs[TILE_PAGES:]
    out_loc = jnp.concatenate([pg[...] for pg in pages], axis=0)
    tt = x1_ref.shape[0]
    route = route_ref[...]
    s_id = _iota((tt, TILE_SLOTS), 1).astype(F32)
    sel = (jnp.where(s_id == route[:, 0:1], route[:, 2:3], 0.0)
           + jnp.where(s_id == route[:, 1:2], route[:, 3:4], 0.0)).astype(BF16)
    moe = jnp.dot(sel, out_loc, preferred_element_type=F32)
    y_ref[...] = _rms(x1_ref[...] + moe, nfin_ref[...])


def _combine(inv, out_pages, x1, route, norm_final, tt):
    rows = x1.shape[0]

    def page_spec(k):
        return pl.BlockSpec((None, PAGE_ROWS, D_MODEL), lambda i, inv: (inv[i * TILE_PAGES + k], 0, 0))

    grid_spec = pltpu.PrefetchScalarGridSpec(
        num_scalar_prefetch=1,
        grid=(rows // tt,),
        in_specs=[page_spec(k) for k in range(TILE_PAGES)]
        + [pl.BlockSpec((tt, D_MODEL), lambda i, inv: (i, 0)),
           pl.BlockSpec((tt, LANES), lambda i, inv: (i, 0)),
           pl.BlockSpec((1, D_MODEL), lambda i, inv: (0, 0))],
        out_specs=pl.BlockSpec((tt, D_MODEL), lambda i, inv: (i, 0)),
    )
    return pl.pallas_call(
        _combine_kernel,
        grid_spec=grid_spec,
        out_shape=jax.ShapeDtypeStruct((rows, D_MODEL), F32),
        compiler_params=_params(1),
        name="combine",
    )(inv, *([out_pages] * TILE_PAGES), x1, route, norm_final)


def _page_tables(cnt, n_steps):
    n_tiles = cnt.shape[0]
    pg = (cnt + (PAGE_ROWS - 1)) // PAGE_ROWS
    lend = jnp.cumsum(pg, axis=1)
    loff = lend - pg
    cum_t = jnp.cumsum(pg, axis=0)
    pref = cum_t - pg
    tot = cum_t[-1]
    totp = ((tot + (STEP_PAGES - 1)) // STEP_PAGES) * STEP_PAGES
    gend = jnp.cumsum(totp)
    gstart = gend - totp
    sel = lambda onehot, vals: jnp.dot(onehot, vals.astype(F32), precision=lax.Precision.HIGHEST)
    gp = jnp.arange(n_steps * STEP_PAGES, dtype=jnp.int32)
    in_e = (gp[:, None] >= gstart[None, :]) & (gp[:, None] < gend[None, :])
    oh_e = in_e.astype(F32)
    r = gp.astype(F32) - sel(oh_e, gstart)
    valid = jnp.any(in_e, axis=1) & (r < sel(oh_e, tot))
    cum_col = sel(oh_e, cum_t.T)
    pref_col = sel(oh_e, pref.T)
    loff_col = sel(oh_e, loff.T)
    in_i = (r[:, None] >= pref_col) & (r[:, None] < cum_col)
    tile_base = (jnp.arange(n_tiles, dtype=jnp.int32) * TILE_PAGES).astype(F32)
    src = jnp.sum(jnp.where(in_i, tile_base[None, :] + loff_col + (r[:, None] - pref_col), 0.0), axis=1)
    tbl = jnp.where(valid, src, 0.0).astype(jnp.int32)
    in_step = in_e[::STEP_PAGES]
    e_ids = jnp.arange(N_EXPERTS, dtype=jnp.int32)
    step_e = jnp.where(jnp.any(in_step, axis=1), jnp.sum(jnp.where(in_step, e_ids[None, :], 0), axis=1),
                       N_EXPERTS - 1).astype(jnp.int32)
    n_used = (gend[-1] // STEP_PAGES).astype(jnp.int32).reshape(1)
    lp = jnp.arange(TILE_PAGES, dtype=jnp.int32)[None, :, None]
    in_l = (lp >= loff[:, None, :]) & (lp < lend[:, None, :])
    gpos = jnp.sum(jnp.where(in_l, gstart[None, None, :] + pref[:, None, :] + lp - loff[:, None, :], 0), axis=2)
    zero_page = (n_steps - 1) * STEP_PAGES
    inv = jnp.where(jnp.any(in_l, axis=2), gpos, zero_page).astype(jnp.int32).reshape(-1)
    return tbl, step_e, n_used, inv


def kernel(x_prompt, x_sample, mem_prompt, cache_mem_k, cache_mem_v, state_gdn, state_conv,
           norm_mix, w_in, b_gate, w_s, b_s, norm_a_v, w_conv, a_log, dt_bias, norm_gdn_out,
           norm_mem, w_mem_kv, w_o, norm_ffn, w_router_group, b_router_group, w_router_expert,
           b_router_expert, w_exp_gate, w_exp_up, w_exp_down, norm_final):
    depth = norm_mix.shape[0]
    assert depth == 1
    bsz, seq, _ = x_prompt.shape
    n_s = x_sample.shape[0]
    assert x_sample.shape[1] == 1 and seq % MIX_TILE == 0 and n_s % 8 == 0 and n_s <= MIX_TILE
    assert bsz % GDN_SEQS == 0
    l = 0
    row = lambda v: v.reshape(1, -1)

    wi = w_in[l]
    o_z = 2 * D_MODEL + C_QKV
    o_beta = o_z + D_MODEL
    o_qm = o_beta + 2 * B_HEADS
    o_gate = o_qm + D_MODEL
    w_perm = jnp.concatenate([wi[:, :o_beta], wi[:, o_qm:]], axis=1)
    w_main = w_perm.astype(BF16)
    w_main_lo = (w_perm - w_main.astype(F32)).astype(BF16)
    w_ba_f32 = jnp.pad(wi[:, o_beta:o_qm], ((0, 0), (0, LANES - 2 * B_HEADS)))
    w_ba = w_ba_f32.astype(BF16)
    w_ba_lo = (w_ba_f32 - w_ba.astype(F32)).astype(BF16)
    pad_heads = lambda v: jnp.pad(v.reshape(1, B_HEADS), ((0, 0), (B_HEADS, LANES - 2 * B_HEADS)))
    alog_row = pad_heads(a_log[l])
    dt_row = pad_heads(dt_bias[l])
    w_kv = w_mem_kv[l].astype(BF16)
    wo = w_o[l].astype(BF16)
    wr_t = jnp.pad(jnp.concatenate([w_router_expert[l], w_router_group[l]], axis=1).T,
                   ((0, LANES - N_EXPERTS - N_GROUPS), (0, 0)))
    br_col = jnp.pad(jnp.concatenate([b_router_expert[l], b_router_group[l]]).reshape(-1, 1),
                     ((0, LANES - N_EXPERTS - N_GROUPS), (0, LANES - 1)))
    bs_t = jnp.pad(b_s[l].T, ((0, 0), (0, LANES - A_GROUPS)))
    ws0_row = jnp.repeat(w_s[l][:, 0, 0], A_CHUNK).reshape(1, D_MODEL)
    bs0_row = jnp.repeat(b_s[l][:, 0], A_CHUNK).reshape(1, D_MODEL)

    xp = x_prompt.reshape(bsz * seq, D_MODEL)
    xs_ = x_sample.reshape(n_s, D_MODEL)
    n_tiles_p = (bsz * seq) // MIX_TILE
    n_tiles = n_tiles_p + 1

    p_s, ba_s = _proj2(xs_, row(norm_mix[l]), w_main, w_main_lo, w_ba, w_ba_lo, row(norm_a_v[l]),
                       row(b_gate[l]))
    mk, mv = _memkv(mem_prompt.reshape(bsz * N_MEM, D_MODEL), row(norm_mem[l]), w_kv)
    p_p, ba_p = _proj(xp, row(norm_mix[l]), w_main, w_ba, row(norm_a_v[l]), row(b_gate[l]),
                      tm=min(PROJ_ROWS, bsz * seq))
    ob_p, s_p = _gdn(p_p, ba_p, w_conv[l], alog_row, dt_row, row(norm_gdn_out[l]), bsz, seq)
    x1_p, xs_all, route_p, cnt_p, oc_s, ob_s, s_s, c_s = _mix(
        xp, p_p, ob_p, mk, mv, w_s[l], bs_t, wo, row(norm_ffn[l]), wr_t, br_col, seq,
        p_s, ba_s, cache_mem_k[l], cache_mem_v[l], state_conv[l], state_gdn[l], w_conv[l], alog_row, dt_row,
        row(norm_gdn_out[l]))
    x1_s, xs_all, route_s, cnt_s = _mix1(xs_, p_s, ob_s, oc_s, ws0_row, bs0_row, w_o[l], row(norm_ffn[l]),
                                         wr_t, br_col, xs_all)
    cnt = jnp.concatenate([cnt_s[:, 0, :N_EXPERTS], cnt_p[:, 0, :N_EXPERTS]], axis=0).astype(jnp.int32)
    max_pages = n_tiles_p * TILE_PAGES + (2 * n_s) // PAGE_ROWS + N_EXPERTS
    n_steps = (max_pages + N_EXPERTS * (STEP_PAGES - 1)) // STEP_PAGES + 1
    tbl, step_e, n_used, inv = _page_tables(cnt, n_steps)
    ne = N_GROUPS * EXP_PER_GROUP
    out_sorted = _experts(tbl, step_e, n_used, xs_all.reshape(n_tiles * TILE_PAGES, PAGE_ROWS, D_MODEL),
                          w_exp_gate[l].reshape(ne, D_MODEL, D_EXPERT),
                          w_exp_up[l].reshape(ne, D_MODEL, D_EXPERT),
                          w_exp_down[l].reshape(ne, D_EXPERT, D_MODEL), n_steps)
    out_pages = out_sorted.reshape(n_steps * STEP_PAGES, PAGE_ROWS, D_MODEL)
    y_s = _combine(inv[:TILE_PAGES], out_pages, x1_s, route_s, row(norm_final), n_s)
    y_p = _combine(inv[TILE_PAGES:], out_pages, x1_p, route_p, row(norm_final), MIX_TILE)

    conv_tail = p_p.reshape(bsz, seq, PROJ_COLS)[:, seq - (B_CONV - 1):, 2 * D_MODEL:2 * D_MODEL + C_QKV]
    return (y_p.reshape(bsz, seq, D_MODEL),
            y_s.reshape(n_s, 1, D_MODEL),
            mk.reshape(1, bsz, N_MEM, M_HEADS, M_HEAD_DIM),
            mv.reshape(1, bsz, N_MEM, M_HEADS, M_HEAD_DIM),
            jnp.concatenate(s_p, axis=0)[None],
            conv_tail[None],
            s_s[None],
            c_s[None],
            p_s[:, D_MODEL:2 * D_MODEL].reshape(1, n_s, 1, D_MODEL))
```

```python
import functools
import math

import jax
import jax.numpy as jnp
from jax import lax
from jax.experimental import pallas as pl
from jax.experimental.pallas import tpu as pltpu

F32 = jnp.float32
BF16 = jnp.bfloat16

D_MODEL = 1024
A_GROUPS = 8
A_CHUNK = 128
B_HEADS = 8
B_DK = 128
C_QKV = 3 * D_MODEL
B_CONV = 4
N_MEM = 256
M_HEADS = 4
M_HEAD_DIM = 256
N_GROUPS = 4
EXP_PER_GROUP = 8
N_EXPERTS = 32
D_EXPERT = 512
EPS = 1e-6

LANES = 128
GDN_CHUNK = 128
GDN_SEQS = 2
MIX_TILE = 256
PAGE_ROWS = 16
TILE_PAGES = 64
TILE_SLOTS = PAGE_ROWS * TILE_PAGES
STEP_PAGES = 32
STEP_ROWS = PAGE_ROWS * STEP_PAGES
PROJ_COLS = 10 * D_MODEL
PROJ_CHUNK = 256
PROJ_ROWS = 2048
PROJ_PIECE_ROWS = 1024
VMEM_LIMIT = 56 * 1024 * 1024


def _params(n_grid):
    return pltpu.CompilerParams(dimension_semantics=("arbitrary",) * n_grid,
                                vmem_limit_bytes=VMEM_LIMIT)


def _rms(x, g):
    return x * lax.rsqrt(jnp.mean(x * x, axis=-1, keepdims=True) + EPS) * g


def _dot(a, b):
    return jnp.dot(a.astype(BF16), b.astype(BF16), preferred_element_type=F32)


def _dot_nt(a, b):
    return lax.dot_general(a.astype(BF16), b.astype(BF16), (((1,), (1,)), ((), ())),
                           preferred_element_type=F32)


def _dot_tn(a, b):
    return lax.dot_general(a.astype(BF16), b.astype(BF16), (((0,), (0,)), ((), ())),
                           preferred_element_type=F32)


def _split3(x):
    hi = x.astype(BF16)
    r = x - hi.astype(F32)
    mid = r.astype(BF16)
    lo = (r - mid.astype(F32)).astype(BF16)
    return hi, mid, lo


def _iota(shape, axis):
    return lax.broadcasted_iota(jnp.int32, shape, axis)


def _memkv_kernel(mem_ref, g_ref, w_ref, k_ref, v_ref):
    h = _rms(mem_ref[...], g_ref[...]).astype(BF16)
    kv = jnp.dot(h, w_ref[...], preferred_element_type=F32)
    k_ref[...] = kv[:, :D_MODEL]
    v_ref[...] = kv[:, D_MODEL:]


def _memkv(mem2d, norm_mem, w_kv_bf16):
    rows = mem2d.shape[0]
    nb = rows // N_MEM
    return pl.pallas_call(
        _memkv_kernel,
        grid=(nb,),
        in_specs=[pl.BlockSpec((N_MEM, D_MODEL), lambda i: (i, 0)),
                  pl.BlockSpec((1, D_MODEL), lambda i: (0, 0)),
                  pl.BlockSpec((D_MODEL, 2 * D_MODEL), lambda i: (0, 0))],
        out_specs=[pl.BlockSpec((N_MEM, D_MODEL), lambda i: (i, 0)),
                   pl.BlockSpec((N_MEM, D_MODEL), lambda i: (i, 0))],
        out_shape=[jax.ShapeDtypeStruct((rows, D_MODEL), F32)] * 2,
        compiler_params=_params(1),
        name="memkv",
    )(mem2d, norm_mem, w_kv_bf16)


def _proj_kernel(x_ref, nm_ref, w_ref, wba_ref, nav_ref, bg_ref, p_ref, ba_ref, hn_ref):
    j = pl.program_id(1)

    @pl.when(j == 0)
    def _():
        hn = _rms(x_ref[...], nm_ref[...]).astype(BF16)
        hn_ref[...] = hn
        ba_ref[...] = jnp.dot(hn, wba_ref[...], preferred_element_type=F32)

    tm = x_ref.shape[0]
    rows = [(r0, min(tm, r0 + PROJ_PIECE_ROWS)) for r0 in range(0, tm, PROJ_PIECE_ROWS)]
    pieces = [(r, c0) for r in rows for c0 in range(0, D_MODEL, PROJ_CHUNK)]

    def chunks(epilogue):
        def matmul(piece):
            (r0, r1), c0 = piece
            return jnp.dot(hn_ref[r0:r1, :], w_ref[:, c0:c0 + PROJ_CHUNK], preferred_element_type=F32)

        def finish(piece, acc):
            (r0, r1), c0 = piece
            p_ref[r0:r1, c0:c0 + PROJ_CHUNK] = epilogue(acc, piece)

        acc = matmul(pieces[0])
        for n in range(1, len(pieces)):
            nxt = matmul(pieces[n])
            finish(pieces[n - 1], acc)
            acc = nxt
        finish(pieces[-1], acc)

    @pl.when(j == 0)
    def _():
        chunks(lambda acc, piece: jax.nn.gelu(acc))

    @pl.when(j == 1)
    def _():
        ssq = {r: [] for r in rows}

        def gelu_ssq(acc, piece):
            a = jax.nn.gelu(acc)
            ssq[piece[0]].append(jnp.sum(a * a, axis=-1, keepdims=True))
            return a

        chunks(gelu_ssq)
        for r0, r1 in rows:
            inv = lax.rsqrt(sum(ssq[(r0, r1)]) * (1.0 / D_MODEL) + EPS)
            p_ref[r0:r1, :] = p_ref[r0:r1, :] * inv * nav_ref[...]

    @pl.when((j >= 2) & (j <= 6))
    def _():
        chunks(lambda acc, piece: acc)

    @pl.when(j >= 7)
    def _():
        chunks(lambda acc, piece: jax.nn.sigmoid(acc + bg_ref[:, piece[1]:piece[1] + PROJ_CHUNK]))


def _proj(x2d, norm_mix, w_bf16, wba_bf16, norm_a_v, b_gate, tm):
    rows = x2d.shape[0]
    ncol = PROJ_COLS // D_MODEL
    return pl.pallas_call(
        _proj_kernel,
        grid=(rows // tm, ncol),
        in_specs=[pl.BlockSpec((tm, D_MODEL), lambda i, j: (i, 0)),
                  pl.BlockSpec((1, D_MODEL), lambda i, j: (0, 0)),
                  pl.BlockSpec((D_MODEL, D_MODEL), lambda i, j: (0, j)),
                  pl.BlockSpec((D_MODEL, LANES), lambda i, j: (0, 0)),
                  pl.BlockSpec((1, D_MODEL), lambda i, j: (0, 0)),
                  pl.BlockSpec((1, D_MODEL), lambda i, j: (0, jnp.maximum(j - 7, 0)))],
        out_specs=[pl.BlockSpec((tm, D_MODEL), lambda i, j: (i, j)),
                   pl.BlockSpec((tm, LANES), lambda i, j: (i, 0))],
        out_shape=[jax.ShapeDtypeStruct((rows, PROJ_COLS), F32),
                   jax.ShapeDtypeStruct((rows, LANES), F32)],
        scratch_shapes=[pltpu.VMEM((tm, D_MODEL), BF16)],
        compiler_params=_params(2),
        name="proj",
    )(x2d, norm_mix, w_bf16, wba_bf16, norm_a_v, b_gate)


def _proj2_kernel(x_ref, nm_ref, w_ref, wba_ref, nav_ref, bg_ref, p_ref, ba_ref, hh_ref, hl_ref):
    j = pl.program_id(1)

    def dot2(w_f32_ref):
        w = w_f32_ref[...]
        w_hi = w.astype(BF16)
        w_lo = (w - w_hi.astype(F32)).astype(BF16)
        return (jnp.dot(hh_ref[...], w_hi, preferred_element_type=F32)
                + (jnp.dot(hl_ref[...], w_hi, preferred_element_type=F32)
                   + jnp.dot(hh_ref[...], w_lo, preferred_element_type=F32)))

    @pl.when(j == 0)
    def _():
        h = _rms(x_ref[...], nm_ref[...])
        hi = h.astype(BF16)
        hh_ref[...] = hi
        hl_ref[...] = (h - hi.astype(F32)).astype(BF16)
        ba_ref[...] = dot2(wba_ref)

    acc = dot2(w_ref)

    @pl.when(j == 0)
    def _():
        p_ref[...] = jax.nn.gelu(acc)

    @pl.when(j == 1)
    def _():
        p_ref[...] = _rms(jax.nn.gelu(acc), nav_ref[...])

    @pl.when((j >= 2) & (j <= 6))
    def _():
        p_ref[...] = acc

    @pl.when(j >= 7)
    def _():
        p_ref[...] = jax.nn.sigmoid(acc + bg_ref[...])


def _proj2(x2d, norm_mix, w_f32, wba_f32, norm_a_v, b_gate):
    rows = x2d.shape[0]
    ncol = PROJ_COLS // D_MODEL
    wcol = pl.BlockSpec((D_MODEL, D_MODEL), lambda i, j: (0, j))
    wba = pl.BlockSpec((D_MODEL, LANES), lambda i, j: (0, 0))
    vec = pl.BlockSpec((1, D_MODEL), lambda i, j: (0, 0))
    return pl.pallas_call(
        _proj2_kernel,
        grid=(1, ncol),
        in_specs=[pl.BlockSpec((rows, D_MODEL), lambda i, j: (0, 0)), vec, wcol, wba, vec,
                  pl.BlockSpec((1, D_MODEL), lambda i, j: (0, jnp.maximum(j - 7, 0)))],
        out_specs=[pl.BlockSpec((rows, D_MODEL), lambda i, j: (0, j)),
                   pl.BlockSpec((rows, LANES), lambda i, j: (0, 0))],
        out_shape=[jax.ShapeDtypeStruct((rows, PROJ_COLS), F32),
                   jax.ShapeDtypeStruct((rows, LANES), F32)],
        scratch_shapes=[pltpu.VMEM((rows, D_MODEL), BF16), pltpu.VMEM((rows, D_MODEL), BF16)],
        compiler_params=_params(2),
        name="proj2",
    )(x2d, norm_mix, w_f32, wba_f32, norm_a_v, b_gate)


def _gate_terms(ba, alog_row, dt_row):
    beta = jax.nn.sigmoid(ba)
    g = -jnp.exp(alog_row) * jax.nn.softplus(ba + dt_row)
    return beta, g


class _Lazy:
    def __init__(self, make):
        self._make = make

    def __getitem__(self, index):
        return self._make(index)


def _round_robin(tasks):
    done = object()
    while tasks:
        tasks = [t for t in tasks if next(t, done) is not done]


def _gdn_kernel(n_c, *refs):
    per = 5
    q_refs, k_refs, v_refs, ba_refs, z_refs = (
        [refs[per * g + j] for g in range(GDN_SEQS)] for j in range(per))
    wc_ref, alog_ref, dt_ref, ng_ref = refs[per * GDN_SEQS:per * GDN_SEQS + 4]
    ob_refs = refs[per * GDN_SEQS + 4:per * GDN_SEQS + 4 + GDN_SEQS]
    s_out_refs = refs[per * GDN_SEQS + 4 + GDN_SEQS:per * GDN_SEQS + 4 + 2 * GDN_SEQS]
    s_ref, ext_ref, prep_ref, gate_ref = refs[per * GDN_SEQS + 4 + 2 * GDN_SEQS:]
    s = pl.program_id(0)
    C = GDN_CHUNK
    par = s % 2
    heads = range(B_HEADS)

    @pl.when(s == 0)
    def _():
        s_ref[...] = jnp.zeros_like(s_ref)
        ext_ref[:, 0:8, :] = jnp.zeros((GDN_SEQS, 8, C_QKV), F32)
        prep_ref[...] = jnp.zeros_like(prep_ref)
        gate_ref[...] = jnp.zeros_like(gate_ref)

    row = _iota((C, C), 0)
    col = _iota((C, C), 1)
    incl = row >= col
    pace = []

    def mark(results):
        pace.append(results[-1][0:1, 0:B_DK] * 0.0)

    def prepare(g):
        q_ref, k_ref, v_ref, ba_ref = q_refs[g], k_refs[g], v_refs[g], ba_refs[g]
        keep = (s % n_c != 0).astype(F32)
        ext_ref[g, 0:8, :] = ext_ref[g, 0:8, :] * keep
        ext_ref[g, 8:8 + C, 0:D_MODEL] = q_ref[...]
        ext_ref[g, 8:8 + C, D_MODEL:2 * D_MODEL] = k_ref[...]
        ext_ref[g, 8:8 + C, 2 * D_MODEL:3 * D_MODEL] = v_ref[...]
        beta_all, g_all = _gate_terms(ba_ref[...], alog_ref[...], dt_ref[...])
        g_hi, g_mid, g_lo = _split3(g_all)
        tri = incl.astype(BF16)
        gcum = (jnp.dot(tri, g_hi, preferred_element_type=F32)
                + jnp.dot(tri, g_mid, preferred_element_type=F32)
                + jnp.dot(tri, g_lo, preferred_element_type=F32))
        gate_ref[par, g, 0] = beta_all
        gate_ref[par, g, 1] = gcum
        gate_ref[par, g, 2] = gcum.T
        yield

        def conv_silu(lo):
            pace_row = pace[-1] if pace else 0.0
            acc = (wc_ref[3:4, lo:lo + B_DK] + pace_row) * ext_ref[g, 8:8 + C, lo:lo + B_DK]
            for j in range(1, B_CONV):
                acc = acc + ((wc_ref[3 - j:4 - j, lo:lo + B_DK] + pace_row)
                             * ext_ref[g, 8 - j:8 - j + C, lo:lo + B_DK])
            return jax.nn.silu(acc)

        def l2n(t):
            return t * lax.rsqrt(jnp.sum(t * t, axis=-1, keepdims=True) + EPS)

        for h in heads:
            lo = h * B_DK
            prep_ref[par, g, :, lo:lo + B_DK] = l2n(conv_silu(lo)) * (B_DK ** -0.5)
            yield
            prep_ref[par, g, :, D_MODEL + lo:D_MODEL + lo + B_DK] = l2n(conv_silu(D_MODEL + lo))
            yield
            prep_ref[par, g, :, 2 * D_MODEL + lo:2 * D_MODEL + lo + B_DK] = conv_silu(2 * D_MODEL + lo)
            yield
        ext_ref[g, 0:8, :] = ext_ref[g, C:C + 8, :]

    def chain():
        heads = range(GDN_SEQS * B_HEADS)
        half = [u // B_HEADS for u in heads]
        head = [u % B_HEADS for u in heads]
        old = 1 - par
        strict = row > col
        eye = (row == col).astype(F32)
        same_blk = [(row // b) == (col // b) for b in (8, 16, 32, 64, 128)]
        beta_all = [gate_ref[old, g, 0] for g in range(GDN_SEQS)]
        gcum = [gate_ref[old, g, 1] for g in range(GDN_SEQS)]
        gcum_t = [gate_ref[old, g, 2] for g in range(GDN_SEQS)]
        lanes = lambda part, u: slice(part * D_MODEL + head[u] * B_DK, part * D_MODEL + (head[u] + 1) * B_DK)
        q = _Lazy(lambda u: prep_ref[old, half[u], :, lanes(0, u)])
        k = _Lazy(lambda u: prep_ref[old, half[u], :, lanes(1, u)])
        v = _Lazy(lambda u: prep_ref[old, half[u], :, lanes(2, u)])
        bcol = [beta_all[half[u]][:, head[u]:head[u] + 1] for u in heads]
        gcol = [gcum[half[u]][:, 8 + head[u]:9 + head[u]] for u in heads]
        glast = [gcum[half[u]][C - 1:C, 8 + head[u]:9 + head[u]] for u in heads]
        eg = [jnp.exp(gcol[h]) for h in heads]
        kb = _Lazy(lambda u: k[u] * bcol[u])
        yield
        a1 = [_dot_nt(jnp.concatenate([kb[h], q[h]], axis=0), k[h]) for h in heads]
        mark(a1)
        yield
        m, attn = [], []
        for u in heads:
            grow = gcum_t[half[u]][8 + head[u]:9 + head[u], :]
            dec = jnp.where(incl, jnp.exp(jnp.where(incl, gcol[u] - grow, 0.0)), 0.0)
            m.append(jnp.where(strict, a1[u][:C] * dec, 0.0))
            attn.append(a1[u][C:] * dec)
        m0 = [jnp.where(same_blk[0], m[h], 0.0) for h in heads]
        x = [eye - m0[h] for h in heads]
        pw = [_dot(m0[h], m0[h]) for h in heads]
        mark(pw)
        yield
        x = [x[h] + _dot(x[h], pw[h]) for h in heads]
        mark(x)
        yield
        pw = [_dot(pw[h], pw[h]) for h in heads]
        mark(pw)
        yield
        x = [x[h] + _dot(x[h], pw[h]) for h in heads]
        mark(x)
        yield
        for lvl in range(1, len(same_blk)):
            in_lvl = same_blk[lvl] & jnp.logical_not(same_blk[lvl - 1])
            xb = [x[h].astype(BF16) for h in heads]
            t = [_dot(xb[h], jnp.where(in_lvl, m[h], 0.0)) for h in heads]
            mark(t)
            yield
            x = [x[h] - _dot(t[h], xb[h]) for h in heads]
            mark(x)
            yield
        rhs = [jnp.concatenate([v[h] * bcol[h], kb[h] * eg[h]], axis=1) for h in heads]
        sol = [_dot(x[h], rhs[h]) for h in heads]
        mark(sol)
        yield
        keep = ((s + n_c - 1) % n_c != 0).astype(F32)
        s_old = [s_ref[half[u], head[u]] * keep for u in heads]
        b1 = [_dot(jnp.concatenate([sol[h][:, B_DK:], q[h] * eg[h]], axis=0), s_old[h]) for h in heads]
        mark(b1)
        yield
        u = [sol[h][:, :B_DK] - b1[h][:C] for h in heads]
        o = [b1[h][C:] + _dot(attn[h], u[h]) for h in heads]
        kd = [k[h] * jnp.exp(glast[h] - gcol[h]) for h in heads]
        mark(o)
        yield
        s_new = [s_old[h] * jnp.exp(glast[h]) + _dot_tn(kd[h], u[h]) for h in heads]
        mark(s_new)
        yield
        for u in heads:
            g, h = half[u], head[u]
            s_ref[g, h] = s_new[u]
            s_out_refs[g][h] = s_new[u]
            zh = z_refs[g][:, h * B_DK:(h + 1) * B_DK]
            ob_refs[g][:, h * B_DK:(h + 1) * B_DK] = _rms(o[u], ng_ref[...]) * jax.nn.silu(zh)
            yield

    done = object()
    chain_task = chain()
    prep_tasks = [prepare(g) for g in range(GDN_SEQS)]
    for _ in range(3):
        next(chain_task)
    alive = True
    while alive:
        alive = False
        for task in prep_tasks + [chain_task] + prep_tasks + prep_tasks + [chain_task]:
            alive = (next(task, done) is not done) or alive


def _gdn(p, ba, w_conv, alog_row, dt_row, norm_gdn, bsz, seq):
    nc = seq // GDN_CHUNK
    total = (bsz // GDN_SEQS) * nc
    cur = lambda s: jnp.minimum(s, total - 1)
    prev = lambda s: jnp.maximum(s - 1, 0)
    small = lambda shape: pl.BlockSpec(shape, lambda s: (0, 0))

    def half_specs(g):
        blk = lambda colblk: pl.BlockSpec((GDN_CHUNK, D_MODEL), lambda s: (g * total + cur(s), colblk))
        return [blk(2), blk(3), blk(4),
                pl.BlockSpec((GDN_CHUNK, LANES), lambda s: (g * total + cur(s), 0)),
                pl.BlockSpec((GDN_CHUNK, D_MODEL), lambda s: (g * total + prev(s), 5))]

    halves = range(GDN_SEQS)
    outs = pl.pallas_call(
        functools.partial(_gdn_kernel, nc),
        grid=(total + 1,),
        in_specs=[spec for g in halves for spec in half_specs(g)]
        + [small((B_CONV, C_QKV)), small((1, LANES)), small((1, LANES)), small((1, B_DK))],
        out_specs=[pl.BlockSpec((GDN_CHUNK, D_MODEL), lambda s: (prev(s), 0)) for g in halves]
        + [pl.BlockSpec((None, B_HEADS, B_DK, B_DK), lambda s: (prev(s) // nc, 0, 0, 0)) for g in halves],
        out_shape=[jax.ShapeDtypeStruct((total * GDN_CHUNK, D_MODEL), F32) for g in halves]
        + [jax.ShapeDtypeStruct((bsz // GDN_SEQS, B_HEADS, B_DK, B_DK), F32) for g in halves],
        scratch_shapes=[pltpu.VMEM((GDN_SEQS, B_HEADS, B_DK, B_DK), F32),
                        pltpu.VMEM((GDN_SEQS, 8 + GDN_CHUNK, C_QKV), F32),
                        pltpu.VMEM((2, GDN_SEQS, GDN_CHUNK, C_QKV), F32),
                        pltpu.VMEM((2, GDN_SEQS, 3, GDN_CHUNK, LANES), F32)],
        compiler_params=_params(1),
        name="gdn",
    )(*([p, p, p, ba, p] * GDN_SEQS), w_conv, alog_row, dt_row, norm_gdn)
    return outs[:GDN_SEQS], outs[GDN_SEQS:]


def _gdn_step_request(q_ref, k_ref, v_ref, z_ref, ba_ref, sc_ref, sg_ref, wc_ref, alog_ref, dt_ref,
                      ng_ref, ob_ref, snew_ref, cnew_ref, base, r):
    def l2n(t):
        return t * lax.rsqrt(jnp.sum(t * t, axis=-1, keepdims=True) + EPS)

    def column(t):
        return jnp.broadcast_to(t, (B_DK, B_DK)).T

    row = pl.ds(base + r, 1)
    beta_all, g_all = _gate_terms(ba_ref[row, :], alog_ref[...], dt_ref[...])
    eg_all = jnp.exp(g_all)
    pre = jnp.concatenate([q_ref[row, :], k_ref[row, :], v_ref[row, :]], axis=1)
    conv = wc_ref[3:4, :] * pre
    for j in range(B_CONV - 1):
        conv = conv + wc_ref[j:j + 1, :] * sc_ref[r, j:j + 1, :]
    cnew_ref[r, 0:1, :] = sc_ref[r, 1:2, :]
    cnew_ref[r, 1:2, :] = sc_ref[r, 2:3, :]
    cnew_ref[r, 2:3, :] = pre
    qkv = jax.nn.silu(conv)
    gate = jax.nn.silu(z_ref[row, :])
    out = []
    yield
    for h in range(B_HEADS):
        lo = h * B_DK
        qcol = column(l2n(qkv[:, lo:lo + B_DK]) * (B_DK ** -0.5))
        kcol = column(l2n(qkv[:, D_MODEL + lo:D_MODEL + lo + B_DK]))
        vh = qkv[:, 2 * D_MODEL + lo:2 * D_MODEL + lo + B_DK]
        s_old = sg_ref[r, h]
        beta = beta_all[:, h:h + 1]
        eg = eg_all[:, 8 + h:9 + h]
        ks = jnp.sum(s_old * kcol, axis=0, keepdims=True)
        u = beta * (vh - eg * ks)
        s_new = s_old * eg + kcol * u
        snew_ref[r, h] = s_new
        yield
        o = jnp.sum(s_new * qcol, axis=0, keepdims=True)
        out.append(_rms(o, ng_ref[...]) * gate[:, lo:lo + B_DK])
        yield
    ob_ref[row, :] = jnp.concatenate(out, axis=1)


XA_REQ = 2


def _xattn1_requests(q_ref, ck_ref, cv_ref, oc_ref, base):
    scale = M_HEAD_DIM ** -0.5
    pairs = (N_MEM // 2, 2 * M_HEADS, M_HEAD_DIM)
    fold = lambda t, op: op(t[0:M_HEADS], t[M_HEADS:2 * M_HEADS])
    twice = lambda t: jnp.concatenate([t, t], axis=0)
    for r in range(XA_REQ):
        q = twice(q_ref[base + r])
        s = jnp.sum(ck_ref[r].reshape(pairs) * q[None], axis=-1, keepdims=True) * scale
        yield
        mx = twice(fold(jnp.max(s, axis=0), jnp.maximum))
        p = jnp.exp(s - mx[None])
        den = twice(fold(jnp.sum(p, axis=0), jnp.add))
        pr = p / den[None]
        yield
        oc_ref[base + r] = fold(jnp.sum(pr * cv_ref[r].reshape(pairs), axis=0), jnp.add)
        yield


def _route_and_dispatch(x1, nf_ref, wrt_ref, br_ref, xs_ref, route_ref, cnt_ref):
    tt = x1.shape[0]
    h2 = _rms(x1, nf_ref[...])
    h_hi, h_mid, h_lo = _split3(h2)
    w_hi, w_mid, w_lo = _split3(wrt_ref[...])
    yield

    def nt(a, b):
        return lax.dot_general(a, b, (((1,), (1,)), ((), ())), preferred_element_type=F32)

    logits = (nt(w_hi, h_hi) + (nt(w_hi, h_mid) + nt(w_mid, h_hi))
              + (nt(w_hi, h_lo) + nt(w_lo, h_hi) + nt(w_mid, h_mid))) + br_ref[:, 0:1]
    yield
    e_id = _iota((LANES, tt), 0).astype(F32)
    neg = jnp.float32(-jnp.inf)
    big = jnp.float32(1 << 20)
    is_grp = (e_id >= N_EXPERTS) & (e_id < N_EXPERTS + N_GROUPS)
    gl = jnp.where(is_grp, logits, neg)
    gmax = jnp.max(gl, axis=0, keepdims=True)
    g_lo = (jnp.min(jnp.where(gl == gmax, e_id, big), axis=0, keepdims=True) - N_EXPERTS) * EXP_PER_GROUP
    p_g = 1.0 / jnp.sum(jnp.exp(gl - gmax), axis=0, keepdims=True)
    yield
    in_grp = (e_id >= g_lo) & (e_id < g_lo + EXP_PER_GROUP)
    el = jnp.where(in_grp, logits, neg)
    v1 = jnp.max(el, axis=0, keepdims=True)
    i1 = jnp.min(jnp.where(el == v1, e_id, big), axis=0, keepdims=True)
    yield
    el2 = jnp.where(e_id == i1, neg, el)
    v2 = jnp.max(el2, axis=0, keepdims=True)
    i2 = jnp.min(jnp.where(el2 == v2, e_id, big), axis=0, keepdims=True)
    yield
    e2 = jnp.exp(v2 - v1)
    w1 = p_g / (1.0 + e2)
    w2 = p_g * e2 / (1.0 + e2)
    oh1 = (e_id == i1).astype(F32)
    oh2 = (e_id == i2).astype(F32)
    a_t = oh1 + oh2
    cnt = jnp.sum(a_t, axis=1, keepdims=True)
    pages = jnp.floor((cnt + (PAGE_ROWS - 1)) * (1.0 / PAGE_ROWS))
    yield
    lower = (_iota((LANES, LANES), 0) > _iota((LANES, LANES), 1)).astype(BF16)
    offp = jnp.dot(lower, jnp.broadcast_to(pages, (LANES, LANES)).astype(BF16),
                   preferred_element_type=F32)[:, 0:1]
    upper = (_iota((tt, tt), 0) < _iota((tt, tt), 1)).astype(BF16)
    rank = jnp.dot(a_t.astype(BF16), upper, preferred_element_type=F32)
    yield
    pos = offp * PAGE_ROWS + rank
    slot1 = jnp.sum(oh1 * pos, axis=0, keepdims=True)
    slot2 = jnp.sum(oh2 * pos, axis=0, keepdims=True)
    yield
    s_id = _iota((TILE_SLOTS, tt), 0).astype(F32)
    sel = jnp.where((s_id == slot1) | (s_id == slot2), 1.0, 0.0).astype(BF16)
    for r0 in range(0, TILE_SLOTS, TILE_SLOTS // 4):
        r1 = r0 + TILE_SLOTS // 4
        xs_ref[r0:r1, :] = jnp.dot(sel[r0:r1], h_hi, preferred_element_type=F32).astype(BF16)
        yield
    r_id = _iota((LANES, tt), 0)
    rows = (jnp.where(r_id == 0, slot1, 0.0) + jnp.where(r_id == 1, slot2, 0.0)
            + jnp.where(r_id == 2, w1, 0.0) + jnp.where(r_id == 3, w2, 0.0))
    route_ref[...] = rows.T
    cnt_ref[...] = jnp.broadcast_to(cnt, (LANES, LANES)).T[0:8, :]


def _merge(x, oa, ob, oc, ga, gb, gc, wo_ref):
    s = ga * oa + gb * ob + gc * oc
    return x + jnp.dot(s.astype(BF16), wo_ref[...], preferred_element_type=F32)


def _mix_kernel(x_ref, au_ref, av_ref, qm_ref, ga_ref, gb_ref, gc_ref, ob0_ref, ob1_ref, mk_ref, mv_ref,
                ws_ref, bst_ref, wo_ref, nf_ref, wrt_ref, br_ref, xq_ref, ck_ref, cv_ref,
                sq_ref, sk_ref, sv_ref, sz_ref, sba_ref, sc_ref, sg_ref, wc_ref, alog_ref, dt_ref, ng_ref,
                x1_ref, xs_ref, route_ref, cnt_ref, xoc_ref, sob_ref, snew_ref, cnew_ref,
                oa_ref, oc_ref, x1_prev_ref):
    i = pl.program_id(0)
    tt = x_ref.shape[0]

    @pl.when(i == 0)
    def _():
        x1_prev_ref[...] = jnp.zeros_like(x1_prev_ref)

    def mix_tile():
        causal = _iota((A_CHUNK, A_CHUNK), 0) >= _iota((A_CHUNK, A_CHUNK), 1)
        for g in range(A_GROUPS):
            wsg = jnp.where(causal, ws_ref[g], 0.0).astype(BF16)
            bcol = bst_ref[:, g:g + 1]
            lo = g * A_CHUNK
            for cc in range(tt // A_CHUNK):
                r0 = cc * A_CHUNK
                sv = jnp.dot(wsg, av_ref[r0:r0 + A_CHUNK, lo:lo + A_CHUNK].astype(BF16),
                             preferred_element_type=F32) + bcol
                oa_ref[r0:r0 + A_CHUNK, lo:lo + A_CHUNK] = au_ref[r0:r0 + A_CHUNK, lo:lo + A_CHUNK] * sv
            yield
        scale = M_HEAD_DIM ** -0.5
        for h in range(M_HEADS):
            lo = h * M_HEAD_DIM
            s = _dot_nt(qm_ref[:, lo:lo + M_HEAD_DIM], mk_ref[:, lo:lo + M_HEAD_DIM]) * scale
            yield
            p = jnp.exp(s - jnp.max(s, axis=-1, keepdims=True))
            pr = p / jnp.sum(p, axis=-1, keepdims=True)
            oc_ref[:, lo:lo + M_HEAD_DIM] = _dot(pr, mv_ref[:, lo:lo + M_HEAD_DIM])
            yield
        n_tiles = pl.num_programs(0) - 1
        in_first_half = jnp.minimum(i, n_tiles - 1) < n_tiles // GDN_SEQS
        ob = jnp.where(in_first_half, ob0_ref[...], ob1_ref[...])
        x1 = _merge(x_ref[...], oa_ref[...], ob, oc_ref[...],
                    ga_ref[...], gb_ref[...], gc_ref[...], wo_ref)
        x1_ref[...] = x1
        x1_prev_ref[i % 2] = x1

    n_req_blocks = xq_ref.shape[0] // XA_REQ
    xa_base = jnp.minimum(i, n_req_blocks - 1) * XA_REQ
    _round_robin([_gdn_step_request(sq_ref, sk_ref, sv_ref, sz_ref, sba_ref, sc_ref, sg_ref, wc_ref,
                                    alog_ref, dt_ref, ng_ref, sob_ref, snew_ref, cnew_ref, xa_base, r)
                  for r in range(XA_REQ)]
                 + [_route_and_dispatch(x1_prev_ref[(i + 1) % 2], nf_ref, wrt_ref, br_ref, xs_ref,
                                        route_ref, cnt_ref),
                    mix_tile(),
                    _xattn1_requests(xq_ref, ck_ref, cv_ref, xoc_ref, xa_base)])


def _mix1_kernel(x_ref, au_ref, av_ref, ga_ref, gb_ref, gc_ref, ob_ref, oc_ref, ws0_ref, bs0_ref,
                 wo_ref, nf_ref, wrt_ref, br_ref, xs_in_ref,
                 x1_ref, xs_ref, route_ref, cnt_ref):
    del xs_in_ref
    oa = au_ref[...] * (ws0_ref[...] * av_ref[...] + bs0_ref[...])
    s = ga_ref[...] * oa + gb_ref[...] * ob_ref[...] + gc_ref[...] * oc_ref[...]
    s_hi = s.astype(BF16)
    s_lo = (s - s_hi.astype(F32)).astype(BF16)
    w = wo_ref[...]
    w_hi = w.astype(BF16)
    w_lo = (w - w_hi.astype(F32)).astype(BF16)
    x1 = x_ref[...] + (jnp.dot(s_hi, w_hi, preferred_element_type=F32)
                       + (jnp.dot(s_lo, w_hi, preferred_element_type=F32)
                          + jnp.dot(s_hi, w_lo, preferred_element_type=F32)))
    x1_ref[...] = x1
    for _ in _route_and_dispatch(x1, nf_ref, wrt_ref, br_ref, xs_ref, route_ref, cnt_ref):
        pass


def _route_out_specs(tt, x1_of, slot_of, route_of):
    return [pl.BlockSpec((tt, D_MODEL), lambda i: (x1_of(i), 0)),
            pl.BlockSpec((None, TILE_SLOTS, D_MODEL), lambda i: (slot_of(i), 0, 0)),
            pl.BlockSpec((tt, LANES), lambda i: (route_of(i), 0)),
            pl.BlockSpec((None, 8, LANES), lambda i: (route_of(i), 0, 0))]


def _mix(x2d, p, ob, mk, mv, w_s, bs_t, wo_bf16, norm_ffn, wr_t, br_col, seq,
         p_sample, ba_sample, cache_k, cache_v, state_conv, state_gdn, w_conv, alog_row, dt_row, norm_gdn):
    n_s = p_sample.shape[0]
    xq = p_sample[:, 6 * D_MODEL:7 * D_MODEL].reshape(n_s, M_HEADS, M_HEAD_DIM)
    n_req_blocks = n_s // XA_REQ
    req_blk = lambda i: jnp.minimum(i, n_req_blocks - 1)
    cache_spec = pl.BlockSpec((XA_REQ, N_MEM, M_HEADS, M_HEAD_DIM), lambda i: (req_blk(i), 0, 0, 0))
    whole_q = pl.BlockSpec((n_s, M_HEADS, M_HEAD_DIM), lambda i: (0, 0, 0))
    s_col = lambda colblk: pl.BlockSpec((n_s, D_MODEL), lambda i: (0, colblk))
    conv_spec = pl.BlockSpec((XA_REQ, B_CONV - 1, C_QKV), lambda i: (req_blk(i), 0, 0))
    state_spec = pl.BlockSpec((XA_REQ, B_HEADS, B_DK, B_DK), lambda i: (req_blk(i), 0, 0, 0))
    rows = x2d.shape[0]
    assert n_req_blocks <= rows // MIX_TILE + 1 and n_s % XA_REQ == 0
    tt = MIX_TILE
    nt = rows // tt
    per_b = seq // tt
    cur = lambda i: jnp.minimum(i, nt - 1)
    prev = lambda i: jnp.maximum(i - 1, 0)
    pblk = lambda colblk: pl.BlockSpec((tt, D_MODEL), lambda i: (cur(i), colblk))
    const = lambda shape: pl.BlockSpec(shape, lambda i: (0,) * len(shape))
    nh = nt // GDN_SEQS
    ob_blk = lambda g: pl.BlockSpec((tt, D_MODEL), lambda i: (jnp.clip(cur(i) - g * nh, 0, nh - 1), 0))
    x1, xs_all, route, cnt, xoc, sob, snew, cnew = pl.pallas_call(
        _mix_kernel,
        grid=(nt + 1,),
        in_specs=[pblk(0), pblk(0), pblk(1), pblk(6), pblk(7), pblk(8), pblk(9), ob_blk(0), ob_blk(1),
                  pl.BlockSpec((N_MEM, D_MODEL), lambda i: (cur(i) // per_b, 0)),
                  pl.BlockSpec((N_MEM, D_MODEL), lambda i: (cur(i) // per_b, 0)),
                  const((A_GROUPS, A_CHUNK, A_CHUNK)), const((A_CHUNK, LANES)),
                  const((D_MODEL, D_MODEL)), const((1, D_MODEL)),
                  const((LANES, D_MODEL)), const((LANES, LANES)),
                  whole_q, cache_spec, cache_spec,
                  s_col(2), s_col(3), s_col(4), s_col(5), const((n_s, LANES)), conv_spec, state_spec,
                  const((B_CONV, C_QKV)), const((1, LANES)), const((1, LANES)), const((1, B_DK))],
        out_specs=_route_out_specs(tt, cur, lambda i: i, prev)
        + [whole_q, const((n_s, D_MODEL)), state_spec, conv_spec],
        out_shape=[jax.ShapeDtypeStruct((rows, D_MODEL), F32),
                   jax.ShapeDtypeStruct((nt + 1, TILE_SLOTS, D_MODEL), BF16),
                   jax.ShapeDtypeStruct((rows, LANES), F32),
                   jax.ShapeDtypeStruct((nt, 8, LANES), F32),
                   jax.ShapeDtypeStruct((n_s, M_HEADS, M_HEAD_DIM), F32),
                   jax.ShapeDtypeStruct((n_s, D_MODEL), F32),
                   jax.ShapeDtypeStruct((n_s, B_HEADS, B_DK, B_DK), F32),
                   jax.ShapeDtypeStruct((n_s, B_CONV - 1, C_QKV), F32)],
        scratch_shapes=[pltpu.VMEM((tt, D_MODEL), F32), pltpu.VMEM((tt, D_MODEL), F32),
                        pltpu.VMEM((2, tt, D_MODEL), F32)],
        compiler_params=_params(1),
        name="mix",
    )(x2d, p, p, p, p, p, p, ob[0], ob[1], mk, mv, w_s, bs_t, wo_bf16, norm_ffn, wr_t, br_col,
      xq, cache_k, cache_v,
      p_sample, p_sample, p_sample, p_sample, ba_sample, state_conv, state_gdn,
      w_conv, alog_row, dt_row, norm_gdn)
    return x1, xs_all, route, cnt, xoc.reshape(n_s, D_MODEL), sob, snew, cnew


def _mix1(x2d, p, ob, oc, ws0_row, bs0_row, wo_f32, norm_ffn, wr_t, br_col, xs_all):
    rows = x2d.shape[0]
    pblk = lambda colblk: pl.BlockSpec((rows, D_MODEL), lambda i: (0, colblk))
    const = lambda shape: pl.BlockSpec(shape, lambda i: (0,) * len(shape))
    zero = lambda i: 0
    return pl.pallas_call(
        _mix1_kernel,
        grid=(1,),
        in_specs=[pblk(0), pblk(0), pblk(1), pblk(7), pblk(8), pblk(9), pblk(0), pblk(0),
                  const((1, D_MODEL)), const((1, D_MODEL)),
                  const((D_MODEL, D_MODEL)), const((1, D_MODEL)),
                  const((LANES, D_MODEL)), const((LANES, LANES)),
                  pl.BlockSpec(memory_space=pl.ANY)],
        out_specs=_route_out_specs(rows, zero, zero, zero),
        out_shape=[jax.ShapeDtypeStruct((rows, D_MODEL), F32),
                   jax.ShapeDtypeStruct(xs_all.shape, BF16),
                   jax.ShapeDtypeStruct((rows, LANES), F32),
                   jax.ShapeDtypeStruct((1, 8, LANES), F32)],
        input_output_aliases={14: 1},
        compiler_params=_params(1),
        name="mix1",
    )(x2d, p, p, p, p, p, ob, oc, ws0_row, bs0_row, wo_f32, norm_ffn, wr_t, br_col, xs_all)


def _experts_kernel(tbl_ref, se_ref, nu_ref, *refs):
    del tbl_ref
    pages = refs[:STEP_PAGES]
    wg_ref, wu_ref, wd_ref, o_ref, wgb_ref, wub_ref, wdb_ref = refs[STEP_PAGES:]
    s = pl.program_id(0)
    prev = se_ref[jnp.maximum(s - 1, 0)]

    @pl.when((s == 0) | (se_ref[s] != prev))
    def _():
        wgb_ref[...] = wg_ref[...].astype(BF16)
        wub_ref[...] = wu_ref[...].astype(BF16)
        wdb_ref[...] = wd_ref[...].astype(BF16)

    @pl.when(s < nu_ref[0])
    def _():
        x = jnp.concatenate([pg[...] for pg in pages], axis=0)
        gate = jnp.dot(x, wgb_ref[...], preferred_element_type=F32)
        up = jnp.dot(x, wub_ref[...], preferred_element_type=F32)
        act = (jax.nn.silu(gate) * up).astype(BF16)
        o_ref[...] = jnp.dot(act, wdb_ref[...], preferred_element_type=F32).astype(BF16)

    @pl.when(s >= nu_ref[0])
    def _():
        o_ref[...] = jnp.zeros_like(o_ref)


def _experts(tbl, step_e, n_used, xs_pages, w_gate, w_up, w_down, n_steps):
    def page_spec(k):
        return pl.BlockSpec((None, PAGE_ROWS, D_MODEL),
                            lambda s, tbl, se, nu: (tbl[s * STEP_PAGES + k], 0, 0))

    wspec = lambda shape: pl.BlockSpec((None,) + shape, lambda s, tbl, se, nu: (se[s], 0, 0))
    grid_spec = pltpu.PrefetchScalarGridSpec(
        num_scalar_prefetch=3,
        grid=(n_steps,),
        in_specs=[page_spec(k) for k in range(STEP_PAGES)]
        + [wspec((D_MODEL, D_EXPERT)), wspec((D_MODEL, D_EXPERT)), wspec((D_EXPERT, D_MODEL))],
        out_specs=pl.BlockSpec((STEP_ROWS, D_MODEL), lambda s, tbl, se, nu: (s, 0)),
        scratch_shapes=[pltpu.VMEM((D_MODEL, D_EXPERT), BF16), pltpu.VMEM((D_MODEL, D_EXPERT), BF16),
                        pltpu.VMEM((D_EXPERT, D_MODEL), BF16)],
    )
    return pl.pallas_call(
        _experts_kernel,
        grid_spec=grid_spec,
        out_shape=jax.ShapeDtypeStruct((n_steps * STEP_ROWS, D_MODEL), BF16),
        compiler_params=_params(1),
        name="experts",
    )(tbl, step_e, n_used, *([xs_pages] * STEP_PAGES), w_gate, w_up, w_down)


def _combine_kernel(inv_ref, *refs):
    del inv_ref
    pages = refs[:TILE_PAGES]
    x1_ref, route_ref, nfin_ref, y_ref = refs[TILE_PAGES:]
    out_loc = jnp.concatenate([pg[...] for pg in pages], axis=0)
    tt = x1_ref.shape[0]
    route = route_ref[...]
    s_id = _iota((tt, TILE_SLOTS), 1).astype(F32)
    sel = (jnp.where(s_id == route[:, 0:1], route[:, 2:3], 0.0)
           + jnp.where(s_id == route[:, 1:2], route[:, 3:4], 0.0)).astype(BF16)
    moe = jnp.dot(sel, out_loc, preferred_element_type=F32)
    y_ref[...] = _rms(x1_ref[...] + moe, nfin_ref[...])


def _combine(inv, out_pages, x1, route, norm_final, tt):
    rows = x1.shape[0]

    def page_spec(k):
        return pl.BlockSpec((None, PAGE_ROWS, D_MODEL), lambda i, inv: (inv[i * TILE_PAGES + k], 0, 0))

    grid_spec = pltpu.PrefetchScalarGridSpec(
        num_scalar_prefetch=1,
        grid=(rows // tt,),
        in_specs=[page_spec(k) for k in range(TILE_PAGES)]
        + [pl.BlockSpec((tt, D_MODEL), lambda i, inv: (i, 0)),
           pl.BlockSpec((tt, LANES), lambda i, inv: (i, 0)),
           pl.BlockSpec((1, D_MODEL), lambda i, inv: (0, 0))],
        out_specs=pl.BlockSpec((tt, D_MODEL), lambda i, inv: (i, 0)),
    )
    return pl.pallas_call(
        _combine_kernel,
        grid_spec=grid_spec,
        out_shape=jax.ShapeDtypeStruct((rows, D_MODEL), F32),
        compiler_params=_params(1),
        name="combine",
    )(inv, *([out_pages] * TILE_PAGES), x1, route, norm_final)


def _page_tables(cnt, n_steps):
    n_tiles = cnt.shape[0]
    pg = (cnt + (PAGE_ROWS - 1)) // PAGE_ROWS
    lend = jnp.cumsum(pg, axis=1)
    loff = lend - pg
    cum_t = jnp.cumsum(pg, axis=0)
    pref = cum_t - pg
    tot = cum_t[-1]
    totp = ((tot + (STEP_PAGES - 1)) // STEP_PAGES) * STEP_PAGES
    gend = jnp.cumsum(totp)
    gstart = gend - totp
    sel = lambda onehot, vals: jnp.dot(onehot, vals.astype(F32), precision=lax.Precision.HIGHEST)
    gp = jnp.arange(n_steps * STEP_PAGES, dtype=jnp.int32)
    in_e = (gp[:, None] >= gstart[None, :]) & (gp[:, None] < gend[None, :])
    oh_e = in_e.astype(F32)
    r = gp.astype(F32) - sel(oh_e, gstart)
    valid = jnp.any(in_e, axis=1) & (r < sel(oh_e, tot))
    cum_col = sel(oh_e, cum_t.T)
    pref_col = sel(oh_e, pref.T)
    loff_col = sel(oh_e, loff.T)
    in_i = (r[:, None] >= pref_col) & (r[:, None] < cum_col)
    tile_base = (jnp.arange(n_tiles, dtype=jnp.int32) * TILE_PAGES).astype(F32)
    src = jnp.sum(jnp.where(in_i, tile_base[None, :] + loff_col + (r[:, None] - pref_col), 0.0), axis=1)
    tbl = jnp.where(valid, src, 0.0).astype(jnp.int32)
    in_step = in_e[::STEP_PAGES]
    e_ids = jnp.arange(N_EXPERTS, dtype=jnp.int32)
    step_e = jnp.where(jnp.any(in_step, axis=1), jnp.sum(jnp.where(in_step, e_ids[None, :], 0), axis=1),
                       N_EXPERTS - 1).astype(jnp.int32)
    n_used = (gend[-1] // STEP_PAGES).astype(jnp.int32).reshape(1)
    lp = jnp.arange(TILE_PAGES, dtype=jnp.int32)[None, :, None]
    in_l = (lp >= loff[:, None, :]) & (lp < lend[:, None, :])
    gpos = jnp.sum(jnp.where(in_l, gstart[None, None, :] + pref[:, None, :] + lp - loff[:, None, :], 0), axis=2)
    zero_page = (n_steps - 1) * STEP_PAGES
    inv = jnp.where(jnp.any(in_l, axis=2), gpos, zero_page).astype(jnp.int32).reshape(-1)
    return tbl, step_e, n_used, inv


def kernel(x_prompt, x_sample, mem_prompt, cache_mem_k, cache_mem_v, state_gdn, state_conv,
           norm_mix, w_in, b_gate, w_s, b_s, norm_a_v, w_conv, a_log, dt_bias, norm_gdn_out,
           norm_mem, w_mem_kv, w_o, norm_ffn, w_router_group, b_router_group, w_router_expert,
           b_router_expert, w_exp_gate, w_exp_up, w_exp_down, norm_final):
    depth = norm_mix.shape[0]
    assert depth == 1
    bsz, seq, _ = x_prompt.shape
    n_s = x_sample.shape[0]
    assert x_sample.shape[1] == 1 and seq % MIX_TILE == 0 and n_s % 8 == 0 and n_s <= MIX_TILE
    assert bsz % GDN_SEQS == 0
    l = 0
    row = lambda v: v.reshape(1, -1)

    wi = w_in[l]
    o_z = 2 * D_MODEL + C_QKV
    o_beta = o_z + D_MODEL
    o_qm = o_beta + 2 * B_HEADS
    o_gate = o_qm + D_MODEL
    w_perm = jnp.concatenate([wi[:, :o_beta], wi[:, o_qm:]], axis=1)
    w_main = w_perm.astype(BF16)
    w_ba_f32 = jnp.pad(wi[:, o_beta:o_qm], ((0, 0), (0, LANES - 2 * B_HEADS)))
    w_ba = w_ba_f32.astype(BF16)
    pad_heads = lambda v: jnp.pad(v.reshape(1, B_HEADS), ((0, 0), (B_HEADS, LANES - 2 * B_HEADS)))
    alog_row = pad_heads(a_log[l])
    dt_row = pad_heads(dt_bias[l])
    w_kv = w_mem_kv[l].astype(BF16)
    wo = w_o[l].astype(BF16)
    wr_t = jnp.pad(jnp.concatenate([w_router_expert[l], w_router_group[l]], axis=1).T,
                   ((0, LANES - N_EXPERTS - N_GROUPS), (0, 0)))
    br_col = jnp.pad(jnp.concatenate([b_router_expert[l], b_router_group[l]]).reshape(-1, 1),
                     ((0, LANES - N_EXPERTS - N_GROUPS), (0, LANES - 1)))
    bs_t = jnp.pad(b_s[l].T, ((0, 0), (0, LANES - A_GROUPS)))
    ws0_row = jnp.repeat(w_s[l][:, 0, 0], A_CHUNK).reshape(1, D_MODEL)
    bs0_row = jnp.repeat(b_s[l][:, 0], A_CHUNK).reshape(1, D_MODEL)

    xp = x_prompt.reshape(bsz * seq, D_MODEL)
    xs_ = x_sample.reshape(n_s, D_MODEL)
    n_tiles_p = (bsz * seq) // MIX_TILE
    n_tiles = n_tiles_p + 1

    p_s, ba_s = _proj2(xs_, row(norm_mix[l]), w_perm, w_ba_f32, row(norm_a_v[l]), row(b_gate[l]))
    mk, mv = _memkv(mem_prompt.reshape(bsz * N_MEM, D_MODEL), row(norm_mem[l]), w_kv)
    p_p, ba_p = _proj(xp, row(norm_mix[l]), w_main, w_ba, row(norm_a_v[l]), row(b_gate[l]),
                      tm=min(PROJ_ROWS, bsz * seq))
    ob_p, s_p = _gdn(p_p, ba_p, w_conv[l], alog_row, dt_row, row(norm_gdn_out[l]), bsz, seq)
    x1_p, xs_all, route_p, cnt_p, oc_s, ob_s, s_s, c_s = _mix(
        xp, p_p, ob_p, mk, mv, w_s[l], bs_t, wo, row(norm_ffn[l]), wr_t, br_col, seq,
        p_s, ba_s, cache_mem_k[l], cache_mem_v[l], state_conv[l], state_gdn[l], w_conv[l], alog_row, dt_row,
        row(norm_gdn_out[l]))
    x1_s, xs_all, route_s, cnt_s = _mix1(xs_, p_s, ob_s, oc_s, ws0_row, bs0_row, w_o[l], row(norm_ffn[l]),
                                         wr_t, br_col, xs_all)
    cnt = jnp.concatenate([cnt_s[:, 0, :N_EXPERTS], cnt_p[:, 0, :N_EXPERTS]], axis=0).astype(jnp.int32)
    max_pages = n_tiles_p * TILE_PAGES + (2 * n_s) // PAGE_ROWS + N_EXPERTS
    n_steps = (max_pages + N_EXPERTS * (STEP_PAGES - 1)) // STEP_PAGES + 1
    tbl, step_e, n_used, inv = _page_tables(cnt, n_steps)
    ne = N_GROUPS * EXP_PER_GROUP
    out_sorted = _experts(tbl, step_e, n_used, xs_all.reshape(n_tiles * TILE_PAGES, PAGE_ROWS, D_MODEL),
                          w_exp_gate[l].reshape(ne, D_MODEL, D_EXPERT),
                          w_exp_up[l].reshape(ne, D_MODEL, D_EXPERT),
                          w_exp_down[l].reshape(ne, D_EXPERT, D_MODEL), n_steps)
    out_pages = out_sorted.reshape(n_steps * STEP_PAGES, PAGE_ROWS, D_MODEL)
    y_s = _combine(inv[:TILE_PAGES], out_pages, x1_s, route_s, row(norm_final), n_s)
    y_p = _combine(inv[TILE_PAGES:], out_pages, x1_p, route_p, row(norm_final), MIX_TILE)

    conv_tail = p_p.reshape(bsz, seq, PROJ_COLS)[:, seq - (B_CONV - 1):, 2 * D_MODEL:2 * D_MODEL + C_QKV]
    return (y_p.reshape(bsz, seq, D_MODEL),
            y_s.reshape(n_s, 1, D_MODEL),
            mk.reshape(1, bsz, N_MEM, M_HEADS, M_HEAD_DIM),
            mv.reshape(1, bsz, N_MEM, M_HEADS, M_HEAD_DIM),
            jnp.concatenate(s_p, axis=0)[None],
            conv_tail[None],
            s_s[None],
            c_s[None],
            p_s[:, D_MODEL:2 * D_MODEL].reshape(1, n_s, 1, D_MODEL))
```

```python
import functools
import math

import jax
import jax.numpy as jnp
from jax import lax
from jax.experimental import pallas as pl
from jax.experimental.pallas import tpu as pltpu

F32 = jnp.float32
BF16 = jnp.bfloat16

D_MODEL = 1024
A_GROUPS = 8
A_CHUNK = 128
B_HEADS = 8
B_DK = 128
C_QKV = 3 * D_MODEL
B_CONV = 4
N_MEM = 256
M_HEADS = 4
M_HEAD_DIM = 256
N_GROUPS = 4
EXP_PER_GROUP = 8
N_EXPERTS = 32
D_EXPERT = 512
EPS = 1e-6

LANES = 128
GDN_CHUNK = 128
GDN_SEQS = 2
MIX_TILE = 256
PAGE_ROWS = 16
TILE_PAGES = 64
TILE_SLOTS = PAGE_ROWS * TILE_PAGES
STEP_PAGES = 32
STEP_ROWS = PAGE_ROWS * STEP_PAGES
PROJ_COLS = 10 * D_MODEL
PROJ_HEAD_TILES = 6
PROJ_CHUNK = 256
PROJ_ROWS = 2048
PROJ_PIECE_ROWS = 1024
VMEM_LIMIT = 56 * 1024 * 1024


def _params(n_grid):
    return pltpu.CompilerParams(dimension_semantics=("arbitrary",) * n_grid,
                                vmem_limit_bytes=VMEM_LIMIT)


def _rms(x, g):
    return x * lax.rsqrt(jnp.mean(x * x, axis=-1, keepdims=True) + EPS) * g


def _dot(a, b):
    return jnp.dot(a.astype(BF16), b.astype(BF16), preferred_element_type=F32)


def _dot_nt(a, b):
    return lax.dot_general(a.astype(BF16), b.astype(BF16), (((1,), (1,)), ((), ())),
                           preferred_element_type=F32)


def _dot_tn(a, b):
    return lax.dot_general(a.astype(BF16), b.astype(BF16), (((0,), (0,)), ((), ())),
                           preferred_element_type=F32)


def _split3(x):
    hi = x.astype(BF16)
    r = x - hi.astype(F32)
    mid = r.astype(BF16)
    lo = (r - mid.astype(F32)).astype(BF16)
    return hi, mid, lo


def _iota(shape, axis):
    return lax.broadcasted_iota(jnp.int32, shape, axis)


def _memkv_kernel(mem_ref, g_ref, w_ref, k_ref, v_ref):
    h = _rms(mem_ref[...], g_ref[...]).astype(BF16)
    kv = jnp.dot(h, w_ref[...], preferred_element_type=F32)
    k_ref[...] = kv[:, :D_MODEL]
    v_ref[...] = kv[:, D_MODEL:]


def _memkv(mem2d, norm_mem, w_kv_bf16):
    rows = mem2d.shape[0]
    nb = rows // N_MEM
    return pl.pallas_call(
        _memkv_kernel,
        grid=(nb,),
        in_specs=[pl.BlockSpec((N_MEM, D_MODEL), lambda i: (i, 0)),
                  pl.BlockSpec((1, D_MODEL), lambda i: (0, 0)),
                  pl.BlockSpec((D_MODEL, 2 * D_MODEL), lambda i: (0, 0))],
        out_specs=[pl.BlockSpec((N_MEM, D_MODEL), lambda i: (i, 0)),
                   pl.BlockSpec((N_MEM, D_MODEL), lambda i: (i, 0))],
        out_shape=[jax.ShapeDtypeStruct((rows, D_MODEL), F32)] * 2,
        compiler_params=_params(1),
        name="memkv",
    )(mem2d, norm_mem, w_kv_bf16)


def _proj_kernel(x_ref, nm_ref, w_ref, wba_ref, nav_ref, bg_ref, p_ref, ba_ref, hn_ref):
    j = pl.program_id(1)

    @pl.when(j == 0)
    def _():
        hn = _rms(x_ref[...], nm_ref[...]).astype(BF16)
        hn_ref[...] = hn
        ba_ref[...] = jnp.dot(hn, wba_ref[...], preferred_element_type=F32)

    tm = x_ref.shape[0]
    rows = [(r0, min(tm, r0 + PROJ_PIECE_ROWS)) for r0 in range(0, tm, PROJ_PIECE_ROWS)]
    pieces = [(r, c0) for r in rows for c0 in range(0, D_MODEL, PROJ_CHUNK)]

    def chunks(epilogue):
        def matmul(piece):
            (r0, r1), c0 = piece
            return jnp.dot(hn_ref[r0:r1, :], w_ref[:, c0:c0 + PROJ_CHUNK], preferred_element_type=F32)

        def finish(piece, acc):
            (r0, r1), c0 = piece
            p_ref[r0:r1, c0:c0 + PROJ_CHUNK] = epilogue(acc, piece)

        acc = matmul(pieces[0])
        for n in range(1, len(pieces)):
            nxt = matmul(pieces[n])
            finish(pieces[n - 1], acc)
            acc = nxt
        finish(pieces[-1], acc)

    @pl.when(j == 0)
    def _():
        chunks(lambda acc, piece: jax.nn.gelu(acc))

    @pl.when(j == 1)
    def _():
        ssq = {r: [] for r in rows}

        def gelu_ssq(acc, piece):
            a = jax.nn.gelu(acc)
            ssq[piece[0]].append(jnp.sum(a * a, axis=-1, keepdims=True))
            return a

        chunks(gelu_ssq)
        for r0, r1 in rows:
            inv = lax.rsqrt(sum(ssq[(r0, r1)]) * (1.0 / D_MODEL) + EPS)
            p_ref[r0:r1, :] = p_ref[r0:r1, :] * inv * nav_ref[...]

    @pl.when((j >= 2) & (j <= 6))
    def _():
        chunks(lambda acc, piece: acc)

    @pl.when(j >= 7)
    def _():
        chunks(lambda acc, piece: jax.nn.sigmoid(acc + bg_ref[:, piece[1]:piece[1] + PROJ_CHUNK]))


def _proj(x2d, norm_mix, w_bf16, wba_bf16, norm_a_v, b_gate, tm):
    rows = x2d.shape[0]
    ncol = PROJ_COLS // D_MODEL
    return pl.pallas_call(
        _proj_kernel,
        grid=(rows // tm, ncol),
        in_specs=[pl.BlockSpec((tm, D_MODEL), lambda i, j: (i, 0)),
                  pl.BlockSpec((1, D_MODEL), lambda i, j: (0, 0)),
                  pl.BlockSpec((D_MODEL, D_MODEL), lambda i, j: (0, j)),
                  pl.BlockSpec((D_MODEL, LANES), lambda i, j: (0, 0)),
                  pl.BlockSpec((1, D_MODEL), lambda i, j: (0, 0)),
                  pl.BlockSpec((1, D_MODEL), lambda i, j: (0, jnp.maximum(j - 7, 0)))],
        out_specs=[pl.BlockSpec((tm, D_MODEL), lambda i, j: (i, j)),
                   pl.BlockSpec((tm, LANES), lambda i, j: (i, 0))],
        out_shape=[jax.ShapeDtypeStruct((rows, PROJ_COLS), F32),
                   jax.ShapeDtypeStruct((rows, LANES), F32)],
        scratch_shapes=[pltpu.VMEM((tm, D_MODEL), BF16)],
        compiler_params=_params(2),
        name="proj",
    )(x2d, norm_mix, w_bf16, wba_bf16, norm_a_v, b_gate)


def _proj2_kernel(x_ref, nm_ref, w_head_ref, w_tail_ref, wba_ref, nav_ref, bg_ref, p_ref, ba_ref,
                  hh_ref, hl_ref):
    j = pl.program_id(1)

    def dot2(w_f32_ref):
        w = w_f32_ref[...]
        w_hi = w.astype(BF16)
        w_lo = (w - w_hi.astype(F32)).astype(BF16)
        return (jnp.dot(hh_ref[...], w_hi, preferred_element_type=F32)
                + (jnp.dot(hl_ref[...], w_hi, preferred_element_type=F32)
                   + jnp.dot(hh_ref[...], w_lo, preferred_element_type=F32)))

    @pl.when(j == 0)
    def _():
        h = _rms(x_ref[...], nm_ref[...])
        hi = h.astype(BF16)
        hh_ref[...] = hi
        hl_ref[...] = (h - hi.astype(F32)).astype(BF16)
        ba_ref[...] = dot2(wba_ref)

    @pl.when(j == 0)
    def _():
        p_ref[...] = jax.nn.gelu(dot2(w_head_ref))

    @pl.when(j == 1)
    def _():
        p_ref[...] = _rms(jax.nn.gelu(dot2(w_head_ref)), nav_ref[...])

    @pl.when((j >= 2) & (j < PROJ_HEAD_TILES))
    def _():
        p_ref[...] = dot2(w_head_ref)

    @pl.when(j == PROJ_HEAD_TILES)
    def _():
        p_ref[...] = dot2(w_tail_ref)

    @pl.when(j > PROJ_HEAD_TILES)
    def _():
        p_ref[...] = jax.nn.sigmoid(dot2(w_tail_ref) + bg_ref[...])


def _proj2(x2d, norm_mix, w_in, w_tail, wba_f32, norm_a_v, b_gate):
    rows = x2d.shape[0]
    ncol = PROJ_COLS // D_MODEL
    w_head_spec = pl.BlockSpec((D_MODEL, D_MODEL), lambda i, j: (0, jnp.minimum(j, PROJ_HEAD_TILES - 1)))
    w_tail_spec = pl.BlockSpec((D_MODEL, D_MODEL), lambda i, j: (0, jnp.maximum(j - PROJ_HEAD_TILES, 0)))
    wba = pl.BlockSpec((D_MODEL, LANES), lambda i, j: (0, 0))
    vec = pl.BlockSpec((1, D_MODEL), lambda i, j: (0, 0))
    return pl.pallas_call(
        _proj2_kernel,
        grid=(1, ncol),
        in_specs=[pl.BlockSpec((rows, D_MODEL), lambda i, j: (0, 0)), vec, w_head_spec, w_tail_spec, wba,
                  vec, pl.BlockSpec((1, D_MODEL), lambda i, j: (0, jnp.maximum(j - 7, 0)))],
        out_specs=[pl.BlockSpec((rows, D_MODEL), lambda i, j: (0, j)),
                   pl.BlockSpec((rows, LANES), lambda i, j: (0, 0))],
        out_shape=[jax.ShapeDtypeStruct((rows, PROJ_COLS), F32),
                   jax.ShapeDtypeStruct((rows, LANES), F32)],
        scratch_shapes=[pltpu.VMEM((rows, D_MODEL), BF16), pltpu.VMEM((rows, D_MODEL), BF16)],
        compiler_params=_params(2),
        name="proj2",
    )(x2d, norm_mix, w_in, w_tail, wba_f32, norm_a_v, b_gate)


def _gate_terms(ba, alog_row, dt_row):
    beta = jax.nn.sigmoid(ba)
    g = -jnp.exp(alog_row) * jax.nn.softplus(ba + dt_row)
    return beta, g


class _Lazy:
    def __init__(self, make):
        self._make = make

    def __getitem__(self, index):
        return self._make(index)


def _round_robin(tasks, turns=None):
    done = object()
    turns = turns or [1] * len(tasks)
    live = list(zip(tasks, turns))
    while live:
        live = [(t, n) for t, n in live if all(next(t, done) is not done for _ in range(n))]


def _gdn_kernel(n_c, *refs):
    per = 5
    q_refs, k_refs, v_refs, ba_refs, z_refs = (
        [refs[per * g + j] for g in range(GDN_SEQS)] for j in range(per))
    wc_ref, alog_ref, dt_ref, ng_ref = refs[per * GDN_SEQS:per * GDN_SEQS + 4]
    ob_refs = refs[per * GDN_SEQS + 4:per * GDN_SEQS + 4 + GDN_SEQS]
    s_out_refs = refs[per * GDN_SEQS + 4 + GDN_SEQS:per * GDN_SEQS + 4 + 2 * GDN_SEQS]
    s_ref, ext_ref, prep_ref, gate_ref = refs[per * GDN_SEQS + 4 + 2 * GDN_SEQS:]
    s = pl.program_id(0)
    C = GDN_CHUNK
    par = s % 2
    heads = range(B_HEADS)

    @pl.when(s == 0)
    def _():
        s_ref[...] = jnp.zeros_like(s_ref)
        ext_ref[:, 0:8, :] = jnp.zeros((GDN_SEQS, 8, C_QKV), F32)
        prep_ref[...] = jnp.zeros_like(prep_ref)
        gate_ref[...] = jnp.zeros_like(gate_ref)

    row = _iota((C, C), 0)
    col = _iota((C, C), 1)
    incl = row >= col
    pace = []

    def mark(results):
        pace.append(results[-1][0:1, 0:B_DK] * 0.0)

    def prepare(g):
        q_ref, k_ref, v_ref, ba_ref = q_refs[g], k_refs[g], v_refs[g], ba_refs[g]
        keep = (s % n_c != 0).astype(F32)
        ext_ref[g, 0:8, :] = ext_ref[g, 0:8, :] * keep
        ext_ref[g, 8:8 + C, 0:D_MODEL] = q_ref[...]
        ext_ref[g, 8:8 + C, D_MODEL:2 * D_MODEL] = k_ref[...]
        ext_ref[g, 8:8 + C, 2 * D_MODEL:3 * D_MODEL] = v_ref[...]
        beta_all, g_all = _gate_terms(ba_ref[...], alog_ref[...], dt_ref[...])
        g_hi, g_mid, g_lo = _split3(g_all)
        tri = incl.astype(BF16)
        gcum = (jnp.dot(tri, g_hi, preferred_element_type=F32)
                + jnp.dot(tri, g_mid, preferred_element_type=F32)
                + jnp.dot(tri, g_lo, preferred_element_type=F32))
        gate_ref[par, g, 0] = beta_all
        gate_ref[par, g, 1] = gcum
        gate_ref[par, g, 2] = gcum.T
        yield

        def conv_silu(lo):
            pace_row = pace[-1] if pace else 0.0
            acc = (wc_ref[3:4, lo:lo + B_DK] + pace_row) * ext_ref[g, 8:8 + C, lo:lo + B_DK]
            for j in range(1, B_CONV):
                acc = acc + ((wc_ref[3 - j:4 - j, lo:lo + B_DK] + pace_row)
                             * ext_ref[g, 8 - j:8 - j + C, lo:lo + B_DK])
            return jax.nn.silu(acc)

        def l2n(t):
            return t * lax.rsqrt(jnp.sum(t * t, axis=-1, keepdims=True) + EPS)

        for h in heads:
            lo = h * B_DK
            prep_ref[par, g, :, lo:lo + B_DK] = l2n(conv_silu(lo)) * (B_DK ** -0.5)
            yield
            prep_ref[par, g, :, D_MODEL + lo:D_MODEL + lo + B_DK] = l2n(conv_silu(D_MODEL + lo))
            yield
            prep_ref[par, g, :, 2 * D_MODEL + lo:2 * D_MODEL + lo + B_DK] = conv_silu(2 * D_MODEL + lo)
            yield
        ext_ref[g, 0:8, :] = ext_ref[g, C:C + 8, :]

    def chain():
        heads = range(GDN_SEQS * B_HEADS)
        half = [u // B_HEADS for u in heads]
        head = [u % B_HEADS for u in heads]
        old = 1 - par
        strict = row > col
        eye = (row == col).astype(F32)
        same_blk = [(row // b) == (col // b) for b in (8, 16, 32, 64, 128)]
        beta_all = [gate_ref[old, g, 0] for g in range(GDN_SEQS)]
        gcum = [gate_ref[old, g, 1] for g in range(GDN_SEQS)]
        gcum_t = [gate_ref[old, g, 2] for g in range(GDN_SEQS)]
        lanes = lambda part, u: slice(part * D_MODEL + head[u] * B_DK, part * D_MODEL + (head[u] + 1) * B_DK)
        q = _Lazy(lambda u: prep_ref[old, half[u], :, lanes(0, u)])
        k = _Lazy(lambda u: prep_ref[old, half[u], :, lanes(1, u)])
        v = _Lazy(lambda u: prep_ref[old, half[u], :, lanes(2, u)])
        bcol = [beta_all[half[u]][:, head[u]:head[u] + 1] for u in heads]
        gcol = [gcum[half[u]][:, 8 + head[u]:9 + head[u]] for u in heads]
        glast = [gcum[half[u]][C - 1:C, 8 + head[u]:9 + head[u]] for u in heads]
        eg = [jnp.exp(gcol[h]) for h in heads]
        kb = _Lazy(lambda u: k[u] * bcol[u])
        yield
        a1 = [_dot_nt(jnp.concatenate([kb[h], q[h]], axis=0), k[h]) for h in heads]
        mark(a1)
        yield
        m, attn = [], []
        for u in heads:
            grow = gcum_t[half[u]][8 + head[u]:9 + head[u], :]
            dec = jnp.where(incl, jnp.exp(jnp.where(incl, gcol[u] - grow, 0.0)), 0.0)
            m.append(jnp.where(strict, a1[u][:C] * dec, 0.0))
            attn.append(a1[u][C:] * dec)
        m0 = [jnp.where(same_blk[0], m[h], 0.0) for h in heads]
        x = [eye - m0[h] for h in heads]
        pw = [_dot(m0[h], m0[h]) for h in heads]
        mark(pw)
        yield
        x = [x[h] + _dot(x[h], pw[h]) for h in heads]
        mark(x)
        yield
        pw = [_dot(pw[h], pw[h]) for h in heads]
        mark(pw)
        yield
        x = [x[h] + _dot(x[h], pw[h]) for h in heads]
        mark(x)
        yield
        for lvl in range(1, len(same_blk)):
            in_lvl = same_blk[lvl] & jnp.logical_not(same_blk[lvl - 1])
            xb = [x[h].astype(BF16) for h in heads]
            t = [_dot(xb[h], jnp.where(in_lvl, m[h], 0.0)) for h in heads]
            mark(t)
            yield
            x = [x[h] - _dot(t[h], xb[h]) for h in heads]
            mark(x)
            yield
        rhs = [jnp.concatenate([v[h] * bcol[h], kb[h] * eg[h]], axis=1) for h in heads]
        sol = [_dot(x[h], rhs[h]) for h in heads]
        mark(sol)
        yield
        keep = ((s + n_c - 1) % n_c != 0).astype(F32)
        s_old = [s_ref[half[u], head[u]] * keep for u in heads]
        b1 = [_dot(jnp.concatenate([sol[h][:, B_DK:], q[h] * eg[h]], axis=0), s_old[h]) for h in heads]
        mark(b1)
        yield
        u = [sol[h][:, :B_DK] - b1[h][:C] for h in heads]
        o = [b1[h][C:] + _dot(attn[h], u[h]) for h in heads]
        kd = [k[h] * jnp.exp(glast[h] - gcol[h]) for h in heads]
        mark(o)
        yield
        s_new = [s_old[h] * jnp.exp(glast[h]) + _dot_tn(kd[h], u[h]) for h in heads]
        mark(s_new)
        yield
        for u in heads:
            g, h = half[u], head[u]
            s_ref[g, h] = s_new[u]
            s_out_refs[g][h] = s_new[u]
            zh = z_refs[g][:, h * B_DK:(h + 1) * B_DK]
            ob_refs[g][:, h * B_DK:(h + 1) * B_DK] = _rms(o[u], ng_ref[...]) * jax.nn.silu(zh)
            yield

    done = object()
    chain_task = chain()
    prep_tasks = [prepare(g) for g in range(GDN_SEQS)]
    for _ in range(3):
        next(chain_task)
    alive = True
    while alive:
        alive = False
        for task in prep_tasks + [chain_task] + prep_tasks + prep_tasks + [chain_task]:
            alive = (next(task, done) is not done) or alive


def _gdn(p, ba, w_conv, alog_row, dt_row, norm_gdn, bsz, seq):
    nc = seq // GDN_CHUNK
    total = (bsz // GDN_SEQS) * nc
    cur = lambda s: jnp.minimum(s, total - 1)
    prev = lambda s: jnp.maximum(s - 1, 0)
    small = lambda shape: pl.BlockSpec(shape, lambda s: (0, 0))

    def half_specs(g):
        blk = lambda colblk: pl.BlockSpec((GDN_CHUNK, D_MODEL), lambda s: (g * total + cur(s), colblk))
        return [blk(2), blk(3), blk(4),
                pl.BlockSpec((GDN_CHUNK, LANES), lambda s: (g * total + cur(s), 0)),
                pl.BlockSpec((GDN_CHUNK, D_MODEL), lambda s: (g * total + prev(s), 5))]

    halves = range(GDN_SEQS)
    outs = pl.pallas_call(
        functools.partial(_gdn_kernel, nc),
        grid=(total + 1,),
        in_specs=[spec for g in halves for spec in half_specs(g)]
        + [small((B_CONV, C_QKV)), small((1, LANES)), small((1, LANES)), small((1, B_DK))],
        out_specs=[pl.BlockSpec((GDN_CHUNK, D_MODEL), lambda s: (prev(s), 0)) for g in halves]
        + [pl.BlockSpec((None, B_HEADS, B_DK, B_DK), lambda s: (prev(s) // nc, 0, 0, 0)) for g in halves],
        out_shape=[jax.ShapeDtypeStruct((total * GDN_CHUNK, D_MODEL), F32) for g in halves]
        + [jax.ShapeDtypeStruct((bsz // GDN_SEQS, B_HEADS, B_DK, B_DK), F32) for g in halves],
        scratch_shapes=[pltpu.VMEM((GDN_SEQS, B_HEADS, B_DK, B_DK), F32),
                        pltpu.VMEM((GDN_SEQS, 8 + GDN_CHUNK, C_QKV), F32),
                        pltpu.VMEM((2, GDN_SEQS, GDN_CHUNK, C_QKV), F32),
                        pltpu.VMEM((2, GDN_SEQS, 3, GDN_CHUNK, LANES), F32)],
        compiler_params=_params(1),
        name="gdn",
    )(*([p, p, p, ba, p] * GDN_SEQS), w_conv, alog_row, dt_row, norm_gdn)
    return outs[:GDN_SEQS], outs[GDN_SEQS:]


def _gdn_step_request(q_ref, k_ref, v_ref, z_ref, ba_ref, sc_ref, sg_ref, wc_ref, alog_ref, dt_ref,
                      ng_ref, ob_ref, snew_ref, cnew_ref, base, r):
    def l2n(t):
        return t * lax.rsqrt(jnp.sum(t * t, axis=-1, keepdims=True) + EPS)

    def column(t):
        return jnp.broadcast_to(t, (B_DK, B_DK)).T

    row = pl.ds(base + r, 1)
    beta_all, g_all = _gate_terms(ba_ref[row, :], alog_ref[...], dt_ref[...])
    eg_all = jnp.exp(g_all)
    pre = jnp.concatenate([q_ref[row, :], k_ref[row, :], v_ref[row, :]], axis=1)
    conv = wc_ref[3:4, :] * pre
    for j in range(B_CONV - 1):
        conv = conv + wc_ref[j:j + 1, :] * sc_ref[r, j:j + 1, :]
    cnew_ref[r, 0:1, :] = sc_ref[r, 1:2, :]
    cnew_ref[r, 1:2, :] = sc_ref[r, 2:3, :]
    cnew_ref[r, 2:3, :] = pre
    qkv = jax.nn.silu(conv)
    gate = jax.nn.silu(z_ref[row, :])
    out = []
    yield
    for h in range(B_HEADS):
        lo = h * B_DK
        qcol = column(l2n(qkv[:, lo:lo + B_DK]) * (B_DK ** -0.5))
        kcol = column(l2n(qkv[:, D_MODEL + lo:D_MODEL + lo + B_DK]))
        vh = qkv[:, 2 * D_MODEL + lo:2 * D_MODEL + lo + B_DK]
        s_old = sg_ref[r, h]
        beta = beta_all[:, h:h + 1]
        eg = eg_all[:, 8 + h:9 + h]
        ks = jnp.sum(s_old * kcol, axis=0, keepdims=True)
        u = beta * (vh - eg * ks)
        s_new = s_old * eg + kcol * u
        snew_ref[r, h] = s_new
        yield
        o = jnp.sum(s_new * qcol, axis=0, keepdims=True)
        out.append(_rms(o, ng_ref[...]) * gate[:, lo:lo + B_DK])
        yield
    ob_ref[row, :] = jnp.concatenate(out, axis=1)


XA_REQ = 2


def _xattn1_requests(q_ref, ck_ref, cv_ref, oc_ref, base):
    scale = M_HEAD_DIM ** -0.5
    pairs = (N_MEM // 2, 2 * M_HEADS, M_HEAD_DIM)
    fold = lambda t, op: op(t[0:M_HEADS], t[M_HEADS:2 * M_HEADS])
    twice = lambda t: jnp.concatenate([t, t], axis=0)
    for r in range(XA_REQ):
        q = twice(q_ref[base + r])
        s = jnp.sum(ck_ref[r].reshape(pairs) * q[None], axis=-1, keepdims=True) * scale
        yield
        mx = twice(fold(jnp.max(s, axis=0), jnp.maximum))
        p = jnp.exp(s - mx[None])
        den = twice(fold(jnp.sum(p, axis=0), jnp.add))
        pr = p / den[None]
        yield
        oc_ref[base + r] = fold(jnp.sum(pr * cv_ref[r].reshape(pairs), axis=0), jnp.add)
        yield


def _route_and_dispatch(x1, nf_ref, wrt_ref, br_ref, xs_ref, route_ref, cnt_ref):
    tt = x1.shape[0]
    h2 = _rms(x1, nf_ref[...])
    h_hi, h_mid, h_lo = _split3(h2)
    w_hi, w_mid, w_lo = _split3(wrt_ref[...])
    yield

    def nt(a, b):
        return lax.dot_general(a, b, (((1,), (1,)), ((), ())), preferred_element_type=F32)

    logits = (nt(w_hi, h_hi) + (nt(w_hi, h_mid) + nt(w_mid, h_hi))
              + (nt(w_hi, h_lo) + nt(w_lo, h_hi) + nt(w_mid, h_mid))) + br_ref[:, 0:1]
    yield
    e_id = _iota((LANES, tt), 0).astype(F32)
    neg = jnp.float32(-jnp.inf)
    big = jnp.float32(1 << 20)
    is_grp = (e_id >= N_EXPERTS) & (e_id < N_EXPERTS + N_GROUPS)
    gl = jnp.where(is_grp, logits, neg)
    gmax = jnp.max(gl, axis=0, keepdims=True)
    g_lo = (jnp.min(jnp.where(gl == gmax, e_id, big), axis=0, keepdims=True) - N_EXPERTS) * EXP_PER_GROUP
    p_g = 1.0 / jnp.sum(jnp.exp(gl - gmax), axis=0, keepdims=True)
    yield
    in_grp = (e_id >= g_lo) & (e_id < g_lo + EXP_PER_GROUP)
    el = jnp.where(in_grp, logits, neg)
    v1 = jnp.max(el, axis=0, keepdims=True)
    i1 = jnp.min(jnp.where(el == v1, e_id, big), axis=0, keepdims=True)
    yield
    el2 = jnp.where(e_id == i1, neg, el)
    v2 = jnp.max(el2, axis=0, keepdims=True)
    i2 = jnp.min(jnp.where(el2 == v2, e_id, big), axis=0, keepdims=True)
    yield
    e2 = jnp.exp(v2 - v1)
    w1 = p_g / (1.0 + e2)
    w2 = p_g * e2 / (1.0 + e2)
    oh1 = (e_id == i1).astype(F32)
    oh2 = (e_id == i2).astype(F32)
    a_t = oh1 + oh2
    cnt = jnp.sum(a_t, axis=1, keepdims=True)
    pages = jnp.floor((cnt + (PAGE_ROWS - 1)) * (1.0 / PAGE_ROWS))
    yield
    lower = (_iota((LANES, LANES), 0) > _iota((LANES, LANES), 1)).astype(BF16)
    offp = jnp.dot(lower, jnp.broadcast_to(pages, (LANES, LANES)).astype(BF16),
                   preferred_element_type=F32)[:, 0:1]
    upper = (_iota((tt, tt), 0) < _iota((tt, tt), 1)).astype(BF16)
    rank = jnp.dot(a_t.astype(BF16), upper, preferred_element_type=F32)
    yield
    pos = offp * PAGE_ROWS + rank
    slot1 = jnp.sum(oh1 * pos, axis=0, keepdims=True)
    slot2 = jnp.sum(oh2 * pos, axis=0, keepdims=True)
    yield
    s_id = _iota((TILE_SLOTS, tt), 0).astype(F32)
    sel = jnp.where((s_id == slot1) | (s_id == slot2), 1.0, 0.0).astype(BF16)
    for r0 in range(0, TILE_SLOTS, TILE_SLOTS // 4):
        r1 = r0 + TILE_SLOTS // 4
        xs_ref[r0:r1, :] = jnp.dot(sel[r0:r1], h_hi, preferred_element_type=F32).astype(BF16)
        yield
    r_id = _iota((LANES, tt), 0)
    rows = (jnp.where(r_id == 0, slot1, 0.0) + jnp.where(r_id == 1, slot2, 0.0)
            + jnp.where(r_id == 2, w1, 0.0) + jnp.where(r_id == 3, w2, 0.0))
    route_ref[...] = rows.T
    cnt_ref[...] = jnp.broadcast_to(cnt, (LANES, LANES)).T[0:8, :]


def _merge(x, oa, ob, oc, ga, gb, gc, wo_ref):
    s = ga * oa + gb * ob + gc * oc
    return x + jnp.dot(s.astype(BF16), wo_ref[...], preferred_element_type=F32)


def _mix_kernel(x_ref, au_ref, av_ref, qm_ref, ga_ref, gb_ref, gc_ref, ob0_ref, ob1_ref, mk_ref, mv_ref,
                ws_ref, bst_ref, wo_ref, nf_ref, wrt_ref, br_ref, xq_ref, ck_ref, cv_ref,
                sq_ref, sk_ref, sv_ref, sz_ref, sba_ref, sc_ref, sg_ref, wc_ref, alog_ref, dt_ref, ng_ref,
                x1_ref, xs_ref, route_ref, cnt_ref, xoc_ref, sob_ref, snew_ref, cnew_ref,
                oa_ref, oc_ref, x1_prev_ref):
    i = pl.program_id(0)
    tt = x_ref.shape[0]

    @pl.when(i == 0)
    def _():
        x1_prev_ref[...] = jnp.zeros_like(x1_prev_ref)

    def mix_tile():
        causal = _iota((A_CHUNK, A_CHUNK), 0) >= _iota((A_CHUNK, A_CHUNK), 1)
        for g in range(A_GROUPS):
            wsg = jnp.where(causal, ws_ref[g], 0.0).astype(BF16)
            bcol = bst_ref[:, g:g + 1]
            lo = g * A_CHUNK
            for cc in range(tt // A_CHUNK):
                r0 = cc * A_CHUNK
                sv = jnp.dot(wsg, av_ref[r0:r0 + A_CHUNK, lo:lo + A_CHUNK].astype(BF16),
                             preferred_element_type=F32) + bcol
                oa_ref[r0:r0 + A_CHUNK, lo:lo + A_CHUNK] = au_ref[r0:r0 + A_CHUNK, lo:lo + A_CHUNK] * sv
            yield
        scale = M_HEAD_DIM ** -0.5
        for h in range(M_HEADS):
            lo = h * M_HEAD_DIM
            s = _dot_nt(qm_ref[:, lo:lo + M_HEAD_DIM], mk_ref[:, lo:lo + M_HEAD_DIM]) * scale
            yield
            p = jnp.exp(s - jnp.max(s, axis=-1, keepdims=True))
            pr = p / jnp.sum(p, axis=-1, keepdims=True)
            oc_ref[:, lo:lo + M_HEAD_DIM] = _dot(pr, mv_ref[:, lo:lo + M_HEAD_DIM])
            yield
        n_tiles = pl.num_programs(0) - 1
        in_first_half = jnp.minimum(i, n_tiles - 1) < n_tiles // GDN_SEQS
        ob = jnp.where(in_first_half, ob0_ref[...], ob1_ref[...])
        x1 = _merge(x_ref[...], oa_ref[...], ob, oc_ref[...],
                    ga_ref[...], gb_ref[...], gc_ref[...], wo_ref)
        x1_ref[...] = x1
        x1_prev_ref[i % 2] = x1

    n_req_blocks = xq_ref.shape[0] // XA_REQ
    xa_base = jnp.minimum(i, n_req_blocks - 1) * XA_REQ
    _round_robin([_gdn_step_request(sq_ref, sk_ref, sv_ref, sz_ref, sba_ref, sc_ref, sg_ref, wc_ref,
                                    alog_ref, dt_ref, ng_ref, sob_ref, snew_ref, cnew_ref, xa_base, r)
                  for r in range(XA_REQ)]
                 + [_route_and_dispatch(x1_prev_ref[(i + 1) % 2], nf_ref, wrt_ref, br_ref, xs_ref,
                                        route_ref, cnt_ref),
                    mix_tile(),
                    _xattn1_requests(xq_ref, ck_ref, cv_ref, xoc_ref, xa_base)],
                 turns=[2] * XA_REQ + [1, 1, 1])


def _mix1_kernel(x_ref, au_ref, av_ref, ga_ref, gb_ref, gc_ref, ob_ref, oc_ref, ws0_ref, bs0_ref,
                 wo_ref, nf_ref, wrt_ref, br_ref, xs_in_ref,
                 x1_ref, xs_ref, route_ref, cnt_ref):
    del xs_in_ref
    oa = au_ref[...] * (ws0_ref[...] * av_ref[...] + bs0_ref[...])
    s = ga_ref[...] * oa + gb_ref[...] * ob_ref[...] + gc_ref[...] * oc_ref[...]
    s_hi = s.astype(BF16)
    s_lo = (s - s_hi.astype(F32)).astype(BF16)
    w = wo_ref[...]
    w_hi = w.astype(BF16)
    w_lo = (w - w_hi.astype(F32)).astype(BF16)
    x1 = x_ref[...] + (jnp.dot(s_hi, w_hi, preferred_element_type=F32)
                       + (jnp.dot(s_lo, w_hi, preferred_element_type=F32)
                          + jnp.dot(s_hi, w_lo, preferred_element_type=F32)))
    x1_ref[...] = x1
    for _ in _route_and_dispatch(x1, nf_ref, wrt_ref, br_ref, xs_ref, route_ref, cnt_ref):
        pass


def _route_out_specs(tt, x1_of, slot_of, route_of):
    return [pl.BlockSpec((tt, D_MODEL), lambda i: (x1_of(i), 0)),
            pl.BlockSpec((None, TILE_SLOTS, D_MODEL), lambda i: (slot_of(i), 0, 0)),
            pl.BlockSpec((tt, LANES), lambda i: (route_of(i), 0)),
            pl.BlockSpec((None, 8, LANES), lambda i: (route_of(i), 0, 0))]


def _mix(x2d, p, ob, mk, mv, w_s, bs_t, wo_bf16, norm_ffn, wr_t, br_col, seq,
         p_sample, ba_sample, cache_k, cache_v, state_conv, state_gdn, w_conv, alog_row, dt_row, norm_gdn):
    n_s = p_sample.shape[0]
    xq = p_sample[:, 6 * D_MODEL:7 * D_MODEL].reshape(n_s, M_HEADS, M_HEAD_DIM)
    n_req_blocks = n_s // XA_REQ
    req_blk = lambda i: jnp.minimum(i, n_req_blocks - 1)
    cache_spec = pl.BlockSpec((XA_REQ, N_MEM, M_HEADS, M_HEAD_DIM), lambda i: (req_blk(i), 0, 0, 0))
    whole_q = pl.BlockSpec((n_s, M_HEADS, M_HEAD_DIM), lambda i: (0, 0, 0))
    s_col = lambda colblk: pl.BlockSpec((n_s, D_MODEL), lambda i: (0, colblk))
    conv_spec = pl.BlockSpec((XA_REQ, B_CONV - 1, C_QKV), lambda i: (req_blk(i), 0, 0))
    state_spec = pl.BlockSpec((XA_REQ, B_HEADS, B_DK, B_DK), lambda i: (req_blk(i), 0, 0, 0))
    rows = x2d.shape[0]
    assert n_req_blocks <= rows // MIX_TILE + 1 and n_s % XA_REQ == 0
    tt = MIX_TILE
    nt = rows // tt
    per_b = seq // tt
    cur = lambda i: jnp.minimum(i, nt - 1)
    prev = lambda i: jnp.maximum(i - 1, 0)
    pblk = lambda colblk: pl.BlockSpec((tt, D_MODEL), lambda i: (cur(i), colblk))
    const = lambda shape: pl.BlockSpec(shape, lambda i: (0,) * len(shape))
    nh = nt // GDN_SEQS
    ob_blk = lambda g: pl.BlockSpec((tt, D_MODEL), lambda i: (jnp.clip(cur(i) - g * nh, 0, nh - 1), 0))
    x1, xs_all, route, cnt, xoc, sob, snew, cnew = pl.pallas_call(
        _mix_kernel,
        grid=(nt + 1,),
        in_specs=[pblk(0), pblk(0), pblk(1), pblk(6), pblk(7), pblk(8), pblk(9), ob_blk(0), ob_blk(1),
                  pl.BlockSpec((N_MEM, D_MODEL), lambda i: (cur(i) // per_b, 0)),
                  pl.BlockSpec((N_MEM, D_MODEL), lambda i: (cur(i) // per_b, 0)),
                  const((A_GROUPS, A_CHUNK, A_CHUNK)), const((A_CHUNK, LANES)),
                  const((D_MODEL, D_MODEL)), const((1, D_MODEL)),
                  const((LANES, D_MODEL)), const((LANES, LANES)),
                  whole_q, cache_spec, cache_spec,
                  s_col(2), s_col(3), s_col(4), s_col(5), const((n_s, LANES)), conv_spec, state_spec,
                  const((B_CONV, C_QKV)), const((1, LANES)), const((1, LANES)), const((1, B_DK))],
        out_specs=_route_out_specs(tt, cur, lambda i: i, prev)
        + [whole_q, const((n_s, D_MODEL)), state_spec, conv_spec],
        out_shape=[jax.ShapeDtypeStruct((rows, D_MODEL), F32),
                   jax.ShapeDtypeStruct((nt + 1, TILE_SLOTS, D_MODEL), BF16),
                   jax.ShapeDtypeStruct((rows, LANES), F32),
                   jax.ShapeDtypeStruct((nt, 8, LANES), F32),
                   jax.ShapeDtypeStruct((n_s, M_HEADS, M_HEAD_DIM), F32),
                   jax.ShapeDtypeStruct((n_s, D_MODEL), F32),
                   jax.ShapeDtypeStruct((n_s, B_HEADS, B_DK, B_DK), F32),
                   jax.ShapeDtypeStruct((n_s, B_CONV - 1, C_QKV), F32)],
        scratch_shapes=[pltpu.VMEM((tt, D_MODEL), F32), pltpu.VMEM((tt, D_MODEL), F32),
                        pltpu.VMEM((2, tt, D_MODEL), F32)],
        compiler_params=_params(1),
        name="mix",
    )(x2d, p, p, p, p, p, p, ob[0], ob[1], mk, mv, w_s, bs_t, wo_bf16, norm_ffn, wr_t, br_col,
      xq, cache_k, cache_v,
      p_sample, p_sample, p_sample, p_sample, ba_sample, state_conv, state_gdn,
      w_conv, alog_row, dt_row, norm_gdn)
    return x1, xs_all, route, cnt, xoc.reshape(n_s, D_MODEL), sob, snew, cnew


def _mix1(x2d, p, ob, oc, ws0_row, bs0_row, wo_f32, norm_ffn, wr_t, br_col, xs_all):
    rows = x2d.shape[0]
    pblk = lambda colblk: pl.BlockSpec((rows, D_MODEL), lambda i: (0, colblk))
    const = lambda shape: pl.BlockSpec(shape, lambda i: (0,) * len(shape))
    zero = lambda i: 0
    return pl.pallas_call(
        _mix1_kernel,
        grid=(1,),
        in_specs=[pblk(0), pblk(0), pblk(1), pblk(7), pblk(8), pblk(9), pblk(0), pblk(0),
                  const((1, D_MODEL)), const((1, D_MODEL)),
                  const((D_MODEL, D_MODEL)), const((1, D_MODEL)),
                  const((LANES, D_MODEL)), const((LANES, LANES)),
                  pl.BlockSpec(memory_space=pl.ANY)],
        out_specs=_route_out_specs(rows, zero, zero, zero),
        out_shape=[jax.ShapeDtypeStruct((rows, D_MODEL), F32),
                   jax.ShapeDtypeStruct(xs_all.shape, BF16),
                   jax.ShapeDtypeStruct((rows, LANES), F32),
                   jax.ShapeDtypeStruct((1, 8, LANES), F32)],
        input_output_aliases={14: 1},
        compiler_params=_params(1),
        name="mix1",
    )(x2d, p, p, p, p, p, ob, oc, ws0_row, bs0_row, wo_f32, norm_ffn, wr_t, br_col, xs_all)


def _experts_kernel(tbl_ref, se_ref, nu_ref, *refs):
    del tbl_ref
    pages = refs[:STEP_PAGES]
    wg_ref, wu_ref, wd_ref, o_ref, wgb_ref, wub_ref, wdb_ref = refs[STEP_PAGES:]
    s = pl.program_id(0)
    prev = se_ref[jnp.maximum(s - 1, 0)]

    @pl.when((s == 0) | (se_ref[s] != prev))
    def _():
        wgb_ref[...] = wg_ref[...].astype(BF16)
        wub_ref[...] = wu_ref[...].astype(BF16)
        wdb_ref[...] = wd_ref[...].astype(BF16)

    @pl.when(s < nu_ref[0])
    def _():
        x = jnp.concatenate([pg[...] for pg in pages], axis=0)
        gate = jnp.dot(x, wgb_ref[...], preferred_element_type=F32)
        up = jnp.dot(x, wub_ref[...], preferred_element_type=F32)
        act = (jax.nn.silu(gate) * up).astype(BF16)
        o_ref[...] = jnp.dot(act, wdb_ref[...], preferred_element_type=F32).astype(BF16)

    @pl.when(s >= nu_ref[0])
    def _():
        o_ref[...] = jnp.zeros_like(o_ref)


def _experts(tbl, step_e, n_used, xs_pages, w_gate, w_up, w_down, n_steps):
    def page_spec(k):
        return pl.BlockSpec((None, PAGE_ROWS, D_MODEL),
                            lambda s, tbl, se, nu: (tbl[s * STEP_PAGES + k], 0, 0))

    wspec = lambda shape: pl.BlockSpec((None,) + shape, lambda s, tbl, se, nu: (se[s], 0, 0))
    grid_spec = pltpu.PrefetchScalarGridSpec(
        num_scalar_prefetch=3,
        grid=(n_steps,),
        in_specs=[page_spec(k) for k in range(STEP_PAGES)]
        + [wspec((D_MODEL, D_EXPERT)), wspec((D_MODEL, D_EXPERT)), wspec((D_EXPERT, D_MODEL))],
        out_specs=pl.BlockSpec((STEP_ROWS, D_MODEL), lambda s, tbl, se, nu: (s, 0)),
        scratch_shapes=[pltpu.VMEM((D_MODEL, D_EXPERT), BF16), pltpu.VMEM((D_MODEL, D_EXPERT), BF16),
                        pltpu.VMEM((D_EXPERT, D_MODEL), BF16)],
    )
    return pl.pallas_call(
        _experts_kernel,
        grid_spec=grid_spec,
        out_shape=jax.ShapeDtypeStruct((n_steps * STEP_ROWS, D_MODEL), BF16),
        compiler_params=_params(1),
        name="experts",
    )(tbl, step_e, n_used, *([xs_pages] * STEP_PAGES), w_gate, w_up, w_down)


def _combine_kernel(inv_ref, *refs):
    del inv_ref
    pages = refs[:TILE_PAGES]
    x1_ref, route_ref, nfin_ref, y_ref = refs[TILE_PAGES:]
    out_loc = jnp.concatenate([pg[...] for pg in pages], axis=0)
    tt = x1_ref.shape[0]
    route = route_ref[...]
    s_id = _iota((tt, TILE_SLOTS), 1).astype(F32)
    sel = (jnp.where(s_id == route[:, 0:1], route[:, 2:3], 0.0)
           + jnp.where(s_id == route[:, 1:2], route[:, 3:4], 0.0)).astype(BF16)
    moe = jnp.dot(sel, out_loc, preferred_element_type=F32)
    y_ref[...] = _rms(x1_ref[...] + moe, nfin_ref[...])


def _combine(inv, out_pages, x1, route, norm_final, tt):
    rows = x1.shape[0]

    def page_spec(k):
        return pl.BlockSpec((None, PAGE_ROWS, D_MODEL), lambda i, inv: (inv[i * TILE_PAGES + k], 0, 0))

    grid_spec = pltpu.PrefetchScalarGridSpec(
        num_scalar_prefetch=1,
        grid=(rows // tt,),
        in_specs=[page_spec(k) for k in range(TILE_PAGES)]
        + [pl.BlockSpec((tt, D_MODEL), lambda i, inv: (i, 0)),
           pl.BlockSpec((tt, LANES), lambda i, inv: (i, 0)),
           pl.BlockSpec((1, D_MODEL), lambda i, inv: (0, 0))],
        out_specs=pl.BlockSpec((tt, D_MODEL), lambda i, inv: (i, 0)),
    )
    return pl.pallas_call(
        _combine_kernel,
        grid_spec=grid_spec,
        out_shape=jax.ShapeDtypeStruct((rows, D_MODEL), F32),
        compiler_params=_params(1),
        name="combine",
    )(inv, *([out_pages] * TILE_PAGES), x1, route, norm_final)


def _page_tables(cnt, n_steps):
    n_tiles = cnt.shape[0]
    pg = (cnt + (PAGE_ROWS - 1)) // PAGE_ROWS
    lend = jnp.cumsum(pg, axis=1)
    loff = lend - pg
    cum_t = jnp.cumsum(pg, axis=0)
    pref = cum_t - pg
    tot = cum_t[-1]
    totp = ((tot + (STEP_PAGES - 1)) // STEP_PAGES) * STEP_PAGES
    gend = jnp.cumsum(totp)
    gstart = gend - totp
    sel = lambda onehot, vals: jnp.dot(onehot, vals.astype(F32), precision=lax.Precision.HIGHEST)
    gp = jnp.arange(n_steps * STEP_PAGES, dtype=jnp.int32)
    in_e = (gp[:, None] >= gstart[None, :]) & (gp[:, None] < gend[None, :])
    oh_e = in_e.astype(F32)
    r = gp.astype(F32) - sel(oh_e, gstart)
    valid = jnp.any(in_e, axis=1) & (r < sel(oh_e, tot))
    cum_col = sel(oh_e, cum_t.T)
    pref_col = sel(oh_e, pref.T)
    loff_col = sel(oh_e, loff.T)
    in_i = (r[:, None] >= pref_col) & (r[:, None] < cum_col)
    tile_base = (jnp.arange(n_tiles, dtype=jnp.int32) * TILE_PAGES).astype(F32)
    src = jnp.sum(jnp.where(in_i, tile_base[None, :] + loff_col + (r[:, None] - pref_col), 0.0), axis=1)
    tbl = jnp.where(valid, src, 0.0).astype(jnp.int32)
    in_step = in_e[::STEP_PAGES]
    e_ids = jnp.arange(N_EXPERTS, dtype=jnp.int32)
    step_e = jnp.where(jnp.any(in_step, axis=1), jnp.sum(jnp.where(in_step, e_ids[None, :], 0), axis=1),
                       N_EXPERTS - 1).astype(jnp.int32)
    n_used = (gend[-1] // STEP_PAGES).astype(jnp.int32).reshape(1)
    lp = jnp.arange(TILE_PAGES, dtype=jnp.int32)[None, :, None]
    in_l = (lp >= loff[:, None, :]) & (lp < lend[:, None, :])
    gpos = jnp.sum(jnp.where(in_l, gstart[None, None, :] + pref[:, None, :] + lp - loff[:, None, :], 0), axis=2)
    zero_page = (n_steps - 1) * STEP_PAGES
    inv = jnp.where(jnp.any(in_l, axis=2), gpos, zero_page).astype(jnp.int32).reshape(-1)
    return tbl, step_e, n_used, inv


def kernel(x_prompt, x_sample, mem_prompt, cache_mem_k, cache_mem_v, state_gdn, state_conv,
           norm_mix, w_in, b_gate, w_s, b_s, norm_a_v, w_conv, a_log, dt_bias, norm_gdn_out,
           norm_mem, w_mem_kv, w_o, norm_ffn, w_router_group, b_router_group, w_router_expert,
           b_router_expert, w_exp_gate, w_exp_up, w_exp_down, norm_final):
    depth = norm_mix.shape[0]
    assert depth == 1
    bsz, seq, _ = x_prompt.shape
    n_s = x_sample.shape[0]
    assert x_sample.shape[1] == 1 and seq % MIX_TILE == 0 and n_s % 8 == 0 and n_s <= MIX_TILE
    assert bsz % GDN_SEQS == 0
    l = 0
    row = lambda v: v.reshape(1, -1)

    wi = w_in[l]
    o_z = 2 * D_MODEL + C_QKV
    o_beta = o_z + D_MODEL
    o_qm = o_beta + 2 * B_HEADS
    o_gate = o_qm + D_MODEL
    assert o_beta == PROJ_HEAD_TILES * D_MODEL
    w_tail = wi[:, o_qm:]
    w_main = jnp.concatenate([wi[:, :o_beta].astype(BF16), w_tail.astype(BF16)], axis=1)
    w_ba_f32 = jnp.pad(wi[:, o_beta:o_qm], ((0, 0), (0, LANES - 2 * B_HEADS)))
    w_ba = w_ba_f32.astype(BF16)
    pad_heads = lambda v: jnp.pad(v.reshape(1, B_HEADS), ((0, 0), (B_HEADS, LANES - 2 * B_HEADS)))
    alog_row = pad_heads(a_log[l])
    dt_row = pad_heads(dt_bias[l])
    w_kv = w_mem_kv[l].astype(BF16)
    wo = w_o[l].astype(BF16)
    wr_t = jnp.pad(jnp.concatenate([w_router_expert[l], w_router_group[l]], axis=1).T,
                   ((0, LANES - N_EXPERTS - N_GROUPS), (0, 0)))
    br_col = jnp.pad(jnp.concatenate([b_router_expert[l], b_router_group[l]]).reshape(-1, 1),
                     ((0, LANES - N_EXPERTS - N_GROUPS), (0, LANES - 1)))
    bs_t = jnp.pad(b_s[l].T, ((0, 0), (0, LANES - A_GROUPS)))
    ws0_row = jnp.repeat(w_s[l][:, 0, 0], A_CHUNK).reshape(1, D_MODEL)
    bs0_row = jnp.repeat(b_s[l][:, 0], A_CHUNK).reshape(1, D_MODEL)

    xp = x_prompt.reshape(bsz * seq, D_MODEL)
    xs_ = x_sample.reshape(n_s, D_MODEL)
    n_tiles_p = (bsz * seq) // MIX_TILE
    n_tiles = n_tiles_p + 1

    p_s, ba_s = _proj2(xs_, row(norm_mix[l]), wi, w_tail, w_ba_f32, row(norm_a_v[l]), row(b_gate[l]))
    mk, mv = _memkv(mem_prompt.reshape(bsz * N_MEM, D_MODEL), row(norm_mem[l]), w_kv)
    p_p, ba_p = _proj(xp, row(norm_mix[l]), w_main, w_ba, row(norm_a_v[l]), row(b_gate[l]),
                      tm=min(PROJ_ROWS, bsz * seq))
    ob_p, s_p = _gdn(p_p, ba_p, w_conv[l], alog_row, dt_row, row(norm_gdn_out[l]), bsz, seq)
    x1_p, xs_all, route_p, cnt_p, oc_s, ob_s, s_s, c_s = _mix(
        xp, p_p, ob_p, mk, mv, w_s[l], bs_t, wo, row(norm_ffn[l]), wr_t, br_col, seq,
        p_s, ba_s, cache_mem_k[l], cache_mem_v[l], state_conv[l], state_gdn[l], w_conv[l], alog_row, dt_row,
        row(norm_gdn_out[l]))
    x1_s, xs_all, route_s, cnt_s = _mix1(xs_, p_s, ob_s, oc_s, ws0_row, bs0_row, w_o[l], row(norm_ffn[l]),
                                         wr_t, br_col, xs_all)
    cnt = jnp.concatenate([cnt_s[:, 0, :N_EXPERTS], cnt_p[:, 0, :N_EXPERTS]], axis=0).astype(jnp.int32)
    max_pages = n_tiles_p * TILE_PAGES + (2 * n_s) // PAGE_ROWS + N_EXPERTS
    n_steps = (max_pages + N_EXPERTS * (STEP_PAGES - 1)) // STEP_PAGES + 1
    tbl, step_e, n_used, inv = _page_tables(cnt, n_steps)
    ne = N_GROUPS * EXP_PER_GROUP
    out_sorted = _experts(tbl, step_e, n_used, xs_all.reshape(n_tiles * TILE_PAGES, PAGE_ROWS, D_MODEL),
                          w_exp_gate[l].reshape(ne, D_MODEL, D_EXPERT),
                          w_exp_up[l].reshape(ne, D_MODEL, D_EXPERT),
                          w_exp_down[l].reshape(ne, D_EXPERT, D_MODEL), n_steps)
    out_pages = out_sorted.reshape(n_steps * STEP_PAGES, PAGE_ROWS, D_MODEL)
    y_s = _combine(inv[:TILE_PAGES], out_pages, x1_s, route_s, row(norm_final), n_s)
    y_p = _combine(inv[TILE_PAGES:], out_pages, x1_p, route_p, row(norm_final), MIX_TILE)

    conv_tail = p_p.reshape(bsz, seq, PROJ_COLS)[:, seq - (B_CONV - 1):, 2 * D_MODEL:2 * D_MODEL + C_QKV]
    return (y_p.reshape(bsz, seq, D_MODEL),
            y_s.reshape(n_s, 1, D_MODEL),
            mk.reshape(1, bsz, N_MEM, M_HEADS, M_HEAD_DIM),
            mv.reshape(1, bsz, N_MEM, M_HEADS, M_HEAD_DIM),
            jnp.concatenate(s_p, axis=0)[None],
            conv_tail[None],
            s_s[None],
            c_s[None],
            p_s[:, D_MODEL:2 * D_MODEL].reshape(1, n_s, 1, D_MODEL))
```

```python
import functools

import jax
import jax.numpy as jnp
from jax import lax
from jax.experimental import pallas as pl
from jax.experimental.pallas import tpu as pltpu

F32 = jnp.float32
BF16 = jnp.bfloat16

D_MODEL = 1024
A_GROUPS = 8
A_CHUNK = 128
B_HEADS = 8
B_DK = 128
C_QKV = 3 * D_MODEL
B_CONV = 4
N_MEM = 256
M_HEADS = 4
M_HEAD_DIM = 256
N_GROUPS = 4
EXP_PER_GROUP = 8
N_EXPERTS = 32
D_EXPERT = 512
EPS = 1e-6

LANES = 128
GDN_CHUNK = 128
GDN_SEQS = 2
MIX_TILE = 256
PAGE_ROWS = 16
TILE_PAGES = 64
TILE_SLOTS = PAGE_ROWS * TILE_PAGES
STEP_PAGES = 32
STEP_ROWS = PAGE_ROWS * STEP_PAGES
PROJ_COLS = 10 * D_MODEL
PROJ_HEAD_TILES = 6
PROJ_CHUNK = 256
PROJ_ROWS = 2048
PROJ_PIECE_ROWS = 512
VMEM_LIMIT = 56 * 1024 * 1024


def _params(n_grid):
    return pltpu.CompilerParams(dimension_semantics=("arbitrary",) * n_grid,
                                vmem_limit_bytes=VMEM_LIMIT)


def _rms(x, g):
    return x * lax.rsqrt(jnp.mean(x * x, axis=-1, keepdims=True) + EPS) * g


def _dot(a, b):
    return jnp.dot(a.astype(BF16), b.astype(BF16), preferred_element_type=F32)


def _dot_nt(a, b):
    return lax.dot_general(a.astype(BF16), b.astype(BF16), (((1,), (1,)), ((), ())),
                           preferred_element_type=F32)


def _dot_tn(a, b):
    return lax.dot_general(a.astype(BF16), b.astype(BF16), (((0,), (0,)), ((), ())),
                           preferred_element_type=F32)


def _split3(x):
    hi = x.astype(BF16)
    r = x - hi.astype(F32)
    mid = r.astype(BF16)
    lo = (r - mid.astype(F32)).astype(BF16)
    return hi, mid, lo


def _iota(shape, axis):
    return lax.broadcasted_iota(jnp.int32, shape, axis)


def _memkv_kernel(mem_ref, g_ref, w_ref, k_ref, v_ref):
    h = _rms(mem_ref[...], g_ref[...]).astype(BF16)
    kv = jnp.dot(h, w_ref[...], preferred_element_type=F32)
    k_ref[...] = kv[:, :D_MODEL]
    v_ref[...] = kv[:, D_MODEL:]


def _memkv(mem2d, norm_mem, w_kv_bf16):
    rows = mem2d.shape[0]
    nb = rows // N_MEM
    return pl.pallas_call(
        _memkv_kernel,
        grid=(nb,),
        in_specs=[pl.BlockSpec((N_MEM, D_MODEL), lambda i: (i, 0)),
                  pl.BlockSpec((1, D_MODEL), lambda i: (0, 0)),
                  pl.BlockSpec((D_MODEL, 2 * D_MODEL), lambda i: (0, 0))],
        out_specs=[pl.BlockSpec((N_MEM, D_MODEL), lambda i: (i, 0)),
                   pl.BlockSpec((N_MEM, D_MODEL), lambda i: (i, 0))],
        out_shape=[jax.ShapeDtypeStruct((rows, D_MODEL), F32)] * 2,
        compiler_params=_params(1),
        name="memkv",
    )(mem2d, norm_mem, w_kv_bf16)


def _proj_kernel(x_ref, nm_ref, w_ref, wba_ref, nav_ref, bg_ref, p_ref, ba_ref, hn_ref):
    j = pl.program_id(1)

    @pl.when(j == 0)
    def _():
        hn = _rms(x_ref[...], nm_ref[...]).astype(BF16)
        hn_ref[...] = hn
        ba_ref[...] = jnp.dot(hn, wba_ref[...], preferred_element_type=F32)

    tm = x_ref.shape[0]
    rows = [(r0, min(tm, r0 + PROJ_PIECE_ROWS)) for r0 in range(0, tm, PROJ_PIECE_ROWS)]
    pieces = [(r, c0) for r in rows for c0 in range(0, D_MODEL, PROJ_CHUNK)]

    def chunks(epilogue):
        def matmul(piece):
            (r0, r1), c0 = piece
            return jnp.dot(hn_ref[r0:r1, :], w_ref[:, c0:c0 + PROJ_CHUNK], preferred_element_type=F32)

        def finish(piece, acc):
            (r0, r1), c0 = piece
            p_ref[r0:r1, c0:c0 + PROJ_CHUNK] = epilogue(acc, piece)

        acc = matmul(pieces[0])
        for n in range(1, len(pieces)):
            nxt = matmul(pieces[n])
            finish(pieces[n - 1], acc)
            acc = nxt
        finish(pieces[-1], acc)

    @pl.when(j == 0)
    def _():
        chunks(lambda acc, piece: jax.nn.gelu(acc))

    @pl.when(j == 1)
    def _():
        ssq = {r: [] for r in rows}

        def gelu_ssq(acc, piece):
            a = jax.nn.gelu(acc)
            ssq[piece[0]].append(jnp.sum(a * a, axis=-1, keepdims=True))
            return a

        chunks(gelu_ssq)
        for r0, r1 in rows:
            inv = lax.rsqrt(sum(ssq[(r0, r1)]) * (1.0 / D_MODEL) + EPS)
            p_ref[r0:r1, :] = p_ref[r0:r1, :] * inv * nav_ref[...]

    @pl.when((j >= 2) & (j <= 6))
    def _():
        chunks(lambda acc, piece: acc)

    @pl.when(j >= 7)
    def _():
        chunks(lambda acc, piece: jax.nn.sigmoid(acc + bg_ref[:, piece[1]:piece[1] + PROJ_CHUNK]))


def _proj(x2d, norm_mix, w_bf16, wba_bf16, norm_a_v, b_gate, tm):
    rows = x2d.shape[0]
    ncol = PROJ_COLS // D_MODEL
    return pl.pallas_call(
        _proj_kernel,
        grid=(rows // tm, ncol),
        in_specs=[pl.BlockSpec((tm, D_MODEL), lambda i, j: (i, 0)),
                  pl.BlockSpec((1, D_MODEL), lambda i, j: (0, 0)),
                  pl.BlockSpec((D_MODEL, D_MODEL), lambda i, j: (0, j)),
                  pl.BlockSpec((D_MODEL, LANES), lambda i, j: (0, 0)),
                  pl.BlockSpec((1, D_MODEL), lambda i, j: (0, 0)),
                  pl.BlockSpec((1, D_MODEL), lambda i, j: (0, jnp.maximum(j - 7, 0)))],
        out_specs=[pl.BlockSpec((tm, D_MODEL), lambda i, j: (i, j)),
                   pl.BlockSpec((tm, LANES), lambda i, j: (i, 0))],
        out_shape=[jax.ShapeDtypeStruct((rows, PROJ_COLS), F32),
                   jax.ShapeDtypeStruct((rows, LANES), F32)],
        scratch_shapes=[pltpu.VMEM((tm, D_MODEL), BF16)],
        compiler_params=_params(2),
        name="proj",
    )(x2d, norm_mix, w_bf16, wba_bf16, norm_a_v, b_gate)


def _proj2_kernel(x_ref, nm_ref, w_head_ref, w_tail_ref, wba_ref, nav_ref, bg_ref, p_ref, ba_ref,
                  hh_ref, hl_ref):
    j = pl.program_id(1)

    def dot2(w_f32_ref):
        w = w_f32_ref[...]
        w_hi = w.astype(BF16)
        w_lo = (w - w_hi.astype(F32)).astype(BF16)
        return (jnp.dot(hh_ref[...], w_hi, preferred_element_type=F32)
                + (jnp.dot(hl_ref[...], w_hi, preferred_element_type=F32)
                   + jnp.dot(hh_ref[...], w_lo, preferred_element_type=F32)))

    @pl.when(j == 0)
    def _():
        h = _rms(x_ref[...], nm_ref[...])
        hi = h.astype(BF16)
        hh_ref[...] = hi
        hl_ref[...] = (h - hi.astype(F32)).astype(BF16)
        ba_ref[...] = dot2(wba_ref)

    @pl.when(j == 0)
    def _():
        p_ref[...] = jax.nn.gelu(dot2(w_head_ref))

    @pl.when(j == 1)
    def _():
        p_ref[...] = _rms(jax.nn.gelu(dot2(w_head_ref)), nav_ref[...])

    @pl.when((j >= 2) & (j < PROJ_HEAD_TILES))
    def _():
        p_ref[...] = dot2(w_head_ref)

    @pl.when(j == PROJ_HEAD_TILES)
    def _():
        p_ref[...] = dot2(w_tail_ref)

    @pl.when(j > PROJ_HEAD_TILES)
    def _():
        p_ref[...] = jax.nn.sigmoid(dot2(w_tail_ref) + bg_ref[...])


def _proj2(x2d, norm_mix, w_in, w_tail, wba_f32, norm_a_v, b_gate):
    rows = x2d.shape[0]
    ncol = PROJ_COLS // D_MODEL
    w_head_spec = pl.BlockSpec((D_MODEL, D_MODEL), lambda i, j: (0, jnp.minimum(j, PROJ_HEAD_TILES - 1)))
    w_tail_spec = pl.BlockSpec((D_MODEL, D_MODEL), lambda i, j: (0, jnp.maximum(j - PROJ_HEAD_TILES, 0)))
    wba = pl.BlockSpec((D_MODEL, LANES), lambda i, j: (0, 0))
    vec = pl.BlockSpec((1, D_MODEL), lambda i, j: (0, 0))
    return pl.pallas_call(
        _proj2_kernel,
        grid=(1, ncol),
        in_specs=[pl.BlockSpec((rows, D_MODEL), lambda i, j: (0, 0)), vec, w_head_spec, w_tail_spec, wba,
                  vec, pl.BlockSpec((1, D_MODEL), lambda i, j: (0, jnp.maximum(j - 7, 0)))],
        out_specs=[pl.BlockSpec((rows, D_MODEL), lambda i, j: (0, j)),
                   pl.BlockSpec((rows, LANES), lambda i, j: (0, 0))],
        out_shape=[jax.ShapeDtypeStruct((rows, PROJ_COLS), F32),
                   jax.ShapeDtypeStruct((rows, LANES), F32)],
        scratch_shapes=[pltpu.VMEM((rows, D_MODEL), BF16), pltpu.VMEM((rows, D_MODEL), BF16)],
        compiler_params=_params(2),
        name="proj2",
    )(x2d, norm_mix, w_in, w_tail, wba_f32, norm_a_v, b_gate)


def _gate_terms(ba, alog_row, dt_row):
    beta = jax.nn.sigmoid(ba)
    g = -jnp.exp(alog_row) * jax.nn.softplus(ba + dt_row)
    return beta, g


class _Lazy:
    def __init__(self, make):
        self._make = make

    def __getitem__(self, index):
        return self._make(index)


def _round_robin(tasks, turns=None):
    done = object()
    turns = turns or [1] * len(tasks)
    live = list(zip(tasks, turns))
    while live:
        live = [(t, n) for t, n in live if all(next(t, done) is not done for _ in range(n))]


def _gdn_kernel(n_c, *refs):
    per = 5
    q_refs, k_refs, v_refs, ba_refs, z_refs = (
        [refs[per * g + j] for g in range(GDN_SEQS)] for j in range(per))
    wc_ref, alog_ref, dt_ref, ng_ref = refs[per * GDN_SEQS:per * GDN_SEQS + 4]
    ob_refs = refs[per * GDN_SEQS + 4:per * GDN_SEQS + 4 + GDN_SEQS]
    s_out_refs = refs[per * GDN_SEQS + 4 + GDN_SEQS:per * GDN_SEQS + 4 + 2 * GDN_SEQS]
    s_ref, ext_ref, prep_ref, gate_ref = refs[per * GDN_SEQS + 4 + 2 * GDN_SEQS:]
    s = pl.program_id(0)
    C = GDN_CHUNK
    par = s % 2
    heads = range(B_HEADS)

    @pl.when(s == 0)
    def _():
        s_ref[...] = jnp.zeros_like(s_ref)
        ext_ref[:, 0:8, :] = jnp.zeros((GDN_SEQS, 8, C_QKV), F32)
        prep_ref[...] = jnp.zeros_like(prep_ref)
        gate_ref[...] = jnp.zeros_like(gate_ref)

    row = _iota((C, C), 0)
    col = _iota((C, C), 1)
    incl = row >= col
    pace = []

    def mark(results):
        pace.append(results[-1][0:1, 0:B_DK] * 0.0)

    def prepare(g):
        q_ref, k_ref, v_ref, ba_ref = q_refs[g], k_refs[g], v_refs[g], ba_refs[g]
        keep = (s % n_c != 0).astype(F32)
        ext_ref[g, 0:8, :] = ext_ref[g, 0:8, :] * keep
        ext_ref[g, 8:8 + C, 0:D_MODEL] = q_ref[...]
        ext_ref[g, 8:8 + C, D_MODEL:2 * D_MODEL] = k_ref[...]
        ext_ref[g, 8:8 + C, 2 * D_MODEL:3 * D_MODEL] = v_ref[...]
        beta_all, g_all = _gate_terms(ba_ref[...], alog_ref[...], dt_ref[...])
        g_hi, g_mid, g_lo = _split3(g_all)
        tri = incl.astype(BF16)
        gcum = (jnp.dot(tri, g_hi, preferred_element_type=F32)
                + jnp.dot(tri, g_mid, preferred_element_type=F32)
                + jnp.dot(tri, g_lo, preferred_element_type=F32))
        gate_ref[par, g, 0] = beta_all
        gate_ref[par, g, 1] = gcum
        gate_ref[par, g, 2] = gcum.T
        yield

        def conv_silu(lo):
            pace_row = pace[-1] if pace else 0.0
            acc = (wc_ref[3:4, lo:lo + B_DK] + pace_row) * ext_ref[g, 8:8 + C, lo:lo + B_DK]
            for j in range(1, B_CONV):
                acc = acc + ((wc_ref[3 - j:4 - j, lo:lo + B_DK] + pace_row)
                             * ext_ref[g, 8 - j:8 - j + C, lo:lo + B_DK])
            return jax.nn.silu(acc)

        def l2n(t):
            return t * lax.rsqrt(jnp.sum(t * t, axis=-1, keepdims=True) + EPS)

        for h in heads:
            lo = h * B_DK
            prep_ref[par, g, :, lo:lo + B_DK] = l2n(conv_silu(lo)) * (B_DK ** -0.5)
            yield
            prep_ref[par, g, :, D_MODEL + lo:D_MODEL + lo + B_DK] = l2n(conv_silu(D_MODEL + lo))
            yield
            prep_ref[par, g, :, 2 * D_MODEL + lo:2 * D_MODEL + lo + B_DK] = conv_silu(2 * D_MODEL + lo)
            yield
        ext_ref[g, 0:8, :] = ext_ref[g, C:C + 8, :]

    def chain():
        heads = range(GDN_SEQS * B_HEADS)
        half = [u // B_HEADS for u in heads]
        head = [u % B_HEADS for u in heads]
        old = 1 - par
        strict = row > col
        eye = (row == col).astype(F32)
        same_blk = [(row // b) == (col // b) for b in (8, 16, 32, 64, 128)]
        beta_all = [gate_ref[old, g, 0] for g in range(GDN_SEQS)]
        gcum = [gate_ref[old, g, 1] for g in range(GDN_SEQS)]
        gcum_t = [gate_ref[old, g, 2] for g in range(GDN_SEQS)]
        lanes = lambda part, u: slice(part * D_MODEL + head[u] * B_DK, part * D_MODEL + (head[u] + 1) * B_DK)
        q = _Lazy(lambda u: prep_ref[old, half[u], :, lanes(0, u)])
        k = _Lazy(lambda u: prep_ref[old, half[u], :, lanes(1, u)])
        v = _Lazy(lambda u: prep_ref[old, half[u], :, lanes(2, u)])
        bcol = [beta_all[half[u]][:, head[u]:head[u] + 1] for u in heads]
        gcol = [gcum[half[u]][:, 8 + head[u]:9 + head[u]] for u in heads]
        glast = [gcum[half[u]][C - 1:C, 8 + head[u]:9 + head[u]] for u in heads]
        eg = [jnp.exp(gcol[h]) for h in heads]
        kb = _Lazy(lambda u: k[u] * bcol[u])
        yield
        a1 = [_dot_nt(jnp.concatenate([kb[h], q[h]], axis=0), k[h]) for h in heads]
        mark(a1)
        yield
        m, attn = [], []
        for u in heads:
            grow = gcum_t[half[u]][8 + head[u]:9 + head[u], :]
            dec = jnp.where(incl, jnp.exp(jnp.where(incl, gcol[u] - grow, 0.0)), 0.0)
            m.append(jnp.where(strict, a1[u][:C] * dec, 0.0))
            attn.append(a1[u][C:] * dec)
        m0 = [jnp.where(same_blk[0], m[h], 0.0) for h in heads]
        x = [eye - m0[h] for h in heads]
        pw = [_dot(m0[h], m0[h]) for h in heads]
        mark(pw)
        yield
        x = [x[h] + _dot(x[h], pw[h]) for h in heads]
        mark(x)
        yield
        pw = [_dot(pw[h], pw[h]) for h in heads]
        mark(pw)
        yield
        x = [x[h] + _dot(x[h], pw[h]) for h in heads]
        mark(x)
        yield
        for lvl in range(1, len(same_blk)):
            in_lvl = same_blk[lvl] & jnp.logical_not(same_blk[lvl - 1])
            xb = [x[h].astype(BF16) for h in heads]
            t = [_dot(xb[h], jnp.where(in_lvl, m[h], 0.0)) for h in heads]
            mark(t)
            yield
            x = [x[h] - _dot(t[h], xb[h]) for h in heads]
            mark(x)
            yield
        rhs = [jnp.concatenate([v[h] * bcol[h], kb[h] * eg[h]], axis=1) for h in heads]
        sol = [_dot(x[h], rhs[h]) for h in heads]
        mark(sol)
        yield
        keep = ((s + n_c - 1) % n_c != 0).astype(F32)
        s_old = [s_ref[half[u], head[u]] * keep for u in heads]
        b1 = [_dot(jnp.concatenate([sol[h][:, B_DK:], q[h] * eg[h]], axis=0), s_old[h]) for h in heads]
        mark(b1)
        yield
        u = [sol[h][:, :B_DK] - b1[h][:C] for h in heads]
        o = [b1[h][C:] + _dot(attn[h], u[h]) for h in heads]
        kd = [k[h] * jnp.exp(glast[h] - gcol[h]) for h in heads]
        mark(o)
        yield
        s_new = [s_old[h] * jnp.exp(glast[h]) + _dot_tn(kd[h], u[h]) for h in heads]
        mark(s_new)
        yield
        for u in heads:
            g, h = half[u], head[u]
            s_ref[g, h] = s_new[u]
            s_out_refs[g][h] = s_new[u]
            zh = z_refs[g][:, h * B_DK:(h + 1) * B_DK]
            ob_refs[g][:, h * B_DK:(h + 1) * B_DK] = _rms(o[u], ng_ref[...]) * jax.nn.silu(zh)
            yield

    done = object()
    chain_task = chain()
    prep_tasks = [prepare(g) for g in range(GDN_SEQS)]
    for _ in range(3):
        next(chain_task)
    alive = True
    while alive:
        alive = False
        for task in prep_tasks * 3 + [chain_task]:
            alive = (next(task, done) is not done) or alive


def _gdn(p, ba, w_conv, alog_row, dt_row, norm_gdn, bsz, seq):
    nc = seq // GDN_CHUNK
    total = (bsz // GDN_SEQS) * nc
    cur = lambda s: jnp.minimum(s, total - 1)
    prev = lambda s: jnp.maximum(s - 1, 0)
    small = lambda shape: pl.BlockSpec(shape, lambda s: (0, 0))

    def half_specs(g):
        blk = lambda colblk: pl.BlockSpec((GDN_CHUNK, D_MODEL), lambda s: (g * total + cur(s), colblk))
        return [blk(2), blk(3), blk(4),
                pl.BlockSpec((GDN_CHUNK, LANES), lambda s: (g * total + cur(s), 0)),
                pl.BlockSpec((GDN_CHUNK, D_MODEL), lambda s: (g * total + prev(s), 5))]

    halves = range(GDN_SEQS)
    outs = pl.pallas_call(
        functools.partial(_gdn_kernel, nc),
        grid=(total + 1,),
        in_specs=[spec for g in halves for spec in half_specs(g)]
        + [small((B_CONV, C_QKV)), small((1, LANES)), small((1, LANES)), small((1, B_DK))],
        out_specs=[pl.BlockSpec((GDN_CHUNK, D_MODEL), lambda s: (prev(s), 0)) for g in halves]
        + [pl.BlockSpec((None, B_HEADS, B_DK, B_DK), lambda s: (prev(s) // nc, 0, 0, 0)) for g in halves],
        out_shape=[jax.ShapeDtypeStruct((total * GDN_CHUNK, D_MODEL), F32) for g in halves]
        + [jax.ShapeDtypeStruct((bsz // GDN_SEQS, B_HEADS, B_DK, B_DK), F32) for g in halves],
        scratch_shapes=[pltpu.VMEM((GDN_SEQS, B_HEADS, B_DK, B_DK), F32),
                        pltpu.VMEM((GDN_SEQS, 8 + GDN_CHUNK, C_QKV), F32),
                        pltpu.VMEM((2, GDN_SEQS, GDN_CHUNK, C_QKV), F32),
                        pltpu.VMEM((2, GDN_SEQS, 3, GDN_CHUNK, LANES), F32)],
        compiler_params=_params(1),
        name="gdn",
    )(*([p, p, p, ba, p] * GDN_SEQS), w_conv, alog_row, dt_row, norm_gdn)
    return outs[:GDN_SEQS], outs[GDN_SEQS:]


def _gdn_step_request(q_ref, k_ref, v_ref, z_ref, ba_ref, sc_ref, sg_ref, wc_ref, alog_ref, dt_ref,
                      ng_ref, ob_ref, snew_ref, cnew_ref, base, r):
    def l2n(t):
        return t * lax.rsqrt(jnp.sum(t * t, axis=-1, keepdims=True) + EPS)

    def column(t):
        return jnp.broadcast_to(t, (B_DK, B_DK)).T

    row = pl.ds(base + r, 1)
    beta_all, g_all = _gate_terms(ba_ref[row, :], alog_ref[...], dt_ref[...])
    eg_all = jnp.exp(g_all)
    pre = jnp.concatenate([q_ref[row, :], k_ref[row, :], v_ref[row, :]], axis=1)
    conv = wc_ref[3:4, :] * pre
    for j in range(B_CONV - 1):
        conv = conv + wc_ref[j:j + 1, :] * sc_ref[r, j:j + 1, :]
    cnew_ref[r, 0:1, :] = sc_ref[r, 1:2, :]
    cnew_ref[r, 1:2, :] = sc_ref[r, 2:3, :]
    cnew_ref[r, 2:3, :] = pre
    qkv = jax.nn.silu(conv)
    gate = jax.nn.silu(z_ref[row, :])
    out = []
    yield
    for h in range(B_HEADS):
        lo = h * B_DK
        qcol = column(l2n(qkv[:, lo:lo + B_DK]) * (B_DK ** -0.5))
        kcol = column(l2n(qkv[:, D_MODEL + lo:D_MODEL + lo + B_DK]))
        vh = qkv[:, 2 * D_MODEL + lo:2 * D_MODEL + lo + B_DK]
        s_old = sg_ref[r, h]
        beta = beta_all[:, h:h + 1]
        eg = eg_all[:, 8 + h:9 + h]
        ks = jnp.sum(s_old * kcol, axis=0, keepdims=True)
        u = beta * (vh - eg * ks)
        s_new = s_old * eg + kcol * u
        snew_ref[r, h] = s_new
        yield
        o = jnp.sum(s_new * qcol, axis=0, keepdims=True)
        out.append(_rms(o, ng_ref[...]) * gate[:, lo:lo + B_DK])
        yield
    ob_ref[row, :] = jnp.concatenate(out, axis=1)


XA_REQ = 2


def _xattn1_requests(q_ref, ck_ref, cv_ref, oc_ref, base):
    scale = M_HEAD_DIM ** -0.5
    pairs = (N_MEM // 2, 2 * M_HEADS, M_HEAD_DIM)
    fold = lambda t, op: op(t[0:M_HEADS], t[M_HEADS:2 * M_HEADS])
    twice = lambda t: jnp.concatenate([t, t], axis=0)
    for r in range(XA_REQ):
        q = twice(q_ref[base + r])
        s = jnp.sum(ck_ref[r].reshape(pairs) * q[None], axis=-1, keepdims=True) * scale
        yield
        mx = twice(fold(jnp.max(s, axis=0), jnp.maximum))
        p = jnp.exp(s - mx[None])
        den = twice(fold(jnp.sum(p, axis=0), jnp.add))
        pr = p / den[None]
        yield
        oc_ref[base + r] = fold(jnp.sum(pr * cv_ref[r].reshape(pairs), axis=0), jnp.add)
        yield


def _route_and_dispatch(x1, nf_ref, wrt_ref, br_ref, xs_ref, route_ref, cnt_ref):
    tt = x1.shape[0]
    h2 = _rms(x1, nf_ref[...])
    h_hi, h_mid, h_lo = _split3(h2)
    w_hi, w_mid, w_lo = _split3(wrt_ref[...])
    yield

    def nt(a, b):
        return lax.dot_general(a, b, (((1,), (1,)), ((), ())), preferred_element_type=F32)

    logits = (nt(w_hi, h_hi) + (nt(w_hi, h_mid) + nt(w_mid, h_hi))
              + (nt(w_hi, h_lo) + nt(w_lo, h_hi) + nt(w_mid, h_mid))) + br_ref[:, 0:1]
    yield
    e_id = _iota((LANES, tt), 0).astype(F32)
    neg = jnp.float32(-jnp.inf)
    big = jnp.float32(1 << 20)
    is_grp = (e_id >= N_EXPERTS) & (e_id < N_EXPERTS + N_GROUPS)
    gl = jnp.where(is_grp, logits, neg)
    gmax = jnp.max(gl, axis=0, keepdims=True)
    g_lo = (jnp.min(jnp.where(gl == gmax, e_id, big), axis=0, keepdims=True) - N_EXPERTS) * EXP_PER_GROUP
    p_g = 1.0 / jnp.sum(jnp.exp(gl - gmax), axis=0, keepdims=True)
    yield
    in_grp = (e_id >= g_lo) & (e_id < g_lo + EXP_PER_GROUP)
    el = jnp.where(in_grp, logits, neg)
    v1 = jnp.max(el, axis=0, keepdims=True)
    i1 = jnp.min(jnp.where(el == v1, e_id, big), axis=0, keepdims=True)
    yield
    el2 = jnp.where(e_id == i1, neg, el)
    v2 = jnp.max(el2, axis=0, keepdims=True)
    i2 = jnp.min(jnp.where(el2 == v2, e_id, big), axis=0, keepdims=True)
    yield
    e2 = jnp.exp(v2 - v1)
    w1 = p_g / (1.0 + e2)
    w2 = p_g * e2 / (1.0 + e2)
    oh1 = (e_id == i1).astype(F32)
    oh2 = (e_id == i2).astype(F32)
    a_t = oh1 + oh2
    cnt = jnp.sum(a_t, axis=1, keepdims=True)
    pages = jnp.floor((cnt + (PAGE_ROWS - 1)) * (1.0 / PAGE_ROWS))
    yield
    lower = (_iota((LANES, LANES), 0) > _iota((LANES, LANES), 1)).astype(BF16)
    offp = jnp.dot(lower, jnp.broadcast_to(pages, (LANES, LANES)).astype(BF16),
                   preferred_element_type=F32)[:, 0:1]
    upper = (_iota((tt, tt), 0) < _iota((tt, tt), 1)).astype(BF16)
    rank = jnp.dot(a_t.astype(BF16), upper, preferred_element_type=F32)
    yield
    pos = offp * PAGE_ROWS + rank
    slot1 = jnp.sum(oh1 * pos, axis=0, keepdims=True)
    slot2 = jnp.sum(oh2 * pos, axis=0, keepdims=True)
    yield
    s_id = _iota((TILE_SLOTS, tt), 0).astype(F32)
    sel = jnp.where((s_id == slot1) | (s_id == slot2), 1.0, 0.0).astype(BF16)
    for r0 in range(0, TILE_SLOTS, TILE_SLOTS // 4):
        r1 = r0 + TILE_SLOTS // 4
        xs_ref[r0:r1, :] = jnp.dot(sel[r0:r1], h_hi, preferred_element_type=F32).astype(BF16)
        yield
    r_id = _iota((LANES, tt), 0)
    rows = (jnp.where(r_id == 0, slot1, 0.0) + jnp.where(r_id == 1, slot2, 0.0)
            + jnp.where(r_id == 2, w1, 0.0) + jnp.where(r_id == 3, w2, 0.0))
    route_ref[...] = rows.T
    cnt_ref[...] = jnp.broadcast_to(cnt, (LANES, LANES)).T[0:8, :]


def _merge(x, oa, ob, oc, ga, gb, gc, wo_ref):
    s = ga * oa + gb * ob + gc * oc
    return x + jnp.dot(s.astype(BF16), wo_ref[...], preferred_element_type=F32)


def _mix_kernel(x_ref, au_ref, av_ref, qm_ref, ga_ref, gb_ref, gc_ref, ob0_ref, ob1_ref, mk_ref, mv_ref,
                ws_ref, bst_ref, wo_ref, nf_ref, wrt_ref, br_ref, xq_ref, ck_ref, cv_ref,
                sq_ref, sk_ref, sv_ref, sz_ref, sba_ref, sc_ref, sg_ref, wc_ref, alog_ref, dt_ref, ng_ref,
                x1_ref, xs_ref, route_ref, cnt_ref, xoc_ref, sob_ref, snew_ref, cnew_ref,
                oa_ref, oc_ref, x1_prev_ref):
    i = pl.program_id(0)
    tt = x_ref.shape[0]

    @pl.when(i == 0)
    def _():
        x1_prev_ref[...] = jnp.zeros_like(x1_prev_ref)

    def mix_tile():
        causal = _iota((A_CHUNK, A_CHUNK), 0) >= _iota((A_CHUNK, A_CHUNK), 1)
        for g in range(A_GROUPS):
            wsg = jnp.where(causal, ws_ref[g], 0.0).astype(BF16)
            bcol = bst_ref[:, g:g + 1]
            lo = g * A_CHUNK
            for cc in range(tt // A_CHUNK):
                r0 = cc * A_CHUNK
                sv = jnp.dot(wsg, av_ref[r0:r0 + A_CHUNK, lo:lo + A_CHUNK].astype(BF16),
                             preferred_element_type=F32) + bcol
                oa_ref[r0:r0 + A_CHUNK, lo:lo + A_CHUNK] = au_ref[r0:r0 + A_CHUNK, lo:lo + A_CHUNK] * sv
            yield
        scale = M_HEAD_DIM ** -0.5
        for h in range(M_HEADS):
            lo = h * M_HEAD_DIM
            s = _dot_nt(qm_ref[:, lo:lo + M_HEAD_DIM], mk_ref[:, lo:lo + M_HEAD_DIM]) * scale
            yield
            p = jnp.exp(s - jnp.max(s, axis=-1, keepdims=True))
            pr = p / jnp.sum(p, axis=-1, keepdims=True)
            oc_ref[:, lo:lo + M_HEAD_DIM] = _dot(pr, mv_ref[:, lo:lo + M_HEAD_DIM])
            yield
        n_tiles = pl.num_programs(0) - 1
        in_first_half = jnp.minimum(i, n_tiles - 1) < n_tiles // GDN_SEQS
        ob = jnp.where(in_first_half, ob0_ref[...], ob1_ref[...])
        x1 = _merge(x_ref[...], oa_ref[...], ob, oc_ref[...],
                    ga_ref[...], gb_ref[...], gc_ref[...], wo_ref)
        x1_ref[...] = x1
        x1_prev_ref[i % 2] = x1

    n_req_blocks = xq_ref.shape[0] // XA_REQ
    xa_base = jnp.minimum(i, n_req_blocks - 1) * XA_REQ
    _round_robin([_gdn_step_request(sq_ref, sk_ref, sv_ref, sz_ref, sba_ref, sc_ref, sg_ref, wc_ref,
                                    alog_ref, dt_ref, ng_ref, sob_ref, snew_ref, cnew_ref, xa_base, r)
                  for r in range(XA_REQ)]
                 + [_route_and_dispatch(x1_prev_ref[(i + 1) % 2], nf_ref, wrt_ref, br_ref, xs_ref,
                                        route_ref, cnt_ref),
                    mix_tile(),
                    _xattn1_requests(xq_ref, ck_ref, cv_ref, xoc_ref, xa_base)],
                 turns=[2] * XA_REQ + [1, 1, 1])


def _mix1_kernel(x_ref, au_ref, av_ref, ga_ref, gb_ref, gc_ref, ob_ref, oc_ref, ws0_ref, bs0_ref,
                 wo_ref, nf_ref, wrt_ref, br_ref, xs_in_ref,
                 x1_ref, xs_ref, route_ref, cnt_ref):
    del xs_in_ref
    oa = au_ref[...] * (ws0_ref[...] * av_ref[...] + bs0_ref[...])
    s = ga_ref[...] * oa + gb_ref[...] * ob_ref[...] + gc_ref[...] * oc_ref[...]
    s_hi = s.astype(BF16)
    s_lo = (s - s_hi.astype(F32)).astype(BF16)
    w = wo_ref[...]
    w_hi = w.astype(BF16)
    w_lo = (w - w_hi.astype(F32)).astype(BF16)
    x1 = x_ref[...] + (jnp.dot(s_hi, w_hi, preferred_element_type=F32)
                       + (jnp.dot(s_lo, w_hi, preferred_element_type=F32)
                          + jnp.dot(s_hi, w_lo, preferred_element_type=F32)))
    x1_ref[...] = x1
    for _ in _route_and_dispatch(x1, nf_ref, wrt_ref, br_ref, xs_ref, route_ref, cnt_ref):
        pass


def _route_out_specs(tt, x1_of, slot_of, route_of):
    return [pl.BlockSpec((tt, D_MODEL), lambda i: (x1_of(i), 0)),
            pl.BlockSpec((None, TILE_SLOTS, D_MODEL), lambda i: (slot_of(i), 0, 0)),
            pl.BlockSpec((tt, LANES), lambda i: (route_of(i), 0)),
            pl.BlockSpec((None, 8, LANES), lambda i: (route_of(i), 0, 0))]


def _mix(x2d, p, ob, mk, mv, w_s, bs_t, wo_bf16, norm_ffn, wr_t, br_col, seq,
         p_sample, ba_sample, cache_k, cache_v, state_conv, state_gdn, w_conv, alog_row, dt_row, norm_gdn):
    n_s = p_sample.shape[0]
    xq = p_sample[:, 6 * D_MODEL:7 * D_MODEL].reshape(n_s, M_HEADS, M_HEAD_DIM)
    n_req_blocks = n_s // XA_REQ
    req_blk = lambda i: jnp.minimum(i, n_req_blocks - 1)
    cache_spec = pl.BlockSpec((XA_REQ, N_MEM, M_HEADS, M_HEAD_DIM), lambda i: (req_blk(i), 0, 0, 0))
    whole_q = pl.BlockSpec((n_s, M_HEADS, M_HEAD_DIM), lambda i: (0, 0, 0))
    s_col = lambda colblk: pl.BlockSpec((n_s, D_MODEL), lambda i: (0, colblk))
    conv_spec = pl.BlockSpec((XA_REQ, B_CONV - 1, C_QKV), lambda i: (req_blk(i), 0, 0))
    state_spec = pl.BlockSpec((XA_REQ, B_HEADS, B_DK, B_DK), lambda i: (req_blk(i), 0, 0, 0))
    rows = x2d.shape[0]
    assert n_req_blocks <= rows // MIX_TILE + 1 and n_s % XA_REQ == 0
    tt = MIX_TILE
    nt = rows // tt
    per_b = seq // tt
    cur = lambda i: jnp.minimum(i, nt - 1)
    prev = lambda i: jnp.maximum(i - 1, 0)
    pblk = lambda colblk: pl.BlockSpec((tt, D_MODEL), lambda i: (cur(i), colblk))
    const = lambda shape: pl.BlockSpec(shape, lambda i: (0,) * len(shape))
    nh = nt // GDN_SEQS
    ob_blk = lambda g: pl.BlockSpec((tt, D_MODEL), lambda i: (jnp.clip(cur(i) - g * nh, 0, nh - 1), 0))
    x1, xs_all, route, cnt, xoc, sob, snew, cnew = pl.pallas_call(
        _mix_kernel,
        grid=(nt + 1,),
        in_specs=[pblk(0), pblk(0), pblk(1), pblk(6), pblk(7), pblk(8), pblk(9), ob_blk(0), ob_blk(1),
                  pl.BlockSpec((N_MEM, D_MODEL), lambda i: (cur(i) // per_b, 0)),
                  pl.BlockSpec((N_MEM, D_MODEL), lambda i: (cur(i) // per_b, 0)),
                  const((A_GROUPS, A_CHUNK, A_CHUNK)), const((A_CHUNK, LANES)),
                  const((D_MODEL, D_MODEL)), const((1, D_MODEL)),
                  const((LANES, D_MODEL)), const((LANES, LANES)),
                  whole_q, cache_spec, cache_spec,
                  s_col(2), s_col(3), s_col(4), s_col(5), const((n_s, LANES)), conv_spec, state_spec,
                  const((B_CONV, C_QKV)), const((1, LANES)), const((1, LANES)), const((1, B_DK))],
        out_specs=_route_out_specs(tt, cur, lambda i: i, prev)
        + [whole_q, const((n_s, D_MODEL)), state_spec, conv_spec],
        out_shape=[jax.ShapeDtypeStruct((rows, D_MODEL), F32),
                   jax.ShapeDtypeStruct((nt + 1, TILE_SLOTS, D_MODEL), BF16),
                   jax.ShapeDtypeStruct((rows, LANES), F32),
                   jax.ShapeDtypeStruct((nt, 8, LANES), F32),
                   jax.ShapeDtypeStruct((n_s, M_HEADS, M_HEAD_DIM), F32),
                   jax.ShapeDtypeStruct((n_s, D_MODEL), F32),
                   jax.ShapeDtypeStruct((n_s, B_HEADS, B_DK, B_DK), F32),
                   jax.ShapeDtypeStruct((n_s, B_CONV - 1, C_QKV), F32)],
        scratch_shapes=[pltpu.VMEM((tt, D_MODEL), F32), pltpu.VMEM((tt, D_MODEL), F32),
                        pltpu.VMEM((2, tt, D_MODEL), F32)],
        compiler_params=_params(1),
        name="mix",
    )(x2d, p, p, p, p, p, p, ob[0], ob[1], mk, mv, w_s, bs_t, wo_bf16, norm_ffn, wr_t, br_col,
      xq, cache_k, cache_v,
      p_sample, p_sample, p_sample, p_sample, ba_sample, state_conv, state_gdn,
      w_conv, alog_row, dt_row, norm_gdn)
    return x1, xs_all, route, cnt, xoc.reshape(n_s, D_MODEL), sob, snew, cnew


def _mix1(x2d, p, ob, oc, ws0_row, bs0_row, wo_f32, norm_ffn, wr_t, br_col, xs_all):
    rows = x2d.shape[0]
    pblk = lambda colblk: pl.BlockSpec((rows, D_MODEL), lambda i: (0, colblk))
    const = lambda shape: pl.BlockSpec(shape, lambda i: (0,) * len(shape))
    zero = lambda i: 0
    return pl.pallas_call(
        _mix1_kernel,
        grid=(1,),
        in_specs=[pblk(0), pblk(0), pblk(1), pblk(7), pblk(8), pblk(9), pblk(0), pblk(0),
                  const((1, D_MODEL)), const((1, D_MODEL)),
                  const((D_MODEL, D_MODEL)), const((1, D_MODEL)),
                  const((LANES, D_MODEL)), const((LANES, LANES)),
                  pl.BlockSpec(memory_space=pl.ANY)],
        out_specs=_route_out_specs(rows, zero, zero, zero),
        out_shape=[jax.ShapeDtypeStruct((rows, D_MODEL), F32),
                   jax.ShapeDtypeStruct(xs_all.shape, BF16),
                   jax.ShapeDtypeStruct((rows, LANES), F32),
                   jax.ShapeDtypeStruct((1, 8, LANES), F32)],
        input_output_aliases={14: 1},
        compiler_params=_params(1),
        name="mix1",
    )(x2d, p, p, p, p, p, ob, oc, ws0_row, bs0_row, wo_f32, norm_ffn, wr_t, br_col, xs_all)


def _experts_kernel(tbl_ref, se_ref, nu_ref, *refs):
    del tbl_ref
    pages = refs[:STEP_PAGES]
    wg_ref, wu_ref, wd_ref, o_ref, wgb_ref, wub_ref, wdb_ref = refs[STEP_PAGES:]
    s = pl.program_id(0)
    prev = se_ref[jnp.maximum(s - 1, 0)]

    @pl.when((s == 0) | (se_ref[s] != prev))
    def _():
        wgb_ref[...] = wg_ref[...].astype(BF16)
        wub_ref[...] = wu_ref[...].astype(BF16)
        wdb_ref[...] = wd_ref[...].astype(BF16)

    @pl.when(s < nu_ref[0])
    def _():
        x = jnp.concatenate([pg[...] for pg in pages], axis=0)
        gate = jnp.dot(x, wgb_ref[...], preferred_element_type=F32)
        up = jnp.dot(x, wub_ref[...], preferred_element_type=F32)
        act = (jax.nn.silu(gate) * up).astype(BF16)
        o_ref[...] = jnp.dot(act, wdb_ref[...], preferred_element_type=F32).astype(BF16)

    @pl.when(s >= nu_ref[0])
    def _():
        o_ref[...] = jnp.zeros_like(o_ref)


def _experts(tbl, step_e, n_used, xs_pages, w_gate, w_up, w_down, n_steps):
    def page_spec(k):
        return pl.BlockSpec((None, PAGE_ROWS, D_MODEL),
                            lambda s, tbl, se, nu: (tbl[s * STEP_PAGES + k], 0, 0))

    wspec = lambda shape: pl.BlockSpec((None,) + shape, lambda s, tbl, se, nu: (se[s], 0, 0))
    grid_spec = pltpu.PrefetchScalarGridSpec(
        num_scalar_prefetch=3,
        grid=(n_steps,),
        in_specs=[page_spec(k) for k in range(STEP_PAGES)]
        + [wspec((D_MODEL, D_EXPERT)), wspec((D_MODEL, D_EXPERT)), wspec((D_EXPERT, D_MODEL))],
        out_specs=pl.BlockSpec((STEP_ROWS, D_MODEL), lambda s, tbl, se, nu: (s, 0)),
        scratch_shapes=[pltpu.VMEM((D_MODEL, D_EXPERT), BF16), pltpu.VMEM((D_MODEL, D_EXPERT), BF16),
                        pltpu.VMEM((D_EXPERT, D_MODEL), BF16)],
    )
    return pl.pallas_call(
        _experts_kernel,
        grid_spec=grid_spec,
        out_shape=jax.ShapeDtypeStruct((n_steps * STEP_ROWS, D_MODEL), BF16),
        compiler_params=_params(1),
        name="experts",
    )(tbl, step_e, n_used, *([xs_pages] * STEP_PAGES), w_gate, w_up, w_down)


def _combine_kernel(inv_ref, *refs):
    del inv_ref
    pages = refs[:TILE_PAGES]
    x1_ref, route_ref, nfin_ref, y_ref = refs[TILE_PAGES:]
    out_loc = jnp.concatenate([pg[...] for pg in pages], axis=0)
    tt = x1_ref.shape[0]
    route = route_ref[...]
    s_id = _iota((tt, TILE_SLOTS), 1).astype(F32)
    sel = (jnp.where(s_id == route[:, 0:1], route[:, 2:3], 0.0)
           + jnp.where(s_id == route[:, 1:2], route[:, 3:4], 0.0)).astype(BF16)
    moe = jnp.dot(sel, out_loc, preferred_element_type=F32)
    y_ref[...] = _rms(x1_ref[...] + moe, nfin_ref[...])


def _combine(inv, out_pages, x1, route, norm_final, tt):
    rows = x1.shape[0]

    def page_spec(k):
        return pl.BlockSpec((None, PAGE_ROWS, D_MODEL), lambda i, inv: (inv[i * TILE_PAGES + k], 0, 0))

    grid_spec = pltpu.PrefetchScalarGridSpec(
        num_scalar_prefetch=1,
        grid=(rows // tt,),
        in_specs=[page_spec(k) for k in range(TILE_PAGES)]
        + [pl.BlockSpec((tt, D_MODEL), lambda i, inv: (i, 0)),
           pl.BlockSpec((tt, LANES), lambda i, inv: (i, 0)),
           pl.BlockSpec((1, D_MODEL), lambda i, inv: (0, 0))],
        out_specs=pl.BlockSpec((tt, D_MODEL), lambda i, inv: (i, 0)),
    )
    return pl.pallas_call(
        _combine_kernel,
        grid_spec=grid_spec,
        out_shape=jax.ShapeDtypeStruct((rows, D_MODEL), F32),
        compiler_params=_params(1),
        name="combine",
    )(inv, *([out_pages] * TILE_PAGES), x1, route, norm_final)


def _page_tables(cnt, n_steps):
    n_tiles = cnt.shape[0]
    pg = (cnt + (PAGE_ROWS - 1)) // PAGE_ROWS
    lend = jnp.cumsum(pg, axis=1)
    loff = lend - pg
    cum_t = jnp.cumsum(pg, axis=0)
    pref = cum_t - pg
    tot = cum_t[-1]
    totp = ((tot + (STEP_PAGES - 1)) // STEP_PAGES) * STEP_PAGES
    gend = jnp.cumsum(totp)
    gstart = gend - totp
    sel = lambda onehot, vals: jnp.dot(onehot, vals.astype(F32), precision=lax.Precision.HIGHEST)
    gp = jnp.arange(n_steps * STEP_PAGES, dtype=jnp.int32)
    in_e = (gp[:, None] >= gstart[None, :]) & (gp[:, None] < gend[None, :])
    oh_e = in_e.astype(F32)
    r = gp.astype(F32) - sel(oh_e, gstart)
    valid = jnp.any(in_e, axis=1) & (r < sel(oh_e, tot))
    cum_col = sel(oh_e, cum_t.T)
    pref_col = sel(oh_e, pref.T)
    loff_col = sel(oh_e, loff.T)
    in_i = (r[:, None] >= pref_col) & (r[:, None] < cum_col)
    tile_base = (jnp.arange(n_tiles, dtype=jnp.int32) * TILE_PAGES).astype(F32)
    src = jnp.sum(jnp.where(in_i, tile_base[None, :] + loff_col + (r[:, None] - pref_col), 0.0), axis=1)
    tbl = jnp.where(valid, src, 0.0).astype(jnp.int32)
    in_step = in_e[::STEP_PAGES]
    e_ids = jnp.arange(N_EXPERTS, dtype=jnp.int32)
    step_e = jnp.where(jnp.any(in_step, axis=1), jnp.sum(jnp.where(in_step, e_ids[None, :], 0), axis=1),
                       N_EXPERTS - 1).astype(jnp.int32)
    n_used = (gend[-1] // STEP_PAGES).astype(jnp.int32).reshape(1)
    lp = jnp.arange(TILE_PAGES, dtype=jnp.int32)[None, :, None]
    in_l = (lp >= loff[:, None, :]) & (lp < lend[:, None, :])
    gpos = jnp.sum(jnp.where(in_l, gstart[None, None, :] + pref[:, None, :] + lp - loff[:, None, :], 0), axis=2)
    zero_page = (n_steps - 1) * STEP_PAGES
    inv = jnp.where(jnp.any(in_l, axis=2), gpos, zero_page).astype(jnp.int32).reshape(-1)
    return tbl, step_e, n_used, inv


def kernel(x_prompt, x_sample, mem_prompt, cache_mem_k, cache_mem_v, state_gdn, state_conv,
           norm_mix, w_in, b_gate, w_s, b_s, norm_a_v, w_conv, a_log, dt_bias, norm_gdn_out,
           norm_mem, w_mem_kv, w_o, norm_ffn, w_router_group, b_router_group, w_router_expert,
           b_router_expert, w_exp_gate, w_exp_up, w_exp_down, norm_final):
    depth = norm_mix.shape[0]
    assert depth == 1
    bsz, seq, _ = x_prompt.shape
    n_s = x_sample.shape[0]
    assert x_sample.shape[1] == 1 and seq % MIX_TILE == 0 and n_s % 8 == 0 and n_s <= MIX_TILE
    assert bsz % GDN_SEQS == 0
    l = 0
    row = lambda v: v.reshape(1, -1)

    wi = w_in[l]
    o_z = 2 * D_MODEL + C_QKV
    o_beta = o_z + D_MODEL
    o_qm = o_beta + 2 * B_HEADS
    o_gate = o_qm + D_MODEL
    assert o_beta == PROJ_HEAD_TILES * D_MODEL
    w_tail = wi[:, o_qm:]
    w_main = jnp.concatenate([wi[:, :o_beta].astype(BF16), w_tail.astype(BF16)], axis=1)
    w_ba_f32 = jnp.pad(wi[:, o_beta:o_qm], ((0, 0), (0, LANES - 2 * B_HEADS)))
    w_ba = w_ba_f32.astype(BF16)
    pad_heads = lambda v: jnp.pad(v.reshape(1, B_HEADS), ((0, 0), (B_HEADS, LANES - 2 * B_HEADS)))
    alog_row = pad_heads(a_log[l])
    dt_row = pad_heads(dt_bias[l])
    w_kv = w_mem_kv[l].astype(BF16)
    wo = w_o[l].astype(BF16)
    wr_t = jnp.pad(jnp.concatenate([w_router_expert[l], w_router_group[l]], axis=1).T,
                   ((0, LANES - N_EXPERTS - N_GROUPS), (0, 0)))
    br_col = jnp.pad(jnp.concatenate([b_router_expert[l], b_router_group[l]]).reshape(-1, 1),
                     ((0, LANES - N_EXPERTS - N_GROUPS), (0, LANES - 1)))
    bs_t = jnp.pad(b_s[l].T, ((0, 0), (0, LANES - A_GROUPS)))
    ws0_row = jnp.repeat(w_s[l][:, 0, 0], A_CHUNK).reshape(1, D_MODEL)
    bs0_row = jnp.repeat(b_s[l][:, 0], A_CHUNK).reshape(1, D_MODEL)

    xp = x_prompt.reshape(bsz * seq, D_MODEL)
    xs_ = x_sample.reshape(n_s, D_MODEL)
    n_tiles_p = (bsz * seq) // MIX_TILE
    n_tiles = n_tiles_p + 1

    p_s, ba_s = _proj2(xs_, row(norm_mix[l]), wi, w_tail, w_ba_f32, row(norm_a_v[l]), row(b_gate[l]))
    mk, mv = _memkv(mem_prompt.reshape(bsz * N_MEM, D_MODEL), row(norm_mem[l]), w_kv)
    p_p, ba_p = _proj(xp, row(norm_mix[l]), w_main, w_ba, row(norm_a_v[l]), row(b_gate[l]),
                      tm=min(PROJ_ROWS, bsz * seq))
    ob_p, s_p = _gdn(p_p, ba_p, w_conv[l], alog_row, dt_row, row(norm_gdn_out[l]), bsz, seq)
    x1_p, xs_all, route_p, cnt_p, oc_s, ob_s, s_s, c_s = _mix(
        xp, p_p, ob_p, mk, mv, w_s[l], bs_t, wo, row(norm_ffn[l]), wr_t, br_col, seq,
        p_s, ba_s, cache_mem_k[l], cache_mem_v[l], state_conv[l], state_gdn[l], w_conv[l], alog_row, dt_row,
        row(norm_gdn_out[l]))
    x1_s, xs_all, route_s, cnt_s = _mix1(xs_, p_s, ob_s, oc_s, ws0_row, bs0_row, w_o[l], row(norm_ffn[l]),
                                         wr_t, br_col, xs_all)
    cnt = jnp.concatenate([cnt_s[:, 0, :N_EXPERTS], cnt_p[:, 0, :N_EXPERTS]], axis=0).astype(jnp.int32)
    max_pages = n_tiles_p * TILE_PAGES + (2 * n_s) // PAGE_ROWS + N_EXPERTS
    n_steps = (max_pages + N_EXPERTS * (STEP_PAGES - 1)) // STEP_PAGES + 1
    tbl, step_e, n_used, inv = _page_tables(cnt, n_steps)
    ne = N_GROUPS * EXP_PER_GROUP
    out_sorted = _experts(tbl, step_e, n_used, xs_all.reshape(n_tiles * TILE_PAGES, PAGE_ROWS, D_MODEL),
                          w_exp_gate[l].reshape(ne, D_MODEL, D_EXPERT),
                          w_exp_up[l].reshape(ne, D_MODEL, D_EXPERT),
                          w_exp_down[l].reshape(ne, D_EXPERT, D_MODEL), n_steps)
    out_pages = out_sorted.reshape(n_steps * STEP_PAGES, PAGE_ROWS, D_MODEL)
    y_s = _combine(inv[:TILE_PAGES], out_pages, x1_s, route_s, row(norm_final), n_s)
    y_p = _combine(inv[TILE_PAGES:], out_pages, x1_p, route_p, row(norm_final), MIX_TILE)

    conv_tail = p_p.reshape(bsz, seq, PROJ_COLS)[:, seq - (B_CONV - 1):, 2 * D_MODEL:2 * D_MODEL + C_QKV]
    return (y_p.reshape(bsz, seq, D_MODEL),
            y_s.reshape(n_s, 1, D_MODEL),
            mk.reshape(1, bsz, N_MEM, M_HEADS, M_HEAD_DIM),
            mv.reshape(1, bsz, N_MEM, M_HEADS, M_HEAD_DIM),
            jnp.concatenate(s_p, axis=0)[None],
            conv_tail[None],
            s_s[None],
            c_s[None],
            p_s[:, D_MODEL:2 * D_MODEL].reshape(1, n_s, 1, D_MODEL))
```
